```python
import math
import jax
import jax.numpy as jnp
from jax import lax
import numpy as np

D_MODEL = 2048
BATCH = 2
SEQ = 8192
DEPTH = 4

GRID_W = 64
CTX_LEN = 256
EPS = 1e-6
N_MOD = 6

DIFF_HEADS = 12
DIFF_DIM = 64
DIFF_V = 2 * DIFF_DIM
EV_Q = DIFF_HEADS * 2 * DIFF_DIM
EV_V = DIFF_HEADS * DIFF_V
FOURIER_GROUPS = 4
FOURIER_DIM = 128
FOURIER_WIDTH = FOURIER_GROUPS * FOURIER_DIM
EV_IN = 2 * EV_Q + EV_V + FOURIER_WIDTH
EV_MIX = EV_V + FOURIER_WIDTH
ROPE_BASE = 10000.0
Q_BLOCK = 128

CONV_WIDTH = 1024
CONV_K = 3
NA_HEADS = 8
NA_DIM = 128
NA_WIDTH = NA_HEADS * NA_DIM
NA_WIN_ROWS = 8
NA_WIN_COLS = 16
OD_IN = 3 * CONV_WIDTH + 3 * NA_WIDTH
OD_MIX = CONV_WIDTH + NA_WIDTH

FFN_HIDDEN = -(-8 * D_MODEL // (3 * 256)) * 256
N_EVEN = (DEPTH + 1) // 2
N_ODD = DEPTH // 2

kernel_name = "hybrid_diffattn_fourier_shortconv_natten_dit"


def rms_norm(x, g):
    xf = x.astype(jnp.float32)
    y = xf * lax.rsqrt(jnp.mean(xf * xf, axis=-1, keepdims=True) + EPS)
    return (y * g.astype(jnp.float32)).astype(x.dtype)


def modulate(h, shift, scale):
    return h * (1.0 + scale) + shift


def swiglu(u, w_in, w_out):
    gu = u @ w_in
    return (jax.nn.silu(gu[..., :FFN_HIDDEN]) * gu[..., FFN_HIDDEN:]) @ w_out


def axial_rope(n_tok):
    t = jnp.arange(n_tok)
    rows = (t // GRID_W).astype(jnp.float32)
    cols = (t % GRID_W).astype(jnp.float32)
    n_freq = DIFF_DIM // 4
    inv = ROPE_BASE ** (-jnp.arange(n_freq, dtype=jnp.float32) / n_freq)
    ang = jnp.stack([rows[:, None] * inv, cols[:, None] * inv], axis=1)
    return jnp.cos(ang), jnp.sin(ang)


def apply_rope(x, cos, sin):
    xr = x.astype(jnp.float32).reshape(x.shape[:-1] + (2, 2, DIFF_DIM // 4))
    x1, x2 = xr[..., 0, :], xr[..., 1, :]
    bshape = (cos.shape[0],) + (1,) * (x.ndim - 3) + cos.shape[1:]
    c, s = cos.reshape(bshape), sin.reshape(bshape)
    out = jnp.stack([x1 * c - x2 * s, x2 * c + x1 * s], axis=-2)
    return out.reshape(x.shape).astype(x.dtype)


def diff_weights(s, lam, v, scale):
    p = jax.nn.softmax(s.astype(jnp.float32) * scale, axis=-1)
    a = p[:, :, 0] - lam * p[:, :, 1]
    return jnp.einsum("bhqk,bkhd->bqhd", a.astype(v.dtype), v)


def fourier_mix(u):
    b, n, _ = u.shape
    ug = u.astype(jnp.float32).reshape(b, n, FOURIER_GROUPS, FOURIER_DIM)
    out = jnp.fft.fft2(ug, axes=(1, 3), norm="ortho").real
    return out.reshape(b, n, FOURIER_WIDTH).astype(u.dtype)


def short_conv(u, w):
    n = u.shape[1]
    pad = CONV_K // 2
    up = jnp.pad(u, ((0, 0), (pad, pad), (0, 0)))
    out = up[:, 0:n] * w[0]
    for j in range(1, CONV_K):
        out = out + up[:, j:j + n] * w[j]
    return out


def even_mixer(a_lat, a_ctx, w_in, w_out, qk_gain, lam_vec, subln_gain, lam_init, ctx_out):
    b, n, _ = a_lat.shape
    n_ctx = a_ctx.shape[1]
    scale = DIFF_DIM ** -0.5
    lamf = lam_vec.astype(jnp.float32)
    lam = jnp.exp(jnp.sum(lamf[0] * lamf[1])) - jnp.exp(jnp.sum(lamf[2] * lamf[3])) + lam_init

    def qk(t, g, m):
        return rms_norm(t.reshape(b, m, DIFF_HEADS, 2, DIFF_DIM), g)

    def vv(t, m):
        return t.reshape(b, m, DIFF_HEADS, DIFF_V)

    def head_out(o):
        return (rms_norm(o, subln_gain) * (1.0 - lam_init)).reshape(o.shape[0], o.shape[1], EV_V)

    p = a_lat @ w_in
    q = qk(p[..., :EV_Q], qk_gain[0], n)
    k = qk(p[..., EV_Q:2 * EV_Q], qk_gain[1], n)
    v = vv(p[..., 2 * EV_Q:2 * EV_Q + EV_V], n)
    f = p[..., 2 * EV_Q + EV_V:]
    cos, sin = axial_rope(n)
    q_rot = apply_rope(q, cos, sin)
    k_rot = apply_rope(k, cos, sin)

    if ctx_out:
        pc = a_ctx @ w_in
        kvc = pc[..., EV_Q:2 * EV_Q + EV_V]
    else:
        kvc = a_ctx @ w_in[:, EV_Q:2 * EV_Q + EV_V]
    kc = qk(kvc[..., :EV_Q], qk_gain[1], n_ctx)
    vc = vv(kvc[..., EV_Q:], n_ctx)
    v_all = jnp.concatenate([vc, v], axis=1)

    nb = n // Q_BLOCK

    def to_blocks(t):
        return jnp.moveaxis(t.reshape((b, nb, Q_BLOCK) + t.shape[2:]), 1, 0)

    def block(args):
        qr, qp = args
        s = jnp.concatenate([
            jnp.einsum("bqhcd,bkhcd->bhcqk", qp, kc),
            jnp.einsum("bqhcd,bkhcd->bhcqk", qr, k_rot)], axis=-1)
        return diff_weights(s, lam, v_all, scale)

    o = lax.map(block, (to_blocks(q_rot), to_blocks(q)))
    o = jnp.moveaxis(o, 0, 1).reshape(b, n, DIFF_HEADS, DIFF_V)
    y = jnp.concatenate([head_out(o), fourier_mix(f)], axis=-1) @ w_out
    if not ctx_out:
        return y, None
    qc = qk(pc[..., :EV_Q], qk_gain[0], n_ctx)
    oc = diff_weights(jnp.einsum("bqhcd,bkhcd->bhcqk", qc, kc), lam, vc, scale)
    yc = jnp.concatenate([head_out(oc), fourier_mix(pc[..., 2 * EV_Q + EV_V:])], axis=-1) @ w_out
    return y, yc


def neighborhood_attention(q, k, v, kc, vc, rpb):
    b, n, nh, d = q.shape
    n_ctx = kc.shape[1]
    rows = n // GRID_W
    wr = min(NA_WIN_ROWS, rows)
    scale = d ** -0.5
    qg = q.reshape(b, rows, GRID_W, nh, d)
    kg = k.reshape(b, rows, GRID_W, nh, d)
    vg = v.reshape(b, rows, GRID_W, nh, d)
    col = jnp.arange(GRID_W)
    cs = jnp.clip(col - NA_WIN_COLS // 2, 0, GRID_W - NA_WIN_COLS)
    col_idx = cs[:, None] + jnp.arange(NA_WIN_COLS)
    dc = col_idx - col[:, None] + (NA_WIN_COLS - 1)
    rpbf = rpb.astype(jnp.float32)

    def row_fn(r):
        rs = jnp.clip(r - wr // 2, 0, rows - wr)
        q_r = lax.dynamic_index_in_dim(qg, r, axis=1, keepdims=False)
        k_r = lax.dynamic_slice_in_dim(kg, rs, wr, axis=1)[:, :, col_idx]
        v_r = lax.dynamic_slice_in_dim(vg, rs, wr, axis=1)[:, :, col_idx]
        dr = rs + jnp.arange(wr) - r + (NA_WIN_ROWS - 1)
        bias = rpbf[:, dr[None, :, None], dc[:, None, :]]
        s_nb = jnp.einsum("bchd,bwcjhd->bhcwj", q_r, k_r).astype(jnp.float32) * scale + bias
        s_cx = jnp.einsum("bchd,bkhd->bhck", q_r, kc).astype(jnp.float32) * scale
        s = jnp.concatenate([s_cx, s_nb.reshape(b, nh, GRID_W, wr * NA_WIN_COLS)], axis=-1)
        p = jax.nn.softmax(s, axis=-1).astype(v.dtype)
        p_cx = p[..., :n_ctx]
        p_nb = p[..., n_ctx:].reshape(b, nh, GRID_W, wr, NA_WIN_COLS)
        return (jnp.einsum("bhcwj,bwcjhd->bchd", p_nb, v_r)
                + jnp.einsum("bhck,bkhd->bchd", p_cx, vc))

    out = lax.map(row_fn, jnp.arange(rows))
    return jnp.moveaxis(out, 0, 1).reshape(b, n, nh * d)


def odd_mixer(a_lat, a_ctx, w_in, w_out, qk_gain, conv_w, rpb, ctx_out):
    b, n, _ = a_lat.shape
    n_ctx = a_ctx.shape[1]
    o_q = 3 * CONV_WIDTH
    o_kv = o_q + NA_WIDTH
    scale = NA_DIM ** -0.5

    def conv_branch(p):
        gb = p[..., :CONV_WIDTH]
        gc = p[..., CONV_WIDTH:2 * CONV_WIDTH]
        hh = p[..., 2 * CONV_WIDTH:o_q]
        return gb * short_conv(gc * hh, conv_w)

    def heads(t, m):
        return t.reshape(b, m, NA_HEADS, NA_DIM)

    p = a_lat @ w_in
    q = rms_norm(heads(p[..., o_q:o_kv], n), qk_gain[0])
    k = rms_norm(heads(p[..., o_kv:o_kv + NA_WIDTH], n), qk_gain[1])
    v = heads(p[..., o_kv + NA_WIDTH:], n)
    if ctx_out:
        pc = a_ctx @ w_in
        kvc = pc[..., o_kv:]
    else:
        kvc = a_ctx @ w_in[:, o_kv:]
    kc = rms_norm(heads(kvc[..., :NA_WIDTH], n_ctx), qk_gain[1])
    vc = heads(kvc[..., NA_WIDTH:], n_ctx)
    na = neighborhood_attention(q, k, v, kc, vc, rpb)
    y = jnp.concatenate([conv_branch(p), na], axis=-1) @ w_out
    if not ctx_out:
        return y, None
    qc = rms_norm(heads(pc[..., o_q:o_kv], n_ctx), qk_gain[0])
    pcx = jax.nn.softmax(jnp.einsum("bqhd,bkhd->bhqk", qc, kc).astype(jnp.float32) * scale, axis=-1)
    oc = jnp.einsum("bhqk,bkhd->bqhd", pcx.astype(vc.dtype), vc).reshape(b, n_ctx, NA_WIDTH)
    yc = jnp.concatenate([conv_branch(pc), oc], axis=-1) @ w_out
    return y, yc


def setup_inputs(seed: int = 0) -> dict:
    key = jax.random.key(seed)
    ks = jax.random.split(key, 20)

    def nrm(k, shape, scale):
        return jax.random.normal(k, shape, jnp.float32) * scale

    return {
        "x": nrm(ks[0], (BATCH, SEQ, D_MODEL), 1.0),
        "c": nrm(ks[1], (BATCH, D_MODEL), 1.0),
        "ctx": nrm(ks[2], (BATCH, CTX_LEN, D_MODEL), 1.0),
        "c_ctx": nrm(ks[3], (D_MODEL,), 1.0),
        "w_mod": nrm(ks[4], (DEPTH, D_MODEL, N_MOD * D_MODEL), D_MODEL ** -0.5),
        "b_mod": nrm(ks[5], (DEPTH, N_MOD * D_MODEL), 0.02),
        "norm_gain": 1.0 + nrm(ks[6], (DEPTH, 2, D_MODEL), 0.02),
        "w_ffn_in": nrm(ks[7], (DEPTH, D_MODEL, 2 * FFN_HIDDEN), D_MODEL ** -0.5),
        "w_ffn_out": nrm(ks[8], (DEPTH, FFN_HIDDEN, D_MODEL), FFN_HIDDEN ** -0.5),
        "ev_w_in": nrm(ks[9], (N_EVEN, D_MODEL, EV_IN), D_MODEL ** -0.5),
        "ev_w_out": nrm(ks[10], (N_EVEN, EV_MIX, D_MODEL), EV_MIX ** -0.5),
        "ev_qk_gain": 1.0 + nrm(ks[11], (N_EVEN, 2, DIFF_DIM), 0.02),
        "ev_lambda": nrm(ks[12], (N_EVEN, 4, DIFF_DIM), 0.1),
        "ev_subln_gain": 1.0 + nrm(ks[13], (N_EVEN, DIFF_V), 0.02),
        "od_w_in": nrm(ks[14], (N_ODD, D_MODEL, OD_IN), D_MODEL ** -0.5),
        "od_w_out": nrm(ks[15], (N_ODD, OD_MIX, D_MODEL), OD_MIX ** -0.5),
        "od_qk_gain": 1.0 + nrm(ks[16], (N_ODD, 2, NA_DIM), 0.02),
        "od_conv_w": nrm(ks[17], (N_ODD, CONV_K, CONV_WIDTH), CONV_K ** -0.5),
        "od_rpb": nrm(ks[18], (N_ODD, NA_HEADS, 2 * NA_WIN_ROWS - 1, 2 * NA_WIN_COLS - 1), 0.1),
    }


def reference(x, c, ctx, c_ctx, w_mod, b_mod, norm_gain, w_ffn_in, w_ffn_out,
              ev_w_in, ev_w_out, ev_qk_gain, ev_lambda, ev_subln_gain,
              od_w_in, od_w_out, od_qk_gain, od_conv_w, od_rpb):
    b = x.shape[0]
    h, hc = x, ctx
    s_lat = jax.nn.silu(c)
    s_ctx = jax.nn.silu(c_ctx)
    for i in range(DEPTH):
        ctx_out = i < DEPTH - 1
        m = (s_lat @ w_mod[i] + b_mod[i]).reshape(b, N_MOD, 1, D_MODEL)
        mc = (s_ctx @ w_mod[i] + b_mod[i]).reshape(N_MOD, D_MODEL)
        a_lat = modulate(rms_norm(h, norm_gain[i, 0]), m[:, 0], m[:, 1])
        a_ctx = modulate(rms_norm(hc, norm_gain[i, 0]), mc[0], mc[1])
        j = i // 2
        if i % 2 == 0:
            lam_init = 0.8 - 0.6 * math.exp(-0.3 * i)
            y, yc = even_mixer(a_lat, a_ctx, ev_w_in[j], ev_w_out[j], ev_qk_gain[j],
                               ev_lambda[j], ev_subln_gain[j], lam_init, ctx_out)
        else:
            y, yc = odd_mixer(a_lat, a_ctx, od_w_in[j], od_w_out[j], od_qk_gain[j],
                              od_conv_w[j], od_rpb[j], ctx_out)
        h = h + m[:, 2] * y
        h = h + m[:, 5] * swiglu(modulate(rms_norm(h, norm_gain[i, 1]), m[:, 3], m[:, 4]),
                                 w_ffn_in[i], w_ffn_out[i])
        if ctx_out:
            hc = hc + mc[2] * yc
            hc = hc + mc[5] * swiglu(modulate(rms_norm(hc, norm_gain[i, 1]), mc[3], mc[4]),
                                     w_ffn_in[i], w_ffn_out[i])
    return h
```

```python
import functools
import math

import jax
import jax.numpy as jnp
import numpy as np
from jax import lax
from jax.experimental import pallas as pl
from jax.experimental.pallas import tpu as pltpu

GRID_W = 64
EPS = 1e-6
N_MOD = 6

DIFF_HEADS = 12
DIFF_DIM = 64
DIFF_V = 2 * DIFF_DIM
FOURIER_GROUPS = 4
FOURIER_DIM = 128
ROPE_BASE = 10000.0

CONV_WIDTH = 1024
CONV_K = 3
NA_HEADS = 8
NA_DIM = 128
NA_WIN_ROWS = 8
NA_WIN_COLS = 16

F32 = jnp.float32
BF16 = jnp.bfloat16
LOG2E = 1.4426950408889634
NEG_BIG = -1e30
LANES = 128
MXU_DIM = 256
VMEM_LIMIT = 56 * 1024 * 1024


def _cparams(*sem):
    return pltpu.CompilerParams(dimension_semantics=sem, vmem_limit_bytes=VMEM_LIMIT)


def _mm(a, b):
    return jnp.dot(a, b, preferred_element_type=F32)


def _mm_nt(a, b):
    return lax.dot_general(a, b, (((1,), (1,)), ((), ())), preferred_element_type=F32)


def _tile_lanes(x, n):
    return x if n == 1 else jnp.concatenate([x] * n, axis=1)


def _mod_kernel(c_ref, w_ref, b_ref, o_ref):
    c = c_ref[...]
    s = (c * jax.nn.sigmoid(c)).astype(BF16)
    o_ref[0] = _mm(s, w_ref[0].astype(BF16)) + b_ref[0]


def _mod_vectors(crow, w_mod, b_mod):
    depth, d, nm = w_mod.shape
    tn = 1024 if nm % 1024 == 0 else 512
    return pl.pallas_call(
        _mod_kernel,
        grid=(depth, nm // tn),
        in_specs=[
            pl.BlockSpec((8, d), lambda l, j: (0, 0)),
            pl.BlockSpec((1, d, tn), lambda l, j: (l, 0, j)),
            pl.BlockSpec((1, 1, tn), lambda l, j: (l, 0, j)),
        ],
        out_specs=pl.BlockSpec((1, 8, tn), lambda l, j: (l, 0, j)),
        out_shape=jax.ShapeDtypeStruct((depth, 8, nm), F32),
        compiler_params=_cparams("parallel", "parallel"),
        name="mod_vectors",
    )(crow, w_mod, b_mod.reshape(depth, 1, nm))


def _norm_mod(x, gain, shift, scale):
    y = x * lax.rsqrt(jnp.mean(x * x, axis=-1, keepdims=True) + EPS) * gain
    return y * (1.0 + scale) + shift


def _group_rms(x, ones_bd, gain, group):
    x2 = (x * x).astype(BF16)
    parts = [_mm(x2[:, c:c + MXU_DIM], ones_bd) for c in range(0, x.shape[1], MXU_DIM)]
    ms = jnp.concatenate(parts, axis=1) * (1.0 / group)
    return x * lax.rsqrt(ms + EPS) * gain


def _rope(x, cos, s_lo, s_hi):
    q = DIFF_DIM // 4
    outs = []
    for c in range(0, x.shape[1], LANES):
        xs = x[:, c:c + LANES]
        outs.append(xs * cos + pltpu.roll(xs, q, 1) * s_lo + pltpu.roll(xs, LANES - q, 1) * s_hi)
    return jnp.concatenate(outs, axis=1)


def _inproj_even_kernel(h_ref, sh_ref, sc_ref, gain_ref, w_ref, gq_ref, gk_ref, ones_ref, cos_ref, slo_ref, shi_ref,
                        qn_ref, qr_ref, kr_ref, v_ref, f_ref, a_scr, *, nq, nv, qscale):
    j = pl.program_id(1)

    @pl.when(j == 0)
    def _():
        a_scr[...] = _norm_mod(h_ref[...], gain_ref[...], sh_ref[0], sc_ref[0]).astype(BF16)

    acc = _mm(a_scr[...], w_ref[...])

    @pl.when(j < nq)
    def _():
        qn = _group_rms(acc, ones_ref[...], gq_ref[...], DIFF_DIM) * qscale
        qn_ref[...] = qn.astype(BF16)
        qr_ref[...] = _rope(qn, cos_ref[...], slo_ref[...], shi_ref[...]).astype(BF16)

    @pl.when(jnp.logical_and(j >= nq, j < 2 * nq))
    def _():
        kn = _group_rms(acc, ones_ref[...], gk_ref[...], DIFF_DIM)
        kr_ref[...] = _rope(kn, cos_ref[...], slo_ref[...], shi_ref[...]).astype(BF16)

    @pl.when(jnp.logical_and(j >= 2 * nq, j < 2 * nq + nv))
    def _():
        v_ref[...] = acc.astype(BF16)

    @pl.when(j >= 2 * nq + nv)
    def _():
        f_ref[...] = acc.astype(BF16)


def _inproj_odd_kernel(h_ref, sh_ref, sc_ref, gain_ref, w_ref, gq_ref, gk_ref, ones_ref,
                       cg_ref, q_ref, k_ref, v_ref, a_scr, *, ncg, nh, qscale):
    j = pl.program_id(1)

    @pl.when(j == 0)
    def _():
        a_scr[...] = _norm_mod(h_ref[...], gain_ref[...], sh_ref[0], sc_ref[0]).astype(BF16)

    acc = _mm(a_scr[...], w_ref[...])

    @pl.when(j < ncg)
    def _():
        cg_ref[...] = acc.astype(BF16)

    @pl.when(jnp.logical_and(j >= ncg, j < ncg + nh))
    def _():
        q_ref[...] = (_group_rms(acc, ones_ref[...], gq_ref[...], NA_DIM) * qscale).astype(BF16)

    @pl.when(jnp.logical_and(j >= ncg + nh, j < ncg + 2 * nh))
    def _():
        k_ref[...] = _group_rms(acc, ones_ref[...], gk_ref[...], NA_DIM).astype(BF16)

    @pl.when(j >= ncg + 2 * nh)
    def _():
        v_ref[...] = acc.astype(BF16)


def _row_specs(tm, d, tiles_per_batch, n_batch):
    def mod_spec(k):
        return pl.BlockSpec((1, 1, d), lambda i, j: (jnp.minimum(i // tiles_per_batch, n_batch) * N_MOD + k, 0, 0))

    return pl.BlockSpec((tm, d), lambda i, j: (i, 0)), mod_spec


def _clamped(tm, tn, lo, n):
    return pl.BlockSpec((tm, tn), lambda i, j: (i, jnp.clip(j - lo, 0, n - 1)))


def _ones_blockdiag(group):
    idx = np.arange(MXU_DIM) // group
    return jnp.asarray((idx[:, None] == idx[None, :]).astype(np.float32), dtype=BF16)


def _inproj_even(hh, mods, gain, w, gq, gk, rope_tabs, dims):
    r, d = hh.shape
    tm, tn = dims["tm"], dims["tn"]
    ev_q, ev_v, fw = dims["ev_q"], dims["ev_v"], dims["fw"]
    nq, nv, nf = ev_q // tn, ev_v // tn, fw // tn
    nj = 2 * nq + nv + nf
    hspec, mod_spec = _row_specs(tm, d, dims["tpb"], dims["b"])
    const = lambda shape: pl.BlockSpec(shape, lambda i, j: (0,) * len(shape))
    tab = pl.BlockSpec((tm, LANES), lambda i, j: (i, 0))
    kern = functools.partial(_inproj_even_kernel, nq=nq, nv=nv, qscale=DIFF_DIM ** -0.5 * LOG2E)
    return pl.pallas_call(
        kern,
        grid=(r // tm, nj),
        in_specs=[hspec, mod_spec(0), mod_spec(1), const((1, d)),
                  pl.BlockSpec((d, tn), lambda i, j: (0, j)),
                  const((1, tn)), const((1, tn)), const((MXU_DIM, MXU_DIM)), tab, tab, tab],
        out_specs=[_clamped(tm, tn, 0, nq), _clamped(tm, tn, 0, nq), _clamped(tm, tn, nq, nq),
                   _clamped(tm, tn, 2 * nq, nv), _clamped(tm, tn, 2 * nq + nv, nf)],
        out_shape=[jax.ShapeDtypeStruct((r, ev_q), BF16)] * 3
        + [jax.ShapeDtypeStruct((r, ev_v), BF16), jax.ShapeDtypeStruct((r, fw), BF16)],
        scratch_shapes=[pltpu.VMEM((tm, d), BF16)],
        compiler_params=_cparams("parallel", "arbitrary"),
        name="inproj_even",
    )(hh, mods, mods, gain, w, gq, gk, _ones_blockdiag(DIFF_DIM), *rope_tabs)


def _inproj_odd(hh, mods, gain, w, gq, gk, dims):
    r, d = hh.shape
    tm, tn = dims["tm"], dims["tn"]
    cw, nw = dims["cw"], dims["nw"]
    ncg, nh = 3 * cw // tn, nw // tn
    nj = ncg + 3 * nh
    hspec, mod_spec = _row_specs(tm, d, dims["tpb"], dims["b"])
    const = lambda shape: pl.BlockSpec(shape, lambda i, j: (0,) * len(shape))
    kern = functools.partial(_inproj_odd_kernel, ncg=ncg, nh=nh, qscale=NA_DIM ** -0.5 * LOG2E)
    return pl.pallas_call(
        kern,
        grid=(r // tm, nj),
        in_specs=[hspec, mod_spec(0), mod_spec(1), const((1, d)),
                  pl.BlockSpec((d, tn), lambda i, j: (0, j)),
                  const((1, tn)), const((1, tn)), const((MXU_DIM, MXU_DIM))],
        out_specs=[_clamped(tm, tn, 0, ncg), _clamped(tm, tn, ncg, nh), _clamped(tm, tn, ncg + nh, nh),
                   _clamped(tm, tn, ncg + 2 * nh, nh)],
        out_shape=[jax.ShapeDtypeStruct((r, 3 * cw), BF16)] + [jax.ShapeDtypeStruct((r, nw), BF16)] * 3,
        scratch_shapes=[pltpu.VMEM((tm, d), BF16)],
        compiler_params=_cparams("parallel", "arbitrary"),
        name="inproj_odd",
    )(hh, mods, mods, gain, w, gq, gk, _ones_blockdiag(NA_DIM))


def _split_components(q):
    lane = lax.broadcasted_iota(jnp.int32, q.shape, 1)
    zero = jnp.zeros_like(q)
    return jnp.concatenate([jnp.where(lane < DIFF_DIM, q, zero), jnp.where(lane >= DIFF_DIM, q, zero)], axis=0)


def _softmax_step(q2, kk, vv, m_scr, l_scr, acc_scr):
    s = _mm_nt(q2, kk)
    m_old = m_scr[...]
    m_new = jnp.maximum(m_old, jnp.max(s, axis=-1, keepdims=True))
    alpha = jnp.exp2(m_old - m_new)
    p = jnp.exp2(s - _tile_lanes(m_new, s.shape[1] // LANES))
    l_scr[...] = alpha * l_scr[...] + jnp.sum(p, axis=-1, keepdims=True)
    acc_scr[...] = alpha * acc_scr[...] + _mm(p.astype(BF16), vv)
    m_scr[...] = m_new


def _diff_attn_kernel(*refs, n_chunks, tk, lam_init):
    if n_chunks:
        lam_ref, g_ref, qn_ref, qr_ref, kc_ref, vc_ref, k_ref, v_ref, o_ref, m_scr, l_scr, acc_scr = refs
    else:
        lam_ref, g_ref, qn_ref, kc_ref, vc_ref, o_ref, m_scr, l_scr, acc_scr = refs
    tq = qn_ref.shape[0]
    m_scr[...] = jnp.full(m_scr.shape, NEG_BIG, F32)
    l_scr[...] = jnp.zeros(l_scr.shape, F32)
    acc_scr[...] = jnp.zeros(acc_scr.shape, F32)

    _softmax_step(_split_components(qn_ref[...]), kc_ref[...], vc_ref[...], m_scr, l_scr, acc_scr)

    if n_chunks:
        q2 = _split_components(qr_ref[...])

        def body(c, carry):
            off = pl.multiple_of(c * tk, tk)
            _softmax_step(q2, k_ref[pl.ds(off, tk), :], v_ref[pl.ds(off, tk), :], m_scr, l_scr, acc_scr)
            return carry

        lax.fori_loop(0, n_chunks, body, 0)

    lv = lam_ref[...]
    lam = (jnp.exp(jnp.sum(lv[0:1] * lv[1:2], axis=-1, keepdims=True))
           - jnp.exp(jnp.sum(lv[2:3] * lv[3:4], axis=-1, keepdims=True)) + lam_init)
    o2 = acc_scr[...] / l_scr[...]
    o = o2[:tq] - lam * o2[tq:]
    o = o * lax.rsqrt(jnp.mean(o * o, axis=-1, keepdims=True) + EPS) * g_ref[...] * (1.0 - lam_init)
    o_ref[...] = o.astype(BF16)


def _diff_attn_latent(lam_vec, subln, qn, qr, kr, v, dims, lam_init):
    b, s, ctx, r = dims["b"], dims["s"], dims["ctx"], dims["r"]
    tq, tk = dims["tq"], dims["tk"]
    nh = qn.shape[1] // DIFF_V
    ctx_blk0 = b * s // ctx
    kern = functools.partial(_diff_attn_kernel, n_chunks=s // tk, tk=tk, lam_init=lam_init)
    qspec = pl.BlockSpec((tq, DIFF_V), lambda bi, h, qi: (bi * (s // tq) + qi, h))
    cspec = pl.BlockSpec((ctx, DIFF_V), lambda bi, h, qi: (ctx_blk0 + bi, h))
    kspec = pl.BlockSpec((s, DIFF_V), lambda bi, h, qi: (bi, h))
    return pl.pallas_call(
        kern,
        grid=(b, nh, s // tq),
        in_specs=[pl.BlockSpec((4, DIFF_DIM), lambda bi, h, qi: (0, 0)),
                  pl.BlockSpec((1, DIFF_V), lambda bi, h, qi: (0, 0)),
                  qspec, qspec, cspec, cspec, kspec, kspec],
        out_specs=qspec,
        out_shape=jax.ShapeDtypeStruct((r, nh * DIFF_V), BF16),
        scratch_shapes=[pltpu.VMEM((2 * tq, LANES), F32)] * 3,
        compiler_params=_cparams("parallel", "parallel", "arbitrary"),
        name="diff_attn_latent",
    )(lam_vec, subln, qn, qr, kr, v, kr, v)


def _diff_attn_ctx(lam_vec, subln, qn, kr, v, o_lat, dims, lam_init):
    b, s, ctx = dims["b"], dims["s"], dims["ctx"]
    nh = qn.shape[1] // DIFF_V
    ctx_blk0 = b * s // ctx
    kern = functools.partial(_diff_attn_kernel, n_chunks=0, tk=0, lam_init=lam_init)
    cspec = pl.BlockSpec((ctx, DIFF_V), lambda bi, h: (ctx_blk0 + bi, h))

    def wrapped(lam_ref, g_ref, qn_ref, kc_ref, vc_ref, alias_ref, o_ref, m_scr, l_scr, acc_scr):
        del alias_ref
        kern(lam_ref, g_ref, qn_ref, kc_ref, vc_ref, o_ref, m_scr, l_scr, acc_scr)

    return pl.pallas_call(
        wrapped,
        grid=(b, nh),
        in_specs=[pl.BlockSpec((4, DIFF_DIM), lambda bi, h: (0, 0)),
                  pl.BlockSpec((1, DIFF_V), lambda bi, h: (0, 0)),
                  cspec, cspec, cspec, pl.BlockSpec(memory_space=pl.ANY)],
        out_specs=cspec,
        out_shape=jax.ShapeDtypeStruct(o_lat.shape, BF16),
        input_output_aliases={5: 0},
        scratch_shapes=[pltpu.VMEM((2 * ctx, LANES), F32)] * 3,
        compiler_params=_cparams("parallel", "parallel"),
        name="diff_attn_ctx",
    )(lam_vec, subln, qn, kr, v, o_lat)


def _dft_cs(n):
    k = np.arange(n)
    ang = 2.0 * np.pi * ((k[:, None] * k[None, :]) % n) / n
    return np.cos(ang), np.sin(ang)


def _channel_dft(groups, scale):
    c, s = _dft_cs(FOURIER_DIM)
    eye = np.eye(groups)
    return np.kron(eye, c) * scale, np.kron(eye, s) * scale


def _fourier_rows_kernel(x_ref, w_ref, tc_ref, ts_ref, y_ref, *, cg, fw, nr):
    for c in range(cg):
        y = _mm(w_ref[...], x_ref[:, c * fw:(c + 1) * fw])
        yr, yi = y[:nr], y[nr:]
        tc = _tile_lanes(tc_ref[c], fw // LANES)
        ts = _tile_lanes(ts_ref[c], fw // LANES)
        y_ref[0, 0, :, c * fw:(c + 1) * fw] = (yr * tc + yi * ts).astype(BF16)
        y_ref[0, 1, :, c * fw:(c + 1) * fw] = (yi * tc - yr * ts).astype(BF16)


def _fourier_cols_kernel(y_ref, w2_ref, wc_ref, ws_ref, o_ref, *, ag, fw, nc):
    for a in range(ag):
        yy = jnp.concatenate([y_ref[0, 0, a], y_ref[0, 1, a]], axis=0)
        z = _mm(w2_ref[...], yy)
        out = _mm(z[:nc].astype(BF16), wc_ref[...]) + _mm(z[nc:].astype(BF16), ws_ref[...])
        o_ref[:, a * fw:(a + 1) * fw] = out.astype(BF16)


def _fourier_dense_kernel(x_ref, wc_ref, ws_ref, cn_ref, sn_ref, alias_ref, o_ref):
    del alias_ref
    x = x_ref[...]
    gc = _mm(x, wc_ref[...]).astype(BF16)
    gs = _mm(x, ws_ref[...]).astype(BF16)
    o_ref[...] = (_mm(cn_ref[...], gc) - _mm(sn_ref[...], gs)).astype(BF16)


def _fourier_latent(f, dims):
    b, s, r = dims["b"], dims["s"], dims["r"]
    fw = f.shape[1]
    groups = fw // FOURIER_DIM
    nc = GRID_W
    nr = s // nc
    cg = 4
    ag = 8
    c1, s1 = _dft_cs(nr)
    w1 = jnp.asarray(np.concatenate([c1, -s1], axis=0), dtype=BF16)
    ang = 2.0 * np.pi * (np.arange(nc)[:, None] * np.arange(nr)[None, :]) / s
    tcos = jnp.asarray(np.repeat(np.cos(ang)[:, :, None], LANES, axis=2), dtype=F32)
    tsin = jnp.asarray(np.repeat(np.sin(ang)[:, :, None], LANES, axis=2), dtype=F32)
    x2d = f.reshape(r // nc, nc * fw)
    y = pl.pallas_call(
        functools.partial(_fourier_rows_kernel, cg=cg, fw=fw, nr=nr),
        grid=(b, nc // cg),
        in_specs=[pl.BlockSpec((nr, cg * fw), lambda bi, j: (bi, j)),
                  pl.BlockSpec((2 * nr, nr), lambda bi, j: (0, 0)),
                  pl.BlockSpec((cg, nr, LANES), lambda bi, j: (j, 0, 0)),
                  pl.BlockSpec((cg, nr, LANES), lambda bi, j: (j, 0, 0))],
        out_specs=pl.BlockSpec((1, 2, nr, cg * fw), lambda bi, j: (bi, 0, 0, j)),
        out_shape=jax.ShapeDtypeStruct((b, 2, nr, nc * fw), BF16),
        compiler_params=_cparams("parallel", "parallel"),
        name="fourier_rows",
    )(x2d, w1, tcos, tsin)
    c2, s2 = _dft_cs(nc)
    w2 = jnp.asarray(np.block([[c2, s2], [-s2, c2]]), dtype=BF16)
    wc, ws = _channel_dft(groups, 1.0 / math.sqrt(s * FOURIER_DIM))
    out = pl.pallas_call(
        functools.partial(_fourier_cols_kernel, ag=ag, fw=fw, nc=nc),
        grid=(b, nr // ag),
        in_specs=[pl.BlockSpec((1, 2, ag, nc, fw), lambda bi, j: (bi, 0, j, 0, 0)),
                  pl.BlockSpec((2 * nc, 2 * nc), lambda bi, j: (0, 0)),
                  pl.BlockSpec((fw, fw), lambda bi, j: (0, 0)),
                  pl.BlockSpec((fw, fw), lambda bi, j: (0, 0))],
        out_specs=pl.BlockSpec((nc, ag * fw), lambda bi, j: (bi, j)),
        out_shape=jax.ShapeDtypeStruct((r // nr, nr * fw), BF16),
        compiler_params=_cparams("parallel", "parallel"),
        name="fourier_cols",
    )(y.reshape(b, 2, nr, nc, fw), w2, jnp.asarray(wc, dtype=BF16), jnp.asarray(ws, dtype=BF16))
    return out.reshape(r, fw)


def _fourier_ctx(f, fo, dims):
    b, s, ctx = dims["b"], dims["s"], dims["ctx"]
    fw = f.shape[1]
    groups = fw // FOURIER_DIM
    wc, ws = _channel_dft(groups, 1.0 / math.sqrt(ctx * FOURIER_DIM))
    cn, sn = _dft_cs(ctx)
    blk0 = b * s // ctx
    rows = pl.BlockSpec((ctx, fw), lambda bi: (blk0 + bi, 0))
    const = lambda shape: pl.BlockSpec(shape, lambda bi: (0, 0))
    return pl.pallas_call(
        _fourier_dense_kernel,
        grid=(b,),
        in_specs=[rows, const((fw, fw)), const((fw, fw)), const((ctx, ctx)), const((ctx, ctx)),
                  pl.BlockSpec(memory_space=pl.ANY)],
        out_specs=rows,
        out_shape=jax.ShapeDtypeStruct(fo.shape, BF16),
        input_output_aliases={5: 0},
        compiler_params=_cparams("parallel"),
        name="fourier_ctx",
    )(f, jnp.asarray(wc, dtype=BF16), jnp.asarray(ws, dtype=BF16),
      jnp.asarray(cn, dtype=BF16), jnp.asarray(sn, dtype=BF16), fo)


def _conv_kernel(gb_ref, gc_ref, hh_ref, gcp_ref, hhp_ref, gcn_ref, hhn_ref, w_ref, o_ref, *, n_lat_tiles, s, ctx):
    i = pl.program_id(0)
    tm = gb_ref.shape[0]
    u = gc_ref[...].astype(F32) * hh_ref[...].astype(F32)
    halo = gcp_ref.shape[0]
    u_prev = gcp_ref[halo - 1:halo, :].astype(F32) * hhp_ref[halo - 1:halo, :].astype(F32)
    u_next = gcn_ref[0:1, :].astype(F32) * hhn_ref[0:1, :].astype(F32)
    row = lax.broadcasted_iota(jnp.int32, (tm, 1), 0)
    seq = jnp.where(i < n_lat_tiles, s, ctx)
    pos = lax.rem(i * tm + row, seq)
    dn = jnp.where(row == 0, u_prev, pltpu.roll(u, 1, 0))
    dn = jnp.where(pos == 0, 0.0, dn)
    up = jnp.where(row == tm - 1, u_next, pltpu.roll(u, tm - 1, 0))
    up = jnp.where(pos == seq - 1, 0.0, up)
    w = w_ref[...]
    y = dn * w[0:1] + u * w[1:2] + up * w[2:3]
    o_ref[...] = (gb_ref[...].astype(F32) * y).astype(BF16)


def _short_conv(cg, conv_w, dims):
    r = cg.shape[0]
    cw, tm = dims["cw"], dims["tm"]
    halo = 16
    tc = min(cw, 1024)
    ncol = cw // tc
    nhb = tm // halo
    last = r // halo - 1
    cur = lambda off: pl.BlockSpec((tm, tc), lambda i, j: (i, off * ncol + j))
    prev = lambda off: pl.BlockSpec((halo, tc), lambda i, j: (jnp.maximum(i * nhb - 1, 0), off * ncol + j))
    nxt = lambda off: pl.BlockSpec((halo, tc), lambda i, j: (jnp.minimum((i + 1) * nhb, last), off * ncol + j))
    kern = functools.partial(_conv_kernel, n_lat_tiles=dims["b"] * dims["tpb"], s=dims["s"], ctx=dims["ctx"])
    return pl.pallas_call(
        kern,
        grid=(r // tm, ncol),
        in_specs=[cur(0), cur(1), cur(2), prev(1), prev(2), nxt(1), nxt(2),
                  pl.BlockSpec((CONV_K, tc), lambda i, j: (0, j))],
        out_specs=pl.BlockSpec((tm, tc), lambda i, j: (i, j)),
        out_shape=jax.ShapeDtypeStruct((r, cw), BF16),
        compiler_params=_cparams("parallel", "parallel"),
        name="short_conv",
    )(cg, cg, cg, cg, cg, cg, cg, conv_w)


NA_BLOCK_ROWS = 8
NA_KEY_ROWS = 16


def _na_plan(rows):
    wr = min(NA_WIN_ROWS, rows)
    starts, variants, keys = [], [], {}
    for jb in range(rows // NA_BLOCK_ROWS):
        r0 = jb * NA_BLOCK_ROWS
        ks = int(np.clip(r0 - wr // 2, 0, rows - NA_KEY_ROWS))
        rs = [int(np.clip(r0 + t - wr // 2, 0, rows - wr)) for t in range(NA_BLOCK_ROWS)]
        key = (r0 - ks, tuple(x - ks for x in rs))
        variants.append(keys.setdefault(key, len(keys)))
        starts.append(ks)
    return starts, variants, list(keys), wr


def _na_bias(rpb, rows):
    _, _, keys, wr = _na_plan(rows)
    col = np.arange(GRID_W)
    cs = np.clip(col - NA_WIN_COLS // 2, 0, GRID_W - NA_WIN_COLS)
    kc = np.arange(GRID_W)
    dc = kc[None, :] - col[:, None] + (NA_WIN_COLS - 1)
    col_ok = (kc[None, :] >= cs[:, None]) & (kc[None, :] < cs[:, None] + NA_WIN_COLS)
    out = []
    for r0_rel, rs_rel in keys:
        t = np.arange(NA_BLOCK_ROWS)
        i = np.arange(NA_KEY_ROWS)
        dr = i[None, :] - (r0_rel + t[:, None]) + (NA_WIN_ROWS - 1)
        rs_arr = np.asarray(rs_rel)
        row_ok = (i[None, :] >= rs_arr[:, None]) & (i[None, :] < rs_arr[:, None] + wr)
        ok = row_ok[:, None, :, None] & col_ok[None, :, None, :]
        dri = np.clip(dr, 0, 2 * NA_WIN_ROWS - 2)[:, None, :, None]
        dci = np.clip(dc, 0, 2 * NA_WIN_COLS - 2)[None, :, None, :]
        vals = rpb.astype(F32)[:, dri, dci] * LOG2E
        vals = jnp.where(ok[None], vals, NEG_BIG)
        out.append(vals.reshape(rpb.shape[0], NA_BLOCK_ROWS * GRID_W, NA_KEY_ROWS * GRID_W))
    return jnp.stack(out)


def _na_kernel(q_ref, k_ref, v_ref, kc_ref, vc_ref, bias_ref, o_ref, *, starts, variants):
    qb = NA_BLOCK_ROWS * GRID_W
    kb = NA_KEY_ROWS * GRID_W

    def lookup(table, jb):
        out = jnp.int32(table[0])
        for idx in range(1, len(table)):
            out = jnp.where(jb == idx, jnp.int32(table[idx]), out)
        return out

    def body(jb, carry):
        q0 = pl.multiple_of(jb * qb, qb)
        k0 = pl.multiple_of(lookup(starts, jb) * GRID_W, GRID_W)
        q = q_ref[pl.ds(q0, qb), :]
        s_nb = _mm_nt(q, k_ref[pl.ds(k0, kb), :]) + bias_ref[lookup(variants, jb), 0]
        s_cx = _mm_nt(q, kc_ref[...])
        m = jnp.maximum(jnp.max(s_nb, axis=-1, keepdims=True), jnp.max(s_cx, axis=-1, keepdims=True))
        p_nb = jnp.exp2(s_nb - m)
        p_cx = jnp.exp2(s_cx - m)
        l = jnp.sum(p_nb, axis=-1, keepdims=True) + jnp.sum(p_cx, axis=-1, keepdims=True)
        o = _mm(p_nb.astype(BF16), v_ref[pl.ds(k0, kb), :]) + _mm(p_cx.astype(BF16), vc_ref[...])
        o_ref[pl.ds(q0, qb), :] = (o / l).astype(BF16)
        return carry

    lax.fori_loop(0, len(starts), body, 0)


def _na_ctx_kernel(q_ref, kc_ref, vc_ref, alias_ref, o_ref):
    del alias_ref
    s = _mm_nt(q_ref[...], kc_ref[...])
    p = jnp.exp2(s - jnp.max(s, axis=-1, keepdims=True))
    o = _mm(p.astype(BF16), vc_ref[...]) / jnp.sum(p, axis=-1, keepdims=True)
    o_ref[...] = o.astype(BF16)


def _na_latent(q, k, v, bias, dims):
    b, s, ctx, r = dims["b"], dims["s"], dims["ctx"], dims["r"]
    nh = q.shape[1] // NA_DIM
    starts, variants, keys, _ = _na_plan(s // GRID_W)
    blk0 = b * s // ctx
    lat = pl.BlockSpec((s, NA_DIM), lambda bi, h: (bi, h))
    cx = pl.BlockSpec((ctx, NA_DIM), lambda bi, h: (blk0 + bi, h))
    nvar = len(keys)
    return pl.pallas_call(
        functools.partial(_na_kernel, starts=tuple(starts), variants=tuple(variants)),
        grid=(b, nh),
        in_specs=[lat, lat, lat, cx, cx,
                  pl.BlockSpec((nvar, 1) + bias.shape[2:], lambda bi, h: (0, h, 0, 0))],
        out_specs=lat,
        out_shape=jax.ShapeDtypeStruct((r, nh * NA_DIM), BF16),
        compiler_params=_cparams("parallel", "arbitrary"),
        name="na_latent",
    )(q, k, v, k, v, bias)


def _na_ctx(q, k, v, o_lat, dims):
    b, s, ctx = dims["b"], dims["s"], dims["ctx"]
    nh = q.shape[1] // NA_DIM
    blk0 = b * s // ctx
    cx = pl.BlockSpec((ctx, NA_DIM), lambda bi, h: (blk0 + bi, h))
    return pl.pallas_call(
        _na_ctx_kernel,
        grid=(b, nh),
        in_specs=[cx, cx, cx, pl.BlockSpec(memory_space=pl.ANY)],
        out_specs=cx,
        out_shape=jax.ShapeDtypeStruct(o_lat.shape, BF16),
        input_output_aliases={3: 0},
        compiler_params=_cparams("parallel", "parallel"),
        name="na_ctx",
    )(q, k, v, o_lat)


def _outproj_kernel(xa_ref, xb_ref, wa_ref, wb_ref, h_ref, gate_ref, o_ref):
    y = _mm(xa_ref[...], wa_ref[...]) + _mm(xb_ref[...], wb_ref[...])
    o_ref[...] = h_ref[...] + gate_ref[0] * y


def _outproj(xa, xb, w, hh, mods, n_tiles, dims):
    r, d = hh.shape
    tm = dims["tm"]
    ka, kb = xa.shape[1], xb.shape[1]
    assert ka % kb == 0
    hspec, mod_spec = _row_specs(tm, d, dims["tpb"], dims["b"])
    one = lambda spec: pl.BlockSpec(spec.block_shape, lambda i: spec.index_map(i, 0))
    return pl.pallas_call(
        _outproj_kernel,
        grid=(n_tiles,),
        in_specs=[pl.BlockSpec((tm, ka), lambda i: (i, 0)), pl.BlockSpec((tm, kb), lambda i: (i, 0)),
                  pl.BlockSpec((ka, d), lambda i: (0, 0)), pl.BlockSpec((kb, d), lambda i: (ka // kb, 0)),
                  one(hspec), one(mod_spec(2))],
        out_specs=one(hspec),
        out_shape=jax.ShapeDtypeStruct((r, d), F32),
        input_output_aliases={4: 0},
        compiler_params=_cparams("parallel"),
        name="outproj",
    )(xa, xb, w, w, hh, mods)


def _ffn_kernel(h_ref, sh_ref, sc_ref, gate_ref, gain_ref, wg_ref, wv_ref, wo_ref, o_ref, u_scr, acc_scr):
    j = pl.program_id(1)

    @pl.when(j == 0)
    def _():
        u_scr[...] = _norm_mod(h_ref[...], gain_ref[...], sh_ref[0], sc_ref[0]).astype(BF16)
        acc_scr[...] = jnp.zeros(acc_scr.shape, F32)

    u = u_scr[...]
    g = _mm(u, wg_ref[...])
    val = _mm(u, wv_ref[...])
    act = (g * jax.nn.sigmoid(g) * val).astype(BF16)
    acc_scr[...] += _mm(act, wo_ref[...])

    @pl.when(j == pl.num_programs(1) - 1)
    def _():
        o_ref[...] = h_ref[...] + gate_ref[0] * acc_scr[...]


def _ffn(hh, mods, gain, w_in, w_out, n_tiles, dims):
    r, d = hh.shape
    tm = dims["tm"]
    hid = w_out.shape[0]
    th = 512 if hid % 512 == 0 else 256
    nj = hid // th
    hspec, mod_spec = _row_specs(tm, d, dims["tpb"], dims["b"])
    return pl.pallas_call(
        _ffn_kernel,
        grid=(n_tiles, nj),
        in_specs=[hspec, mod_spec(3), mod_spec(4), mod_spec(5), pl.BlockSpec((1, d), lambda i, j: (0, 0)),
                  pl.BlockSpec((d, th), lambda i, j: (0, j)), pl.BlockSpec((d, th), lambda i, j: (0, nj + j)),
                  pl.BlockSpec((th, d), lambda i, j: (j, 0))],
        out_specs=hspec,
        out_shape=jax.ShapeDtypeStruct((r, d), F32),
        input_output_aliases={0: 0},
        scratch_shapes=[pltpu.VMEM((tm, d), BF16), pltpu.VMEM((tm, d), F32)],
        compiler_params=_cparams("parallel", "arbitrary"),
        name="ffn",
    )(hh, mods, mods, mods, gain, w_in, w_in, w_out)


def _rope_tables(dims):
    s, b, ctx = dims["s"], dims["b"], dims["ctx"]
    t = np.arange(s)
    n_freq = DIFF_DIM // 4
    inv = (ROPE_BASE ** (-np.arange(n_freq, dtype=np.float32) / n_freq)).astype(np.float32)
    lane = np.arange(LANES) % DIFF_DIM
    chunk = lane // n_freq
    pos = np.where((chunk // 2)[None, :] == 0, (t // GRID_W)[:, None], (t % GRID_W)[:, None]).astype(np.float32)
    ang = pos * inv[lane % n_freq][None, :]
    cos, sin = np.cos(ang), np.sin(ang)
    s_lo = np.where((chunk % 2)[None, :] == 1, sin, 0.0)
    s_hi = np.where((chunk % 2)[None, :] == 0, -sin, 0.0)
    pad = b * ctx

    def full(tab, fill):
        return jnp.asarray(np.concatenate([np.tile(tab, (b, 1)), np.full((pad, LANES), fill)], axis=0), dtype=F32)

    return full(cos, 1.0), full(s_lo, 0.0), full(s_hi, 0.0)


def kernel(x, c, ctx, c_ctx, w_mod, b_mod, norm_gain, w_ffn_in, w_ffn_out, ev_w_in, ev_w_out, ev_qk_gain,
           ev_lambda, ev_subln_gain, od_w_in, od_w_out, od_qk_gain, od_conv_w, od_rpb):
    b, s, d = x.shape
    n_ctx = ctx.shape[1]
    depth = w_mod.shape[0]
    tm = b * n_ctx
    r = b * s + b * n_ctx
    assert s % tm == 0 and s % GRID_W == 0
    ev_v = ev_w_out.shape[1] - FOURIER_GROUPS * FOURIER_DIM
    dims = dict(b=b, s=s, ctx=n_ctx, r=r, tm=tm, tn=512, tpb=s // tm,
                ev_q=ev_v // DIFF_V * 2 * DIFF_DIM, ev_v=ev_v, fw=FOURIER_GROUPS * FOURIER_DIM,
                cw=od_conv_w.shape[2], nw=od_rpb.shape[1] * NA_DIM,
                tq=min(256, s), tk=min(512, s))
    n_all, n_lat = r // tm, b * s // tm

    hh = jnp.concatenate([x.reshape(b * s, d), ctx.reshape(b * n_ctx, d)], axis=0)
    crow = jnp.concatenate([c, c_ctx[None, :], jnp.zeros((8 - b - 1, d), F32)], axis=0)
    mod_all = _mod_vectors(crow, w_mod, b_mod)
    rope_tabs = _rope_tables(dims)
    tn = dims["tn"]

    for i in range(depth):
        last = i == depth - 1
        n_out = n_lat if last else n_all
        mods = mod_all[i].reshape(8 * N_MOD, 1, d)
        gain1 = norm_gain[i, 0].reshape(1, d)
        gain2 = norm_gain[i, 1].reshape(1, d)
        j = i // 2
        if i % 2 == 0:
            lam_init = 0.8 - 0.6 * math.exp(-0.3 * i)
            gq = jnp.tile(ev_qk_gain[j, 0], tn // DIFF_DIM).reshape(1, tn)
            gk = jnp.tile(ev_qk_gain[j, 1], tn // DIFF_DIM).reshape(1, tn)
            qn, qr, kr, v, f = _inproj_even(hh, mods, gain1, ev_w_in[j].astype(BF16), gq, gk, rope_tabs, dims)
            subln = ev_subln_gain[j].reshape(1, DIFF_V)
            o = _diff_attn_latent(ev_lambda[j], subln, qn, qr, kr, v, dims, lam_init)
            fo = _fourier_latent(f, dims)
            if not last:
                o = _diff_attn_ctx(ev_lambda[j], subln, qn, kr, v, o, dims, lam_init)
                fo = _fourier_ctx(f, fo, dims)
            hh = _outproj(o, fo, ev_w_out[j].astype(BF16), hh, mods, n_out, dims)
        else:
            gq = jnp.tile(od_qk_gain[j, 0], tn // NA_DIM).reshape(1, tn)
            gk = jnp.tile(od_qk_gain[j, 1], tn // NA_DIM).reshape(1, tn)
            cg, q, k, v = _inproj_odd(hh, mods, gain1, od_w_in[j].astype(BF16), gq, gk, dims)
            cb = _short_conv(cg, od_conv_w[j], dims)
            o = _na_latent(q, k, v, _na_bias(od_rpb[j], s // GRID_W), dims)
            if not last:
                o = _na_ctx(q, k, v, o, dims)
            hh = _outproj(cb, o, od_w_out[j].astype(BF16), hh, mods, n_out, dims)
        hh = _ffn(hh, mods, gain2, w_ffn_in[i].astype(BF16), w_ffn_out[i].astype(BF16), n_out, dims)

    return hh[:b * s].reshape(b, s, d)
```

```python
import functools
import math

import jax
import jax.numpy as jnp
import numpy as np
from jax import lax
from jax.experimental import pallas as pl
from jax.experimental.pallas import tpu as pltpu

GRID_W = 64
EPS = 1e-6
N_MOD = 6

DIFF_HEADS = 12
DIFF_DIM = 64
DIFF_V = 2 * DIFF_DIM
FOURIER_GROUPS = 4
FOURIER_DIM = 128
ROPE_BASE = 10000.0

CONV_WIDTH = 1024
CONV_K = 3
NA_HEADS = 8
NA_DIM = 128
NA_WIN_ROWS = 8
NA_WIN_COLS = 16

F32 = jnp.float32
BF16 = jnp.bfloat16
LOG2E = 1.4426950408889634
NEG_BIG = -1e30
LANES = 128
MXU_DIM = 256
VMEM_LIMIT = 56 * 1024 * 1024
ATTN_CHUNKS_PER_TRIP = 8


def _cparams(*sem):
    return pltpu.CompilerParams(dimension_semantics=sem, vmem_limit_bytes=VMEM_LIMIT)


def _mm(a, b):
    return jnp.dot(a, b, preferred_element_type=F32)


def _mm_nt(a, b):
    return lax.dot_general(a, b, (((1,), (1,)), ((), ())), preferred_element_type=F32)


def _tile_lanes(x, n):
    return x if n == 1 else jnp.concatenate([x] * n, axis=1)


def _mod_kernel(c_ref, w_ref, b_ref, o_ref):
    c = c_ref[...]
    s = (c * jax.nn.sigmoid(c)).astype(BF16)
    o_ref[0] = _mm(s, w_ref[0].astype(BF16)) + b_ref[0]


def _mod_vectors(crow, w_mod, b_mod):
    depth, d, nm = w_mod.shape
    tn = 1024 if nm % 1024 == 0 else 512
    return pl.pallas_call(
        _mod_kernel,
        grid=(depth, nm // tn),
        in_specs=[
            pl.BlockSpec((8, d), lambda l, j: (0, 0)),
            pl.BlockSpec((1, d, tn), lambda l, j: (l, 0, j)),
            pl.BlockSpec((1, 1, tn), lambda l, j: (l, 0, j)),
        ],
        out_specs=pl.BlockSpec((1, 8, tn), lambda l, j: (l, 0, j)),
        out_shape=jax.ShapeDtypeStruct((depth, 8, nm), F32),
        compiler_params=_cparams("parallel", "parallel"),
        name="mod_vectors",
    )(crow, w_mod, b_mod.reshape(depth, 1, nm))


def _norm_mod(x, gain, shift, scale):
    y = x * lax.rsqrt(jnp.mean(x * x, axis=-1, keepdims=True) + EPS) * gain
    return y * (1.0 + scale) + shift


def _group_rms(x, ones_bd, gain, group):
    x2 = (x * x).astype(BF16)
    parts = [_mm(x2[:, c:c + MXU_DIM], ones_bd) for c in range(0, x.shape[1], MXU_DIM)]
    ms = jnp.concatenate(parts, axis=1) * (1.0 / group)
    return x * lax.rsqrt(ms + EPS) * gain


def _rope(x, cos, s_lo, s_hi):
    q = DIFF_DIM // 4
    outs = []
    for c in range(0, x.shape[1], LANES):
        xs = x[:, c:c + LANES]
        outs.append(xs * cos + pltpu.roll(xs, q, 1) * s_lo + pltpu.roll(xs, LANES - q, 1) * s_hi)
    return jnp.concatenate(outs, axis=1)


def _inproj_even_kernel(h_ref, sh_ref, sc_ref, gain_ref, w_ref, gq_ref, gk_ref, ones_ref, cos_ref, slo_ref, shi_ref,
                        qn_ref, qr_ref, kr_ref, v_ref, f_ref, a_scr, *, nq, nv, qscale):
    j = pl.program_id(1)

    @pl.when(j == 0)
    def _():
        a_scr[...] = _norm_mod(h_ref[...], gain_ref[...], sh_ref[0], sc_ref[0]).astype(BF16)

    acc = _mm(a_scr[...], w_ref[...])

    @pl.when(j < nq)
    def _():
        qn = _group_rms(acc, ones_ref[...], gq_ref[...], DIFF_DIM) * qscale
        qn_ref[...] = qn.astype(BF16)
        qr_ref[...] = _rope(qn, cos_ref[...], slo_ref[...], shi_ref[...]).astype(BF16)

    @pl.when(jnp.logical_and(j >= nq, j < 2 * nq))
    def _():
        kn = _group_rms(acc, ones_ref[...], gk_ref[...], DIFF_DIM)
        kr_ref[...] = _rope(kn, cos_ref[...], slo_ref[...], shi_ref[...]).astype(BF16)

    @pl.when(jnp.logical_and(j >= 2 * nq, j < 2 * nq + nv))
    def _():
        v_ref[...] = acc.astype(BF16)

    @pl.when(j >= 2 * nq + nv)
    def _():
        f_ref[...] = acc.astype(BF16)


def _inproj_odd_kernel(h_ref, sh_ref, sc_ref, gain_ref, w_ref, gq_ref, gk_ref, ones_ref,
                       cg_ref, q_ref, k_ref, v_ref, a_scr, *, ncg, nh, qscale):
    j = pl.program_id(1)

    @pl.when(j == 0)
    def _():
        a_scr[...] = _norm_mod(h_ref[...], gain_ref[...], sh_ref[0], sc_ref[0]).astype(BF16)

    acc = _mm(a_scr[...], w_ref[...])

    @pl.when(j < ncg)
    def _():
        cg_ref[...] = acc.astype(BF16)

    @pl.when(jnp.logical_and(j >= ncg, j < ncg + nh))
    def _():
        q_ref[...] = (_group_rms(acc, ones_ref[...], gq_ref[...], NA_DIM) * qscale).astype(BF16)

    @pl.when(jnp.logical_and(j >= ncg + nh, j < ncg + 2 * nh))
    def _():
        k_ref[...] = _group_rms(acc, ones_ref[...], gk_ref[...], NA_DIM).astype(BF16)

    @pl.when(j >= ncg + 2 * nh)
    def _():
        v_ref[...] = acc.astype(BF16)


def _row_specs(tm, d, tiles_per_batch, n_batch):
    def mod_spec(k):
        return pl.BlockSpec((1, 1, d), lambda i, j: (jnp.minimum(i // tiles_per_batch, n_batch) * N_MOD + k, 0, 0))

    return pl.BlockSpec((tm, d), lambda i, j: (i, 0)), mod_spec


def _clamped(tm, tn, lo, n):
    return pl.BlockSpec((tm, tn), lambda i, j: (i, jnp.clip(j - lo, 0, n - 1)))


def _ones_blockdiag(group):
    idx = np.arange(MXU_DIM) // group
    return jnp.asarray((idx[:, None] == idx[None, :]).astype(np.float32), dtype=BF16)


def _inproj_even(hh, mods, gain, w, gq, gk, rope_tabs, dims):
    r, d = hh.shape
    tm, tn = dims["tm_in"], dims["tn"]
    ev_q, ev_v, fw = dims["ev_q"], dims["ev_v"], dims["fw"]
    nq, nv, nf = ev_q // tn, ev_v // tn, fw // tn
    nj = 2 * nq + nv + nf
    hspec, mod_spec = _row_specs(tm, d, dims["s"] // tm, dims["b"])
    const = lambda shape: pl.BlockSpec(shape, lambda i, j: (0,) * len(shape))
    tab = pl.BlockSpec((tm, LANES), lambda i, j: (i, 0))
    kern = functools.partial(_inproj_even_kernel, nq=nq, nv=nv, qscale=DIFF_DIM ** -0.5 * LOG2E)
    return pl.pallas_call(
        kern,
        grid=(pl.cdiv(r, tm), nj),
        in_specs=[hspec, mod_spec(0), mod_spec(1), const((1, d)),
                  pl.BlockSpec((d, tn), lambda i, j: (0, j)),
                  const((1, tn)), const((1, tn)), const((MXU_DIM, MXU_DIM)), tab, tab, tab],
        out_specs=[_clamped(tm, tn, 0, nq), _clamped(tm, tn, 0, nq), _clamped(tm, tn, nq, nq),
                   _clamped(tm, tn, 2 * nq, nv), _clamped(tm, tn, 2 * nq + nv, nf)],
        out_shape=[jax.ShapeDtypeStruct((r, ev_q), BF16)] * 3
        + [jax.ShapeDtypeStruct((r, ev_v), BF16), jax.ShapeDtypeStruct((r, fw), BF16)],
        scratch_shapes=[pltpu.VMEM((tm, d), BF16)],
        compiler_params=_cparams("parallel", "arbitrary"),
        name="inproj_even",
    )(hh, mods, mods, gain, w, gq, gk, _ones_blockdiag(DIFF_DIM), *rope_tabs)


def _inproj_odd(hh, mods, gain, w, gq, gk, dims):
    r, d = hh.shape
    tm, tn = dims["tm_in"], dims["tn"]
    cw, nw = dims["cw"], dims["nw"]
    ncg, nh = 3 * cw // tn, nw // tn
    nj = ncg + 3 * nh
    hspec, mod_spec = _row_specs(tm, d, dims["s"] // tm, dims["b"])
    const = lambda shape: pl.BlockSpec(shape, lambda i, j: (0,) * len(shape))
    kern = functools.partial(_inproj_odd_kernel, ncg=ncg, nh=nh, qscale=NA_DIM ** -0.5 * LOG2E)
    return pl.pallas_call(
        kern,
        grid=(pl.cdiv(r, tm), nj),
        in_specs=[hspec, mod_spec(0), mod_spec(1), const((1, d)),
                  pl.BlockSpec((d, tn), lambda i, j: (0, j)),
                  const((1, tn)), const((1, tn)), const((MXU_DIM, MXU_DIM))],
        out_specs=[_clamped(tm, tn, 0, ncg), _clamped(tm, tn, ncg, nh), _clamped(tm, tn, ncg + nh, nh),
                   _clamped(tm, tn, ncg + 2 * nh, nh)],
        out_shape=[jax.ShapeDtypeStruct((r, 3 * cw), BF16)] + [jax.ShapeDtypeStruct((r, nw), BF16)] * 3,
        scratch_shapes=[pltpu.VMEM((tm, d), BF16)],
        compiler_params=_cparams("parallel", "arbitrary"),
        name="inproj_odd",
    )(hh, mods, mods, gain, w, gq, gk, _ones_blockdiag(NA_DIM))


def _split_components(q):
    lane = lax.broadcasted_iota(jnp.int32, q.shape, 1)
    zero = jnp.zeros_like(q)
    return jnp.concatenate([jnp.where(lane < DIFF_DIM, q, zero), jnp.where(lane >= DIFF_DIM, q, zero)], axis=0)


def _softmax_steps(q2, kvs, m_scr, acc_scr):
    scores = [_mm_nt(q2, kk) for kk, _ in kvs]
    m, acc = m_scr[...], acc_scr[...]
    for s, (_, vv) in zip(scores, kvs):
        m_new = jnp.maximum(m, jnp.max(s, axis=-1, keepdims=True))
        alpha = jnp.exp2(m - m_new)
        p = jnp.exp2((s - _tile_lanes(m_new, s.shape[1] // LANES)).astype(BF16))
        v_ext = jnp.concatenate([vv, jnp.ones_like(vv)], axis=1)
        acc = _tile_lanes(alpha, 2) * acc + _mm(p, v_ext)
        m = m_new
    m_scr[...], acc_scr[...] = m, acc


def _diff_attn_kernel(*refs, n_chunks, tk, lam_init):
    if n_chunks:
        lam_ref, g_ref, qn_ref, qr_ref, kc_ref, vc_ref, k_ref, v_ref, o_ref, m_scr, acc_scr = refs
    else:
        lam_ref, g_ref, qn_ref, kc_ref, vc_ref, o_ref, m_scr, acc_scr = refs
    tq = qn_ref.shape[0]
    m_scr[...] = jnp.full(m_scr.shape, NEG_BIG, F32)
    acc_scr[...] = jnp.zeros(acc_scr.shape, F32)

    _softmax_steps(_split_components(qn_ref[...]), [(kc_ref[...], vc_ref[...])], m_scr, acc_scr)

    if n_chunks:
        q2 = _split_components(qr_ref[...])
        per_trip = ATTN_CHUNKS_PER_TRIP if n_chunks % ATTN_CHUNKS_PER_TRIP == 0 else 1

        def body(c, carry):
            kvs = []
            for u in range(per_trip):
                off = pl.multiple_of((c * per_trip + u) * tk, tk)
                kvs.append((k_ref[pl.ds(off, tk), :], v_ref[pl.ds(off, tk), :]))
            _softmax_steps(q2, kvs, m_scr, acc_scr)
            return carry

        lax.fori_loop(0, n_chunks // per_trip, body, 0)

    lv = lam_ref[...]
    lam = (jnp.exp(jnp.sum(lv[0:1] * lv[1:2], axis=-1, keepdims=True))
           - jnp.exp(jnp.sum(lv[2:3] * lv[3:4], axis=-1, keepdims=True)) + lam_init)
    acc = acc_scr[...]
    o2 = acc[:, :DIFF_V] / acc[:, DIFF_V:]
    o = o2[:tq] - lam * o2[tq:]
    o = o * lax.rsqrt(jnp.mean(o * o, axis=-1, keepdims=True) + EPS) * g_ref[...] * (1.0 - lam_init)
    o_ref[...] = o.astype(BF16)


def _diff_attn_latent(lam_vec, subln, qn, qr, kr, v, dims, lam_init):
    b, s, ctx, r = dims["b"], dims["s"], dims["ctx"], dims["r"]
    tq, tk = dims["tq"], dims["tk"]
    nh = qn.shape[1] // DIFF_V
    ctx_blk0 = b * s // ctx
    kern = functools.partial(_diff_attn_kernel, n_chunks=s // tk, tk=tk, lam_init=lam_init)
    qspec = pl.BlockSpec((tq, DIFF_V), lambda bi, h, qi: (bi * (s // tq) + qi, h))
    cspec = pl.BlockSpec((ctx, DIFF_V), lambda bi, h, qi: (ctx_blk0 + bi, h))
    kspec = pl.BlockSpec((s, DIFF_V), lambda bi, h, qi: (bi, h))
    return pl.pallas_call(
        kern,
        grid=(b, nh, s // tq),
        in_specs=[pl.BlockSpec((4, DIFF_DIM), lambda bi, h, qi: (0, 0)),
                  pl.BlockSpec((1, DIFF_V), lambda bi, h, qi: (0, 0)),
                  qspec, qspec, cspec, cspec, kspec, kspec],
        out_specs=qspec,
        out_shape=jax.ShapeDtypeStruct((r, nh * DIFF_V), BF16),
        scratch_shapes=[pltpu.VMEM((2 * tq, LANES), F32), pltpu.VMEM((2 * tq, 2 * DIFF_V), F32)],
        compiler_params=_cparams("parallel", "parallel", "arbitrary"),
        name="diff_attn_latent",
    )(lam_vec, subln, qn, qr, kr, v, kr, v)


def _diff_attn_ctx(lam_vec, subln, qn, kr, v, o_lat, dims, lam_init):
    b, s, ctx = dims["b"], dims["s"], dims["ctx"]
    nh = qn.shape[1] // DIFF_V
    ctx_blk0 = b * s // ctx
    kern = functools.partial(_diff_attn_kernel, n_chunks=0, tk=0, lam_init=lam_init)
    cspec = pl.BlockSpec((ctx, DIFF_V), lambda bi, h: (ctx_blk0 + bi, h))

    def wrapped(lam_ref, g_ref, qn_ref, kc_ref, vc_ref, alias_ref, o_ref, m_scr, acc_scr):
        del alias_ref
        kern(lam_ref, g_ref, qn_ref, kc_ref, vc_ref, o_ref, m_scr, acc_scr)

    return pl.pallas_call(
        wrapped,
        grid=(b, nh),
        in_specs=[pl.BlockSpec((4, DIFF_DIM), lambda bi, h: (0, 0)),
                  pl.BlockSpec((1, DIFF_V), lambda bi, h: (0, 0)),
                  cspec, cspec, cspec, pl.BlockSpec(memory_space=pl.ANY)],
        out_specs=cspec,
        out_shape=jax.ShapeDtypeStruct(o_lat.shape, BF16),
        input_output_aliases={5: 0},
        scratch_shapes=[pltpu.VMEM((2 * ctx, LANES), F32), pltpu.VMEM((2 * ctx, 2 * DIFF_V), F32)],
        compiler_params=_cparams("parallel", "parallel"),
        name="diff_attn_ctx",
    )(lam_vec, subln, qn, kr, v, o_lat)


def _dft_cs(n):
    k = np.arange(n)
    ang = 2.0 * np.pi * ((k[:, None] * k[None, :]) % n) / n
    return np.cos(ang), np.sin(ang)


def _channel_dft(groups, scale):
    c, s = _dft_cs(FOURIER_DIM)
    eye = np.eye(groups)
    return np.kron(eye, c) * scale, np.kron(eye, s) * scale


def _fourier_rows_kernel(x_ref, w_ref, tc_ref, ts_ref, y_ref, *, cg, fw, nr):
    for c in range(cg):
        y = _mm(w_ref[...], x_ref[:, c * fw:(c + 1) * fw])
        yr, yi = y[:nr], y[nr:]
        tc = _tile_lanes(tc_ref[c], fw // LANES)
        ts = _tile_lanes(ts_ref[c], fw // LANES)
        y_ref[0, 0, :, c * fw:(c + 1) * fw] = (yr * tc + yi * ts).astype(BF16)
        y_ref[0, 1, :, c * fw:(c + 1) * fw] = (yi * tc - yr * ts).astype(BF16)


def _fourier_cols_kernel(y_ref, w2_ref, wc_ref, ws_ref, o_ref, *, ag, fw, nc):
    for a in range(ag):
        yy = jnp.concatenate([y_ref[0, 0, a], y_ref[0, 1, a]], axis=0)
        z = _mm(w2_ref[...], yy)
        out = _mm(z[:nc].astype(BF16), wc_ref[...]) + _mm(z[nc:].astype(BF16), ws_ref[...])
        o_ref[:, a * fw:(a + 1) * fw] = out.astype(BF16)


def _fourier_dense_kernel(x_ref, wc_ref, ws_ref, cn_ref, sn_ref, alias_ref, o_ref):
    del alias_ref
    x = x_ref[...]
    gc = _mm(x, wc_ref[...]).astype(BF16)
    gs = _mm(x, ws_ref[...]).astype(BF16)
    o_ref[...] = (_mm(cn_ref[...], gc) - _mm(sn_ref[...], gs)).astype(BF16)


def _fourier_latent(f, dims):
    b, s, r = dims["b"], dims["s"], dims["r"]
    fw = f.shape[1]
    groups = fw // FOURIER_DIM
    nc = GRID_W
    nr = s // nc
    cg = 4
    ag = 8
    c1, s1 = _dft_cs(nr)
    w1 = jnp.asarray(np.concatenate([c1, -s1], axis=0), dtype=BF16)
    ang = 2.0 * np.pi * (np.arange(nc)[:, None] * np.arange(nr)[None, :]) / s
    tcos = jnp.asarray(np.repeat(np.cos(ang)[:, :, None], LANES, axis=2), dtype=F32)
    tsin = jnp.asarray(np.repeat(np.sin(ang)[:, :, None], LANES, axis=2), dtype=F32)
    x2d = f.reshape(r // nc, nc * fw)
    y = pl.pallas_call(
        functools.partial(_fourier_rows_kernel, cg=cg, fw=fw, nr=nr),
        grid=(b, nc // cg),
        in_specs=[pl.BlockSpec((nr, cg * fw), lambda bi, j: (bi, j)),
                  pl.BlockSpec((2 * nr, nr), lambda bi, j: (0, 0)),
                  pl.BlockSpec((cg, nr, LANES), lambda bi, j: (j, 0, 0)),
                  pl.BlockSpec((cg, nr, LANES), lambda bi, j: (j, 0, 0))],
        out_specs=pl.BlockSpec((1, 2, nr, cg * fw), lambda bi, j: (bi, 0, 0, j)),
        out_shape=jax.ShapeDtypeStruct((b, 2, nr, nc * fw), BF16),
        compiler_params=_cparams("parallel", "parallel"),
        name="fourier_rows",
    )(x2d, w1, tcos, tsin)
    c2, s2 = _dft_cs(nc)
    w2 = jnp.asarray(np.block([[c2, s2], [-s2, c2]]), dtype=BF16)
    wc, ws = _channel_dft(groups, 1.0 / math.sqrt(s * FOURIER_DIM))
    out = pl.pallas_call(
        functools.partial(_fourier_cols_kernel, ag=ag, fw=fw, nc=nc),
        grid=(b, nr // ag),
        in_specs=[pl.BlockSpec((1, 2, ag, nc, fw), lambda bi, j: (bi, 0, j, 0, 0)),
                  pl.BlockSpec((2 * nc, 2 * nc), lambda bi, j: (0, 0)),
                  pl.BlockSpec((fw, fw), lambda bi, j: (0, 0)),
                  pl.BlockSpec((fw, fw), lambda bi, j: (0, 0))],
        out_specs=pl.BlockSpec((nc, ag * fw), lambda bi, j: (bi, j)),
        out_shape=jax.ShapeDtypeStruct((r // nr, nr * fw), BF16),
        compiler_params=_cparams("parallel", "parallel"),
        name="fourier_cols",
    )(y.reshape(b, 2, nr, nc, fw), w2, jnp.asarray(wc, dtype=BF16), jnp.asarray(ws, dtype=BF16))
    return out.reshape(r, fw)


def _fourier_ctx(f, fo, dims):
    b, s, ctx = dims["b"], dims["s"], dims["ctx"]
    fw = f.shape[1]
    groups = fw // FOURIER_DIM
    wc, ws = _channel_dft(groups, 1.0 / math.sqrt(ctx * FOURIER_DIM))
    cn, sn = _dft_cs(ctx)
    blk0 = b * s // ctx
    rows = pl.BlockSpec((ctx, fw), lambda bi: (blk0 + bi, 0))
    const = lambda shape: pl.BlockSpec(shape, lambda bi: (0, 0))
    return pl.pallas_call(
        _fourier_dense_kernel,
        grid=(b,),
        in_specs=[rows, const((fw, fw)), const((fw, fw)), const((ctx, ctx)), const((ctx, ctx)),
                  pl.BlockSpec(memory_space=pl.ANY)],
        out_specs=rows,
        out_shape=jax.ShapeDtypeStruct(fo.shape, BF16),
        input_output_aliases={5: 0},
        compiler_params=_cparams("parallel"),
        name="fourier_ctx",
    )(f, jnp.asarray(wc, dtype=BF16), jnp.asarray(ws, dtype=BF16),
      jnp.asarray(cn, dtype=BF16), jnp.asarray(sn, dtype=BF16), fo)


def _conv_kernel(gb_ref, gc_ref, hh_ref, gcp_ref, hhp_ref, gcn_ref, hhn_ref, w_ref, o_ref, *, n_lat_tiles, s, ctx):
    i = pl.program_id(0)
    tm = gb_ref.shape[0]
    u = gc_ref[...].astype(F32) * hh_ref[...].astype(F32)
    halo = gcp_ref.shape[0]
    u_prev = gcp_ref[halo - 1:halo, :].astype(F32) * hhp_ref[halo - 1:halo, :].astype(F32)
    u_next = gcn_ref[0:1, :].astype(F32) * hhn_ref[0:1, :].astype(F32)
    row = lax.broadcasted_iota(jnp.int32, (tm, 1), 0)
    seq = jnp.where(i < n_lat_tiles, s, ctx)
    pos = lax.rem(i * tm + row, seq)
    dn = jnp.where(row == 0, u_prev, pltpu.roll(u, 1, 0))
    dn = jnp.where(pos == 0, 0.0, dn)
    up = jnp.where(row == tm - 1, u_next, pltpu.roll(u, tm - 1, 0))
    up = jnp.where(pos == seq - 1, 0.0, up)
    w = w_ref[...]
    y = dn * w[0:1] + u * w[1:2] + up * w[2:3]
    o_ref[...] = (gb_ref[...].astype(F32) * y).astype(BF16)


def _short_conv(cg, conv_w, dims):
    r = cg.shape[0]
    cw, tm = dims["cw"], dims["tm"]
    halo = 16
    tc = min(cw, 1024)
    ncol = cw // tc
    nhb = tm // halo
    last = r // halo - 1
    cur = lambda off: pl.BlockSpec((tm, tc), lambda i, j: (i, off * ncol + j))
    prev = lambda off: pl.BlockSpec((halo, tc), lambda i, j: (jnp.maximum(i * nhb - 1, 0), off * ncol + j))
    nxt = lambda off: pl.BlockSpec((halo, tc), lambda i, j: (jnp.minimum((i + 1) * nhb, last), off * ncol + j))
    kern = functools.partial(_conv_kernel, n_lat_tiles=dims["b"] * dims["tpb"], s=dims["s"], ctx=dims["ctx"])
    return pl.pallas_call(
        kern,
        grid=(r // tm, ncol),
        in_specs=[cur(0), cur(1), cur(2), prev(1), prev(2), nxt(1), nxt(2),
                  pl.BlockSpec((CONV_K, tc), lambda i, j: (0, j))],
        out_specs=pl.BlockSpec((tm, tc), lambda i, j: (i, j)),
        out_shape=jax.ShapeDtypeStruct((r, cw), BF16),
        compiler_params=_cparams("parallel", "parallel"),
        name="short_conv",
    )(cg, cg, cg, cg, cg, cg, cg, conv_w)


NA_BLOCK_ROWS = 8
NA_KEY_ROWS = 16


def _na_plan(rows):
    wr = min(NA_WIN_ROWS, rows)
    starts, variants, keys = [], [], {}
    for jb in range(rows // NA_BLOCK_ROWS):
        r0 = jb * NA_BLOCK_ROWS
        ks = int(np.clip(r0 - wr // 2, 0, rows - NA_KEY_ROWS))
        rs = [int(np.clip(r0 + t - wr // 2, 0, rows - wr)) for t in range(NA_BLOCK_ROWS)]
        key = (r0 - ks, tuple(x - ks for x in rs))
        variants.append(keys.setdefault(key, len(keys)))
        starts.append(ks)
    return starts, variants, list(keys), wr


def _na_bias(rpb, rows):
    _, _, keys, wr = _na_plan(rows)
    col = np.arange(GRID_W)
    cs = np.clip(col - NA_WIN_COLS // 2, 0, GRID_W - NA_WIN_COLS)
    kc = np.arange(GRID_W)
    dc = kc[None, :] - col[:, None] + (NA_WIN_COLS - 1)
    col_ok = (kc[None, :] >= cs[:, None]) & (kc[None, :] < cs[:, None] + NA_WIN_COLS)
    n_dr, n_dc = 2 * NA_WIN_ROWS - 1, 2 * NA_WIN_COLS - 1
    col_sel = ((dc[None] == np.arange(n_dc)[:, None, None]) & col_ok[None]).astype(np.float32)
    t = np.arange(NA_BLOCK_ROWS)
    i = np.arange(NA_KEY_ROWS)
    row_sel, row_okv = [], []
    for r0_rel, rs_rel in keys:
        dr = i[None, :] - (r0_rel + t[:, None]) + (NA_WIN_ROWS - 1)
        rs_arr = np.asarray(rs_rel)
        row_ok = (i[None, :] >= rs_arr[:, None]) & (i[None, :] < rs_arr[:, None] + wr)
        row_sel.append(((dr[None] == np.arange(n_dr)[:, None, None]) & row_ok[None]).astype(np.float32))
        row_okv.append(row_ok)
    row_sel = np.stack(row_sel)
    ok = (jnp.asarray(np.stack(row_okv))[:, None, :, None, :, None]
          & jnp.asarray(col_ok)[None, None, None, :, None, :])
    hi = lax.Precision.HIGHEST
    col_exp = jnp.einsum("hrd,dck->hrck", rpb.astype(F32) * LOG2E, jnp.asarray(col_sel), precision=hi)
    vals = jnp.einsum("vrti,hrck->vhtcik", jnp.asarray(row_sel), col_exp, precision=hi)
    vals = jnp.where(ok, vals, NEG_BIG)
    return vals.reshape(len(keys), rpb.shape[0], NA_BLOCK_ROWS * GRID_W, NA_KEY_ROWS * GRID_W)


def _na_kernel(q_ref, k_ref, v_ref, kc_ref, vc_ref, bias_ref, o_ref, *, starts, variants):
    qb = NA_BLOCK_ROWS * GRID_W
    kb = NA_KEY_ROWS * GRID_W

    def lookup(table, jb):
        out = jnp.int32(table[0])
        for idx in range(1, len(table)):
            out = jnp.where(jb == idx, jnp.int32(table[idx]), out)
        return out

    def body(jb, carry):
        q0 = pl.multiple_of(jb * qb, qb)
        k0 = pl.multiple_of(lookup(starts, jb) * GRID_W, GRID_W)
        q = q_ref[pl.ds(q0, qb), :]
        s_nb = _mm_nt(q, k_ref[pl.ds(k0, kb), :]) + bias_ref[lookup(variants, jb), 0]
        s_cx = _mm_nt(q, kc_ref[...])
        m = jnp.maximum(jnp.max(s_nb, axis=-1, keepdims=True), jnp.max(s_cx, axis=-1, keepdims=True))
        p_nb = jnp.exp2(s_nb - m)
        p_cx = jnp.exp2(s_cx - m)
        l = jnp.sum(p_nb, axis=-1, keepdims=True) + jnp.sum(p_cx, axis=-1, keepdims=True)
        o = _mm(p_nb.astype(BF16), v_ref[pl.ds(k0, kb), :]) + _mm(p_cx.astype(BF16), vc_ref[...])
        o_ref[pl.ds(q0, qb), :] = (o / l).astype(BF16)
        return carry

    lax.fori_loop(0, len(starts), body, 0)


def _na_ctx_kernel(q_ref, kc_ref, vc_ref, alias_ref, o_ref):
    del alias_ref
    s = _mm_nt(q_ref[...], kc_ref[...])
    p = jnp.exp2(s - jnp.max(s, axis=-1, keepdims=True))
    o = _mm(p.astype(BF16), vc_ref[...]) / jnp.sum(p, axis=-1, keepdims=True)
    o_ref[...] = o.astype(BF16)


def _na_latent(q, k, v, bias, dims):
    b, s, ctx, r = dims["b"], dims["s"], dims["ctx"], dims["r"]
    nh = q.shape[1] // NA_DIM
    starts, variants, keys, _ = _na_plan(s // GRID_W)
    blk0 = b * s // ctx
    lat = pl.BlockSpec((s, NA_DIM), lambda bi, h: (bi, h))
    cx = pl.BlockSpec((ctx, NA_DIM), lambda bi, h: (blk0 + bi, h))
    nvar = len(keys)
    return pl.pallas_call(
        functools.partial(_na_kernel, starts=tuple(starts), variants=tuple(variants)),
        grid=(b, nh),
        in_specs=[lat, lat, lat, cx, cx,
                  pl.BlockSpec((nvar, 1) + bias.shape[2:], lambda bi, h: (0, h, 0, 0))],
        out_specs=lat,
        out_shape=jax.ShapeDtypeStruct((r, nh * NA_DIM), BF16),
        compiler_params=_cparams("parallel", "arbitrary"),
        name="na_latent",
    )(q, k, v, k, v, bias)


def _na_ctx(q, k, v, o_lat, dims):
    b, s, ctx = dims["b"], dims["s"], dims["ctx"]
    nh = q.shape[1] // NA_DIM
    blk0 = b * s // ctx
    cx = pl.BlockSpec((ctx, NA_DIM), lambda bi, h: (blk0 + bi, h))
    return pl.pallas_call(
        _na_ctx_kernel,
        grid=(b, nh),
        in_specs=[cx, cx, cx, pl.BlockSpec(memory_space=pl.ANY)],
        out_specs=cx,
        out_shape=jax.ShapeDtypeStruct(o_lat.shape, BF16),
        input_output_aliases={3: 0},
        compiler_params=_cparams("parallel", "parallel"),
        name="na_ctx",
    )(q, k, v, o_lat)


def _outproj_kernel(xa_ref, xb_ref, wa_ref, wb_ref, h_ref, gate_ref, o_ref):
    y = _mm(xa_ref[...], wa_ref[...]) + _mm(xb_ref[...], wb_ref[...])
    o_ref[...] = h_ref[...] + gate_ref[0] * y


def _outproj(xa, xb, w, hh, mods, n_tiles, dims):
    r, d = hh.shape
    tm = dims["tm"]
    ka, kb = xa.shape[1], xb.shape[1]
    assert ka % kb == 0
    hspec, mod_spec = _row_specs(tm, d, dims["tpb"], dims["b"])
    one = lambda spec: pl.BlockSpec(spec.block_shape, lambda i: spec.index_map(i, 0))
    return pl.pallas_call(
        _outproj_kernel,
        grid=(n_tiles,),
        in_specs=[pl.BlockSpec((tm, ka), lambda i: (i, 0)), pl.BlockSpec((tm, kb), lambda i: (i, 0)),
                  pl.BlockSpec((ka, d), lambda i: (0, 0)), pl.BlockSpec((kb, d), lambda i: (ka // kb, 0)),
                  one(hspec), one(mod_spec(2))],
        out_specs=one(hspec),
        out_shape=jax.ShapeDtypeStruct((r, d), F32),
        input_output_aliases={4: 0},
        compiler_params=_cparams("parallel"),
        name="outproj",
    )(xa, xb, w, w, hh, mods)


def _ffn_kernel(h_ref, sh_ref, sc_ref, gate_ref, gain_ref, wg_ref, wv_ref, wo_ref, o_ref, u_scr, acc_scr):
    j = pl.program_id(1)

    @pl.when(j == 0)
    def _():
        u_scr[...] = _norm_mod(h_ref[...], gain_ref[...], sh_ref[0], sc_ref[0]).astype(BF16)
        acc_scr[...] = jnp.zeros(acc_scr.shape, F32)

    u = u_scr[...]
    g = _mm(u, wg_ref[...])
    val = _mm(u, wv_ref[...])
    act = (g * jax.nn.sigmoid(g) * val).astype(BF16)
    acc_scr[...] += _mm(act, wo_ref[...])

    @pl.when(j == pl.num_programs(1) - 1)
    def _():
        o_ref[...] = h_ref[...] + gate_ref[0] * acc_scr[...]


def _ffn(hh, mods, gain, w_in, w_out, n_tiles, dims):
    r, d = hh.shape
    tm = dims["tm"]
    hid = w_out.shape[0]
    th = 512 if hid % 512 == 0 else 256
    nj = hid // th
    hspec, mod_spec = _row_specs(tm, d, dims["tpb"], dims["b"])
    in_place = n_tiles * tm == r
    return pl.pallas_call(
        _ffn_kernel,
        grid=(n_tiles, nj),
        in_specs=[hspec, mod_spec(3), mod_spec(4), mod_spec(5), pl.BlockSpec((1, d), lambda i, j: (0, 0)),
                  pl.BlockSpec((d, th), lambda i, j: (0, j)), pl.BlockSpec((d, th), lambda i, j: (0, nj + j)),
                  pl.BlockSpec((th, d), lambda i, j: (j, 0))],
        out_specs=hspec,
        out_shape=jax.ShapeDtypeStruct((n_tiles * tm, d), F32),
        input_output_aliases={0: 0} if in_place else {},
        scratch_shapes=[pltpu.VMEM((tm, d), BF16), pltpu.VMEM((tm, d), F32)],
        compiler_params=_cparams("parallel", "arbitrary"),
        name="ffn",
    )(hh, mods, mods, mods, gain, w_in, w_in, w_out)


def _rope_tables(dims):
    s, b, ctx = dims["s"], dims["b"], dims["ctx"]
    t = np.arange(s)
    n_freq = DIFF_DIM // 4
    inv = (ROPE_BASE ** (-np.arange(n_freq, dtype=np.float32) / n_freq)).astype(np.float32)
    lane = np.arange(LANES) % DIFF_DIM
    chunk = lane // n_freq
    pos = np.where((chunk // 2)[None, :] == 0, (t // GRID_W)[:, None], (t % GRID_W)[:, None]).astype(np.float32)
    ang = pos * inv[lane % n_freq][None, :]
    cos, sin = np.cos(ang), np.sin(ang)
    s_lo = np.where((chunk % 2)[None, :] == 1, sin, 0.0)
    s_hi = np.where((chunk % 2)[None, :] == 0, -sin, 0.0)
    pad = b * ctx

    def full(tab, fill):
        return jnp.asarray(np.concatenate([np.tile(tab, (b, 1)), np.full((pad, LANES), fill)], axis=0), dtype=F32)

    return full(cos, 1.0), full(s_lo, 0.0), full(s_hi, 0.0)


def kernel(x, c, ctx, c_ctx, w_mod, b_mod, norm_gain, w_ffn_in, w_ffn_out, ev_w_in, ev_w_out, ev_qk_gain,
           ev_lambda, ev_subln_gain, od_w_in, od_w_out, od_qk_gain, od_conv_w, od_rpb):
    b, s, d = x.shape
    n_ctx = ctx.shape[1]
    depth = w_mod.shape[0]
    tm = b * n_ctx
    r = b * s + b * n_ctx
    assert s % tm == 0 and s % GRID_W == 0
    ev_v = ev_w_out.shape[1] - FOURIER_GROUPS * FOURIER_DIM
    dims = dict(b=b, s=s, ctx=n_ctx, r=r, tm=tm, tm_in=min(1024, s), tn=512, tpb=s // tm,
                ev_q=ev_v // DIFF_V * 2 * DIFF_DIM, ev_v=ev_v, fw=FOURIER_GROUPS * FOURIER_DIM,
                cw=od_conv_w.shape[2], nw=od_rpb.shape[1] * NA_DIM,
                tq=min(256, s), tk=min(512, s))
    n_all, n_lat = r // tm, b * s // tm

    hh = jnp.concatenate([x.reshape(b * s, d), ctx.reshape(b * n_ctx, d)], axis=0)
    crow = jnp.concatenate([c, c_ctx[None, :], jnp.zeros((8 - b - 1, d), F32)], axis=0)
    mod_all = _mod_vectors(crow, w_mod, b_mod)
    rope_tabs = _rope_tables(dims)
    tn = dims["tn"]

    for i in range(depth):
        last = i == depth - 1
        n_out = n_lat if last else n_all
        mods = mod_all[i].reshape(8 * N_MOD, 1, d)
        gain1 = norm_gain[i, 0].reshape(1, d)
        gain2 = norm_gain[i, 1].reshape(1, d)
        j = i // 2
        if i % 2 == 0:
            lam_init = 0.8 - 0.6 * math.exp(-0.3 * i)
            gq = jnp.tile(ev_qk_gain[j, 0], tn // DIFF_DIM).reshape(1, tn)
            gk = jnp.tile(ev_qk_gain[j, 1], tn // DIFF_DIM).reshape(1, tn)
            qn, qr, kr, v, f = _inproj_even(hh, mods, gain1, ev_w_in[j].astype(BF16), gq, gk, rope_tabs, dims)
            subln = ev_subln_gain[j].reshape(1, DIFF_V)
            o = _diff_attn_latent(ev_lambda[j], subln, qn, qr, kr, v, dims, lam_init)
            fo = _fourier_latent(f, dims)
            if not last:
                o = _diff_attn_ctx(ev_lambda[j], subln, qn, kr, v, o, dims, lam_init)
                fo = _fourier_ctx(f, fo, dims)
            hh = _outproj(o, fo, ev_w_out[j].astype(BF16), hh, mods, n_out, dims)
        else:
            gq = jnp.tile(od_qk_gain[j, 0], tn // NA_DIM).reshape(1, tn)
            gk = jnp.tile(od_qk_gain[j, 1], tn // NA_DIM).reshape(1, tn)
            cg, q, k, v = _inproj_odd(hh, mods, gain1, od_w_in[j].astype(BF16), gq, gk, dims)
            cb = _short_conv(cg, od_conv_w[j], dims)
            o = _na_latent(q, k, v, _na_bias(od_rpb[j], s // GRID_W), dims)
            if not last:
                o = _na_ctx(q, k, v, o, dims)
            hh = _outproj(cb, o, od_w_out[j].astype(BF16), hh, mods, n_out, dims)
        hh = _ffn(hh, mods, gain2, w_ffn_in[i].astype(BF16), w_ffn_out[i].astype(BF16), n_out, dims)

    return hh.reshape(b, s, d)
```

```python
import functools
import math

import jax
import jax.numpy as jnp
import numpy as np
from jax import lax
from jax.experimental import pallas as pl
from jax.experimental.pallas import tpu as pltpu

GRID_W = 64
EPS = 1e-6
N_MOD = 6

DIFF_HEADS = 12
DIFF_DIM = 64
DIFF_V = 2 * DIFF_DIM
FOURIER_GROUPS = 4
FOURIER_DIM = 128
ROPE_BASE = 10000.0

CONV_WIDTH = 1024
CONV_K = 3
NA_HEADS = 8
NA_DIM = 128
NA_WIN_ROWS = 8
NA_WIN_COLS = 16

F32 = jnp.float32
BF16 = jnp.bfloat16
LOG2E = 1.4426950408889634
NEG_BIG = -1e30
LANES = 128
MXU_DIM = 256
VMEM_LIMIT = 56 * 1024 * 1024
ATTN_CHUNKS_PER_TRIP = 8


def _cparams(*sem):
    return pltpu.CompilerParams(dimension_semantics=sem, vmem_limit_bytes=VMEM_LIMIT)


def _mm(a, b):
    return jnp.dot(a, b, preferred_element_type=F32)


def _mm_nt(a, b):
    return lax.dot_general(a, b, (((1,), (1,)), ((), ())), preferred_element_type=F32)


def _tile_lanes(x, n):
    return x if n == 1 else jnp.concatenate([x] * n, axis=1)


def _mod_kernel(c_ref, w_ref, b_ref, o_ref):
    c = c_ref[...]
    s = (c * jax.nn.sigmoid(c)).astype(BF16)
    o_ref[0] = _mm(s, w_ref[0].astype(BF16)) + b_ref[0]


def _mod_vectors(crow, w_mod, b_mod):
    depth, d, nm = w_mod.shape
    tn = 1024 if nm % 1024 == 0 else 512
    return pl.pallas_call(
        _mod_kernel,
        grid=(depth, nm // tn),
        in_specs=[
            pl.BlockSpec((8, d), lambda l, j: (0, 0)),
            pl.BlockSpec((1, d, tn), lambda l, j: (l, 0, j)),
            pl.BlockSpec((1, 1, tn), lambda l, j: (l, 0, j)),
        ],
        out_specs=pl.BlockSpec((1, 8, tn), lambda l, j: (l, 0, j)),
        out_shape=jax.ShapeDtypeStruct((depth, 8, nm), F32),
        compiler_params=_cparams("parallel", "parallel"),
        name="mod_vectors",
    )(crow, w_mod, b_mod.reshape(depth, 1, nm))


def _norm_mod(x, gain, shift, scale):
    y = x * lax.rsqrt(jnp.mean(x * x, axis=-1, keepdims=True) + EPS) * gain
    return y * (1.0 + scale) + shift


def _group_rms(x, ones_bd, gain, group):
    x2 = (x * x).astype(BF16)
    parts = [_mm(x2[:, c:c + MXU_DIM], ones_bd) for c in range(0, x.shape[1], MXU_DIM)]
    ms = jnp.concatenate(parts, axis=1) * (1.0 / group)
    return x * lax.rsqrt(ms + EPS) * gain


def _rope(x, cos, s_lo, s_hi):
    q = DIFF_DIM // 4
    outs = []
    for c in range(0, x.shape[1], LANES):
        xs = x[:, c:c + LANES]
        outs.append(xs * cos + pltpu.roll(xs, q, 1) * s_lo + pltpu.roll(xs, LANES - q, 1) * s_hi)
    return jnp.concatenate(outs, axis=1)


def _inproj_even_kernel(h_ref, sh_ref, sc_ref, gain_ref, w_ref, gq_ref, gk_ref, ones_ref, cos_ref, slo_ref, shi_ref,
                        qn_ref, qr_ref, kr_ref, v_ref, f_ref, a_scr, *, nq, nv, qscale):
    j = pl.program_id(1)

    @pl.when(j == 0)
    def _():
        a_scr[...] = _norm_mod(h_ref[...], gain_ref[...], sh_ref[0], sc_ref[0]).astype(BF16)

    acc = _mm(a_scr[...], w_ref[...])

    @pl.when(j < nq)
    def _():
        qn = _group_rms(acc, ones_ref[...], gq_ref[...], DIFF_DIM) * qscale
        qn_ref[...] = qn.astype(BF16)
        qr_ref[...] = _rope(qn, cos_ref[...], slo_ref[...], shi_ref[...]).astype(BF16)

    @pl.when(jnp.logical_and(j >= nq, j < 2 * nq))
    def _():
        kn = _group_rms(acc, ones_ref[...], gk_ref[...], DIFF_DIM)
        kr_ref[...] = _rope(kn, cos_ref[...], slo_ref[...], shi_ref[...]).astype(BF16)

    @pl.when(jnp.logical_and(j >= 2 * nq, j < 2 * nq + nv))
    def _():
        v_ref[...] = acc.astype(BF16)

    @pl.when(j >= 2 * nq + nv)
    def _():
        f_ref[...] = acc.astype(BF16)


def _inproj_odd_kernel(h_ref, sh_ref, sc_ref, gain_ref, w_ref, gq_ref, gk_ref, ones_ref,
                       cg_ref, q_ref, k_ref, v_ref, a_scr, *, ncg, nh, qscale):
    j = pl.program_id(1)

    @pl.when(j == 0)
    def _():
        a_scr[...] = _norm_mod(h_ref[...], gain_ref[...], sh_ref[0], sc_ref[0]).astype(BF16)

    acc = _mm(a_scr[...], w_ref[...])

    @pl.when(j < ncg)
    def _():
        cg_ref[...] = acc.astype(BF16)

    @pl.when(jnp.logical_and(j >= ncg, j < ncg + nh))
    def _():
        q_ref[...] = (_group_rms(acc, ones_ref[...], gq_ref[...], NA_DIM) * qscale).astype(BF16)

    @pl.when(jnp.logical_and(j >= ncg + nh, j < ncg + 2 * nh))
    def _():
        k_ref[...] = _group_rms(acc, ones_ref[...], gk_ref[...], NA_DIM).astype(BF16)

    @pl.when(j >= ncg + 2 * nh)
    def _():
        v_ref[...] = acc.astype(BF16)


def _row_specs(tm, d, tiles_per_batch, n_batch):
    def mod_spec(k):
        return pl.BlockSpec((1, 1, d), lambda i, j: (jnp.minimum(i // tiles_per_batch, n_batch) * N_MOD + k, 0, 0))

    return pl.BlockSpec((tm, d), lambda i, j: (i, 0)), mod_spec


def _clamped(tm, tn, lo, n):
    return pl.BlockSpec((tm, tn), lambda i, j: (i, jnp.clip(j - lo, 0, n - 1)))


def _ones_blockdiag(group):
    idx = np.arange(MXU_DIM) // group
    return jnp.asarray((idx[:, None] == idx[None, :]).astype(np.float32), dtype=BF16)


def _inproj_even(hh, mods, gain, w, layer, gq, gk, rope_tabs, dims):
    r, d = hh.shape
    tm, tn = dims["tm_in"], dims["tn"]
    ev_q, ev_v, fw = dims["ev_q"], dims["ev_v"], dims["fw"]
    nq, nv, nf = ev_q // tn, ev_v // tn, fw // tn
    nj = 2 * nq + nv + nf
    hspec, mod_spec = _row_specs(tm, d, dims["s"] // tm, dims["b"])
    const = lambda shape: pl.BlockSpec(shape, lambda i, j: (0,) * len(shape))
    tab = pl.BlockSpec((tm, LANES), lambda i, j: (i, 0))
    kern = functools.partial(_inproj_even_kernel, nq=nq, nv=nv, qscale=DIFF_DIM ** -0.5 * LOG2E)
    return pl.pallas_call(
        kern,
        grid=(pl.cdiv(r, tm), nj),
        in_specs=[hspec, mod_spec(0), mod_spec(1), const((1, d)),
                  pl.BlockSpec((None, d, tn), lambda i, j: (layer, 0, j)),
                  const((1, tn)), const((1, tn)), const((MXU_DIM, MXU_DIM)), tab, tab, tab],
        out_specs=[_clamped(tm, tn, 0, nq), _clamped(tm, tn, 0, nq), _clamped(tm, tn, nq, nq),
                   _clamped(tm, tn, 2 * nq, nv), _clamped(tm, tn, 2 * nq + nv, nf)],
        out_shape=[jax.ShapeDtypeStruct((r, ev_q), BF16)] * 3
        + [jax.ShapeDtypeStruct((r, ev_v), BF16), jax.ShapeDtypeStruct((r, fw), BF16)],
        scratch_shapes=[pltpu.VMEM((tm, d), BF16)],
        compiler_params=_cparams("parallel", "arbitrary"),
        name="inproj_even",
    )(hh, mods, mods, gain, w, gq, gk, _ones_blockdiag(DIFF_DIM), *rope_tabs)


def _inproj_odd(hh, mods, gain, w, layer, gq, gk, dims):
    r, d = hh.shape
    tm, tn = dims["tm_in"], dims["tn"]
    cw, nw = dims["cw"], dims["nw"]
    ncg, nh = 3 * cw // tn, nw // tn
    nj = ncg + 3 * nh
    hspec, mod_spec = _row_specs(tm, d, dims["s"] // tm, dims["b"])
    const = lambda shape: pl.BlockSpec(shape, lambda i, j: (0,) * len(shape))
    kern = functools.partial(_inproj_odd_kernel, ncg=ncg, nh=nh, qscale=NA_DIM ** -0.5 * LOG2E)
    return pl.pallas_call(
        kern,
        grid=(pl.cdiv(r, tm), nj),
        in_specs=[hspec, mod_spec(0), mod_spec(1), const((1, d)),
                  pl.BlockSpec((None, d, tn), lambda i, j: (layer, 0, j)),
                  const((1, tn)), const((1, tn)), const((MXU_DIM, MXU_DIM))],
        out_specs=[_clamped(tm, tn, 0, ncg), _clamped(tm, tn, ncg, nh), _clamped(tm, tn, ncg + nh, nh),
                   _clamped(tm, tn, ncg + 2 * nh, nh)],
        out_shape=[jax.ShapeDtypeStruct((r, 3 * cw), BF16)] + [jax.ShapeDtypeStruct((r, nw), BF16)] * 3,
        scratch_shapes=[pltpu.VMEM((tm, d), BF16)],
        compiler_params=_cparams("parallel", "arbitrary"),
        name="inproj_odd",
    )(hh, mods, mods, gain, w, gq, gk, _ones_blockdiag(NA_DIM))


def _split_components(q):
    lane = lax.broadcasted_iota(jnp.int32, q.shape, 1)
    zero = jnp.zeros_like(q)
    return jnp.concatenate([jnp.where(lane < DIFF_DIM, q, zero), jnp.where(lane >= DIFF_DIM, q, zero)], axis=0)


def _softmax_steps(q2, kvs, m_scr, acc_scr):
    scores = [_mm_nt(q2, kk) for kk, _ in kvs]
    m, acc = m_scr[...], acc_scr[...]
    for s, (_, vv) in zip(scores, kvs):
        m_new = jnp.maximum(m, jnp.max(s, axis=-1, keepdims=True))
        alpha = jnp.exp2(m - m_new)
        p = jnp.exp2((s - _tile_lanes(m_new, s.shape[1] // LANES)).astype(BF16))
        v_ext = jnp.concatenate([vv, jnp.ones_like(vv)], axis=1)
        acc = _tile_lanes(alpha, 2) * acc + _mm(p, v_ext)
        m = m_new
    m_scr[...], acc_scr[...] = m, acc


def _diff_attn_kernel(*refs, n_chunks, tk, lam_init):
    if n_chunks:
        lam_ref, g_ref, qn_ref, qr_ref, kc_ref, vc_ref, k_ref, v_ref, o_ref, m_scr, acc_scr = refs
    else:
        lam_ref, g_ref, qn_ref, kc_ref, vc_ref, o_ref, m_scr, acc_scr = refs
    tq = qn_ref.shape[0]
    m_scr[...] = jnp.full(m_scr.shape, NEG_BIG, F32)
    acc_scr[...] = jnp.zeros(acc_scr.shape, F32)

    _softmax_steps(_split_components(qn_ref[...]), [(kc_ref[...], vc_ref[...])], m_scr, acc_scr)

    if n_chunks:
        q2 = _split_components(qr_ref[...])
        per_trip = math.gcd(ATTN_CHUNKS_PER_TRIP, n_chunks)

        def body(c, carry):
            kvs = []
            for u in range(per_trip):
                off = pl.multiple_of((c * per_trip + u) * tk, tk)
                kvs.append((k_ref[pl.ds(off, tk), :], v_ref[pl.ds(off, tk), :]))
            _softmax_steps(q2, kvs, m_scr, acc_scr)
            return carry

        lax.fori_loop(0, n_chunks // per_trip, body, 0)

    lv = lam_ref[...]
    lam = (jnp.exp(jnp.sum(lv[0:1] * lv[1:2], axis=-1, keepdims=True))
           - jnp.exp(jnp.sum(lv[2:3] * lv[3:4], axis=-1, keepdims=True)) + lam_init)
    acc = acc_scr[...]
    o2 = acc[:, :DIFF_V] / acc[:, DIFF_V:]
    o = o2[:tq] - lam * o2[tq:]
    o = o * lax.rsqrt(jnp.mean(o * o, axis=-1, keepdims=True) + EPS) * g_ref[...] * (1.0 - lam_init)
    o_ref[...] = o.astype(BF16)


def _diff_attn_latent(lam_vec, subln, qn, qr, kr, v, dims, lam_init):
    b, s, ctx, r = dims["b"], dims["s"], dims["ctx"], dims["r"]
    tq, tk = dims["tq"], dims["tk"]
    nh = qn.shape[1] // DIFF_V
    ctx_blk0 = b * s // ctx
    kern = functools.partial(_diff_attn_kernel, n_chunks=s // tk, tk=tk, lam_init=lam_init)
    qspec = pl.BlockSpec((tq, DIFF_V), lambda bi, h, qi: (bi * (s // tq) + qi, h))
    cspec = pl.BlockSpec((ctx, DIFF_V), lambda bi, h, qi: (ctx_blk0 + bi, h))
    kspec = pl.BlockSpec((s, DIFF_V), lambda bi, h, qi: (bi, h))
    return pl.pallas_call(
        kern,
        grid=(b, nh, s // tq),
        in_specs=[pl.BlockSpec((4, DIFF_DIM), lambda bi, h, qi: (0, 0)),
                  pl.BlockSpec((1, DIFF_V), lambda bi, h, qi: (0, 0)),
                  qspec, qspec, cspec, cspec, kspec, kspec],
        out_specs=qspec,
        out_shape=jax.ShapeDtypeStruct((r, nh * DIFF_V), BF16),
        scratch_shapes=[pltpu.VMEM((2 * tq, LANES), F32), pltpu.VMEM((2 * tq, 2 * DIFF_V), F32)],
        compiler_params=_cparams("parallel", "parallel", "arbitrary"),
        name="diff_attn_latent",
    )(lam_vec, subln, qn, qr, kr, v, kr, v)


def _diff_attn_ctx(lam_vec, subln, qn, kr, v, o_lat, dims, lam_init):
    b, s, ctx = dims["b"], dims["s"], dims["ctx"]
    nh = qn.shape[1] // DIFF_V
    ctx_blk0 = b * s // ctx
    kern = functools.partial(_diff_attn_kernel, n_chunks=0, tk=0, lam_init=lam_init)
    cspec = pl.BlockSpec((ctx, DIFF_V), lambda bi, h: (ctx_blk0 + bi, h))

    def wrapped(lam_ref, g_ref, qn_ref, kc_ref, vc_ref, alias_ref, o_ref, m_scr, acc_scr):
        del alias_ref
        kern(lam_ref, g_ref, qn_ref, kc_ref, vc_ref, o_ref, m_scr, acc_scr)

    return pl.pallas_call(
        wrapped,
        grid=(b, nh),
        in_specs=[pl.BlockSpec((4, DIFF_DIM), lambda bi, h: (0, 0)),
                  pl.BlockSpec((1, DIFF_V), lambda bi, h: (0, 0)),
                  cspec, cspec, cspec, pl.BlockSpec(memory_space=pl.ANY)],
        out_specs=cspec,
        out_shape=jax.ShapeDtypeStruct(o_lat.shape, BF16),
        input_output_aliases={5: 0},
        scratch_shapes=[pltpu.VMEM((2 * ctx, LANES), F32), pltpu.VMEM((2 * ctx, 2 * DIFF_V), F32)],
        compiler_params=_cparams("parallel", "parallel"),
        name="diff_attn_ctx",
    )(lam_vec, subln, qn, kr, v, o_lat)


def _dft_cs(n):
    k = np.arange(n)
    ang = 2.0 * np.pi * ((k[:, None] * k[None, :]) % n) / n
    return np.cos(ang), np.sin(ang)


def _channel_dft(groups, scale):
    c, s = _dft_cs(FOURIER_DIM)
    eye = np.eye(groups)
    return np.kron(eye, c) * scale, np.kron(eye, s) * scale


def _fourier_rows_kernel(x_ref, w_ref, tc_ref, ts_ref, y_ref, *, cg, fw, nr):
    for c in range(cg):
        y = _mm(w_ref[...], x_ref[:, c * fw:(c + 1) * fw])
        yr, yi = y[:nr], y[nr:]
        tc = _tile_lanes(tc_ref[c], fw // LANES)
        ts = _tile_lanes(ts_ref[c], fw // LANES)
        y_ref[0, 0, :, c * fw:(c + 1) * fw] = (yr * tc + yi * ts).astype(BF16)
        y_ref[0, 1, :, c * fw:(c + 1) * fw] = (yi * tc - yr * ts).astype(BF16)


def _fourier_cols_kernel(y_ref, w2_ref, wc_ref, ws_ref, o_ref, *, ag, fw, nc):
    for a in range(ag):
        yy = jnp.concatenate([y_ref[0, 0, a], y_ref[0, 1, a]], axis=0)
        z = _mm(w2_ref[...], yy)
        out = _mm(z[:nc].astype(BF16), wc_ref[...]) + _mm(z[nc:].astype(BF16), ws_ref[...])
        o_ref[:, a * fw:(a + 1) * fw] = out.astype(BF16)


def _fourier_dense_kernel(x_ref, wc_ref, ws_ref, cn_ref, sn_ref, alias_ref, o_ref):
    del alias_ref
    x = x_ref[...]
    gc = _mm(x, wc_ref[...]).astype(BF16)
    gs = _mm(x, ws_ref[...]).astype(BF16)
    o_ref[...] = (_mm(cn_ref[...], gc) - _mm(sn_ref[...], gs)).astype(BF16)


def _fourier_latent(f, dims):
    b, s, r = dims["b"], dims["s"], dims["r"]
    fw = f.shape[1]
    groups = fw // FOURIER_DIM
    nc = GRID_W
    nr = s // nc
    cg = 4
    ag = 8
    c1, s1 = _dft_cs(nr)
    w1 = jnp.asarray(np.concatenate([c1, -s1], axis=0), dtype=BF16)
    ang = 2.0 * np.pi * (np.arange(nc)[:, None] * np.arange(nr)[None, :]) / s
    tcos = jnp.asarray(np.repeat(np.cos(ang)[:, :, None], LANES, axis=2), dtype=F32)
    tsin = jnp.asarray(np.repeat(np.sin(ang)[:, :, None], LANES, axis=2), dtype=F32)
    x2d = f.reshape(r // nc, nc * fw)
    y = pl.pallas_call(
        functools.partial(_fourier_rows_kernel, cg=cg, fw=fw, nr=nr),
        grid=(b, nc // cg),
        in_specs=[pl.BlockSpec((nr, cg * fw), lambda bi, j: (bi, j)),
                  pl.BlockSpec((2 * nr, nr), lambda bi, j: (0, 0)),
                  pl.BlockSpec((cg, nr, LANES), lambda bi, j: (j, 0, 0)),
                  pl.BlockSpec((cg, nr, LANES), lambda bi, j: (j, 0, 0))],
        out_specs=pl.BlockSpec((1, 2, nr, cg * fw), lambda bi, j: (bi, 0, 0, j)),
        out_shape=jax.ShapeDtypeStruct((b, 2, nr, nc * fw), BF16),
        compiler_params=_cparams("parallel", "parallel"),
        name="fourier_rows",
    )(x2d, w1, tcos, tsin)
    c2, s2 = _dft_cs(nc)
    w2 = jnp.asarray(np.block([[c2, s2], [-s2, c2]]), dtype=BF16)
    wc, ws = _channel_dft(groups, 1.0 / math.sqrt(s * FOURIER_DIM))
    out = pl.pallas_call(
        functools.partial(_fourier_cols_kernel, ag=ag, fw=fw, nc=nc),
        grid=(b, nr // ag),
        in_specs=[pl.BlockSpec((1, 2, ag, nc, fw), lambda bi, j: (bi, 0, j, 0, 0)),
                  pl.BlockSpec((2 * nc, 2 * nc), lambda bi, j: (0, 0)),
                  pl.BlockSpec((fw, fw), lambda bi, j: (0, 0)),
                  pl.BlockSpec((fw, fw), lambda bi, j: (0, 0))],
        out_specs=pl.BlockSpec((nc, ag * fw), lambda bi, j: (bi, j)),
        out_shape=jax.ShapeDtypeStruct((r // nr, nr * fw), BF16),
        compiler_params=_cparams("parallel", "parallel"),
        name="fourier_cols",
    )(y.reshape(b, 2, nr, nc, fw), w2, jnp.asarray(wc, dtype=BF16), jnp.asarray(ws, dtype=BF16))
    return out.reshape(r, fw)


def _fourier_ctx(f, fo, dims):
    b, s, ctx = dims["b"], dims["s"], dims["ctx"]
    fw = f.shape[1]
    groups = fw // FOURIER_DIM
    wc, ws = _channel_dft(groups, 1.0 / math.sqrt(ctx * FOURIER_DIM))
    cn, sn = _dft_cs(ctx)
    blk0 = b * s // ctx
    rows = pl.BlockSpec((ctx, fw), lambda bi: (blk0 + bi, 0))
    const = lambda shape: pl.BlockSpec(shape, lambda bi: (0, 0))
    return pl.pallas_call(
        _fourier_dense_kernel,
        grid=(b,),
        in_specs=[rows, const((fw, fw)), const((fw, fw)), const((ctx, ctx)), const((ctx, ctx)),
                  pl.BlockSpec(memory_space=pl.ANY)],
        out_specs=rows,
        out_shape=jax.ShapeDtypeStruct(fo.shape, BF16),
        input_output_aliases={5: 0},
        compiler_params=_cparams("parallel"),
        name="fourier_ctx",
    )(f, jnp.asarray(wc, dtype=BF16), jnp.asarray(ws, dtype=BF16),
      jnp.asarray(cn, dtype=BF16), jnp.asarray(sn, dtype=BF16), fo)


def _conv_kernel(gb_ref, gc_ref, hh_ref, gcp_ref, hhp_ref, gcn_ref, hhn_ref, w_ref, o_ref, *, n_lat_tiles, s, ctx):
    i = pl.program_id(0)
    tm = gb_ref.shape[0]
    u = gc_ref[...].astype(F32) * hh_ref[...].astype(F32)
    halo = gcp_ref.shape[0]
    u_prev = gcp_ref[halo - 1:halo, :].astype(F32) * hhp_ref[halo - 1:halo, :].astype(F32)
    u_next = gcn_ref[0:1, :].astype(F32) * hhn_ref[0:1, :].astype(F32)
    row = lax.broadcasted_iota(jnp.int32, (tm, 1), 0)
    seq = jnp.where(i < n_lat_tiles, s, ctx)
    pos = lax.rem(i * tm + row, seq)
    dn = jnp.where(row == 0, u_prev, pltpu.roll(u, 1, 0))
    dn = jnp.where(pos == 0, 0.0, dn)
    up = jnp.where(row == tm - 1, u_next, pltpu.roll(u, tm - 1, 0))
    up = jnp.where(pos == seq - 1, 0.0, up)
    w = w_ref[...]
    y = dn * w[0:1] + u * w[1:2] + up * w[2:3]
    o_ref[...] = (gb_ref[...].astype(F32) * y).astype(BF16)


def _short_conv(cg, conv_w, dims):
    r = cg.shape[0]
    cw, tm = dims["cw"], dims["tm"]
    halo = 16
    tc = min(cw, 1024)
    ncol = cw // tc
    nhb = tm // halo
    last = r // halo - 1
    cur = lambda off: pl.BlockSpec((tm, tc), lambda i, j: (i, off * ncol + j))
    prev = lambda off: pl.BlockSpec((halo, tc), lambda i, j: (jnp.maximum(i * nhb - 1, 0), off * ncol + j))
    nxt = lambda off: pl.BlockSpec((halo, tc), lambda i, j: (jnp.minimum((i + 1) * nhb, last), off * ncol + j))
    kern = functools.partial(_conv_kernel, n_lat_tiles=dims["b"] * dims["tpb"], s=dims["s"], ctx=dims["ctx"])
    return pl.pallas_call(
        kern,
        grid=(r // tm, ncol),
        in_specs=[cur(0), cur(1), cur(2), prev(1), prev(2), nxt(1), nxt(2),
                  pl.BlockSpec((CONV_K, tc), lambda i, j: (0, j))],
        out_specs=pl.BlockSpec((tm, tc), lambda i, j: (i, j)),
        out_shape=jax.ShapeDtypeStruct((r, cw), BF16),
        compiler_params=_cparams("parallel", "parallel"),
        name="short_conv",
    )(cg, cg, cg, cg, cg, cg, cg, conv_w)


NA_BLOCK_ROWS = 8
NA_KEY_ROWS = 16
NA_BLOCKS_PER_TRIP = 4


def _na_plan(rows):
    wr = min(NA_WIN_ROWS, rows)
    starts, variants, keys = [], [], {}
    for jb in range(rows // NA_BLOCK_ROWS):
        r0 = jb * NA_BLOCK_ROWS
        ks = int(np.clip(r0 - wr // 2, 0, rows - NA_KEY_ROWS))
        rs = [int(np.clip(r0 + t - wr // 2, 0, rows - wr)) for t in range(NA_BLOCK_ROWS)]
        key = (r0 - ks, tuple(x - ks for x in rs))
        variants.append(keys.setdefault(key, len(keys)))
        starts.append(ks)
    return starts, variants, list(keys), wr


def _na_bias(rpb, rows):
    _, _, keys, wr = _na_plan(rows)
    col = np.arange(GRID_W)
    cs = np.clip(col - NA_WIN_COLS // 2, 0, GRID_W - NA_WIN_COLS)
    kc = np.arange(GRID_W)
    dc = kc[None, :] - col[:, None] + (NA_WIN_COLS - 1)
    col_ok = (kc[None, :] >= cs[:, None]) & (kc[None, :] < cs[:, None] + NA_WIN_COLS)
    n_dr, n_dc = 2 * NA_WIN_ROWS - 1, 2 * NA_WIN_COLS - 1
    col_sel = ((dc[None] == np.arange(n_dc)[:, None, None]) & col_ok[None]).astype(np.float32)
    t = np.arange(NA_BLOCK_ROWS)
    i = np.arange(NA_KEY_ROWS)
    row_sel, row_okv = [], []
    for r0_rel, rs_rel in keys:
        dr = i[None, :] - (r0_rel + t[:, None]) + (NA_WIN_ROWS - 1)
        rs_arr = np.asarray(rs_rel)
        row_ok = (i[None, :] >= rs_arr[:, None]) & (i[None, :] < rs_arr[:, None] + wr)
        row_sel.append(((dr[None] == np.arange(n_dr)[:, None, None]) & row_ok[None]).astype(np.float32))
        row_okv.append(row_ok)
    row_sel = np.stack(row_sel)
    ok = (jnp.asarray(np.stack(row_okv))[:, None, :, None, :, None]
          & jnp.asarray(col_ok)[None, None, None, :, None, :])
    hi = lax.Precision.HIGHEST
    col_exp = jnp.einsum("hrd,dck->hrck", rpb.astype(F32) * LOG2E, jnp.asarray(col_sel), precision=hi)
    vals = jnp.einsum("vrti,hrck->vhtcik", jnp.asarray(row_sel), col_exp, precision=hi)
    vals = jnp.where(ok, vals, NEG_BIG)
    return vals.reshape(len(keys), rpb.shape[0], NA_BLOCK_ROWS * GRID_W, NA_KEY_ROWS * GRID_W)


def _na_kernel(q_ref, k_ref, v_ref, kc_ref, vc_ref, bias_ref, o_ref, *, starts, variants):
    qb = NA_BLOCK_ROWS * GRID_W
    kb = NA_KEY_ROWS * GRID_W

    def lookup(table, jb):
        out = jnp.int32(table[0])
        for idx in range(1, len(table)):
            out = jnp.where(jb == idx, jnp.int32(table[idx]), out)
        return out

    def block(jb):
        q0 = pl.multiple_of(jb * qb, qb)
        k0 = pl.multiple_of(lookup(starts, jb) * GRID_W, GRID_W)
        q = q_ref[pl.ds(q0, qb), :]
        s_nb = _mm_nt(q, k_ref[pl.ds(k0, kb), :]) + bias_ref[lookup(variants, jb), 0]
        s_cx = _mm_nt(q, kc_ref[...])
        m = jnp.maximum(jnp.max(s_nb, axis=-1, keepdims=True), jnp.max(s_cx, axis=-1, keepdims=True))
        p_nb = jnp.exp2(s_nb - m)
        p_cx = jnp.exp2(s_cx - m)
        l = jnp.sum(p_nb, axis=-1, keepdims=True) + jnp.sum(p_cx, axis=-1, keepdims=True)
        o = _mm(p_nb.astype(BF16), v_ref[pl.ds(k0, kb), :]) + _mm(p_cx.astype(BF16), vc_ref[...])
        o_ref[pl.ds(q0, qb), :] = (o / l).astype(BF16)

    per_trip = math.gcd(NA_BLOCKS_PER_TRIP, len(starts))

    def body(t, carry):
        for u in range(per_trip):
            block(t * per_trip + u)
        return carry

    lax.fori_loop(0, len(starts) // per_trip, body, 0)


def _na_ctx_kernel(q_ref, kc_ref, vc_ref, alias_ref, o_ref):
    del alias_ref
    s = _mm_nt(q_ref[...], kc_ref[...])
    p = jnp.exp2(s - jnp.max(s, axis=-1, keepdims=True))
    o = _mm(p.astype(BF16), vc_ref[...]) / jnp.sum(p, axis=-1, keepdims=True)
    o_ref[...] = o.astype(BF16)


def _na_latent(q, k, v, bias, dims):
    b, s, ctx, r = dims["b"], dims["s"], dims["ctx"], dims["r"]
    nh = q.shape[1] // NA_DIM
    starts, variants, keys, _ = _na_plan(s // GRID_W)
    blk0 = b * s // ctx
    lat = pl.BlockSpec((s, NA_DIM), lambda bi, h: (bi, h))
    cx = pl.BlockSpec((ctx, NA_DIM), lambda bi, h: (blk0 + bi, h))
    nvar = len(keys)
    return pl.pallas_call(
        functools.partial(_na_kernel, starts=tuple(starts), variants=tuple(variants)),
        grid=(b, nh),
        in_specs=[lat, lat, lat, cx, cx,
                  pl.BlockSpec((nvar, 1) + bias.shape[2:], lambda bi, h: (0, h, 0, 0))],
        out_specs=lat,
        out_shape=jax.ShapeDtypeStruct((r, nh * NA_DIM), BF16),
        compiler_params=_cparams("parallel", "arbitrary"),
        name="na_latent",
    )(q, k, v, k, v, bias)


def _na_ctx(q, k, v, o_lat, dims):
    b, s, ctx = dims["b"], dims["s"], dims["ctx"]
    nh = q.shape[1] // NA_DIM
    blk0 = b * s // ctx
    cx = pl.BlockSpec((ctx, NA_DIM), lambda bi, h: (blk0 + bi, h))
    return pl.pallas_call(
        _na_ctx_kernel,
        grid=(b, nh),
        in_specs=[cx, cx, cx, pl.BlockSpec(memory_space=pl.ANY)],
        out_specs=cx,
        out_shape=jax.ShapeDtypeStruct(o_lat.shape, BF16),
        input_output_aliases={3: 0},
        compiler_params=_cparams("parallel", "parallel"),
        name="na_ctx",
    )(q, k, v, o_lat)


def _outproj_kernel(xa_ref, xb_ref, wa_ref, wb_ref, h_ref, gate_ref, o_ref):
    y = _mm(xa_ref[...], wa_ref[...]) + _mm(xb_ref[...], wb_ref[...])
    o_ref[...] = h_ref[...] + gate_ref[0] * y


def _outproj(xa, xb, w, layer, hh, mods, n_tiles, dims):
    r, d = hh.shape
    tm = dims["tm"]
    ka, kb = xa.shape[1], xb.shape[1]
    assert ka % kb == 0
    hspec, mod_spec = _row_specs(tm, d, dims["tpb"], dims["b"])
    one = lambda spec: pl.BlockSpec(spec.block_shape, lambda i: spec.index_map(i, 0))
    return pl.pallas_call(
        _outproj_kernel,
        grid=(n_tiles,),
        in_specs=[pl.BlockSpec((tm, ka), lambda i: (i, 0)), pl.BlockSpec((tm, kb), lambda i: (i, 0)),
                  pl.BlockSpec((None, ka, d), lambda i: (layer, 0, 0)),
                  pl.BlockSpec((None, kb, d), lambda i: (layer, ka // kb, 0)),
                  one(hspec), one(mod_spec(2))],
        out_specs=one(hspec),
        out_shape=jax.ShapeDtypeStruct((r, d), F32),
        input_output_aliases={4: 0},
        compiler_params=_cparams("parallel"),
        name="outproj",
    )(xa, xb, w, w, hh, mods)


def _ffn_kernel(h_ref, sh_ref, sc_ref, gate_ref, gain_ref, wg_ref, wv_ref, wo_ref, o_ref, u_scr, acc_scr):
    j = pl.program_id(1)

    @pl.when(j == 0)
    def _():
        u_scr[...] = _norm_mod(h_ref[...], gain_ref[...], sh_ref[0], sc_ref[0]).astype(BF16)
        acc_scr[...] = jnp.zeros(acc_scr.shape, F32)

    u = u_scr[...]
    g = _mm(u, wg_ref[...])
    val = _mm(u, wv_ref[...])
    act = (g * jax.nn.sigmoid(g) * val).astype(BF16)
    acc_scr[...] += _mm(act, wo_ref[...])

    @pl.when(j == pl.num_programs(1) - 1)
    def _():
        o_ref[...] = h_ref[...] + gate_ref[0] * acc_scr[...]


def _ffn(hh, mods, gain, w_in, w_out, layer, n_tiles, dims):
    r, d = hh.shape
    tm = dims["tm"]
    hid = w_out.shape[1]
    th = 512 if hid % 512 == 0 else 256
    nj = hid // th
    hspec, mod_spec = _row_specs(tm, d, dims["tpb"], dims["b"])
    in_place = n_tiles * tm == r
    return pl.pallas_call(
        _ffn_kernel,
        grid=(n_tiles, nj),
        in_specs=[hspec, mod_spec(3), mod_spec(4), mod_spec(5), pl.BlockSpec((1, d), lambda i, j: (0, 0)),
                  pl.BlockSpec((None, d, th), lambda i, j: (layer, 0, j)),
                  pl.BlockSpec((None, d, th), lambda i, j: (layer, 0, nj + j)),
                  pl.BlockSpec((None, th, d), lambda i, j: (layer, j, 0))],
        out_specs=hspec,
        out_shape=jax.ShapeDtypeStruct((n_tiles * tm, d), F32),
        input_output_aliases={0: 0} if in_place else {},
        scratch_shapes=[pltpu.VMEM((tm, d), BF16), pltpu.VMEM((tm, d), F32)],
        compiler_params=_cparams("parallel", "arbitrary"),
        name="ffn",
    )(hh, mods, mods, mods, gain, w_in, w_in, w_out)


def _rope_tables(dims):
    s, b, ctx = dims["s"], dims["b"], dims["ctx"]
    t = np.arange(s)
    n_freq = DIFF_DIM // 4
    inv = (ROPE_BASE ** (-np.arange(n_freq, dtype=np.float32) / n_freq)).astype(np.float32)
    lane = np.arange(LANES) % DIFF_DIM
    chunk = lane // n_freq
    pos = np.where((chunk // 2)[None, :] == 0, (t // GRID_W)[:, None], (t % GRID_W)[:, None]).astype(np.float32)
    ang = pos * inv[lane % n_freq][None, :]
    cos, sin = np.cos(ang), np.sin(ang)
    s_lo = np.where((chunk % 2)[None, :] == 1, sin, 0.0)
    s_hi = np.where((chunk % 2)[None, :] == 0, -sin, 0.0)
    pad = b * ctx

    def full(tab, fill):
        return jnp.asarray(np.concatenate([np.tile(tab, (b, 1)), np.full((pad, LANES), fill)], axis=0), dtype=F32)

    return full(cos, 1.0), full(s_lo, 0.0), full(s_hi, 0.0)


def kernel(x, c, ctx, c_ctx, w_mod, b_mod, norm_gain, w_ffn_in, w_ffn_out, ev_w_in, ev_w_out, ev_qk_gain,
           ev_lambda, ev_subln_gain, od_w_in, od_w_out, od_qk_gain, od_conv_w, od_rpb):
    b, s, d = x.shape
    n_ctx = ctx.shape[1]
    depth = w_mod.shape[0]
    tm = b * n_ctx
    r = b * s + b * n_ctx
    assert s % tm == 0 and s % GRID_W == 0
    ev_v = ev_w_out.shape[1] - FOURIER_GROUPS * FOURIER_DIM
    dims = dict(b=b, s=s, ctx=n_ctx, r=r, tm=tm, tm_in=min(1024, s), tn=512, tpb=s // tm,
                ev_q=ev_v // DIFF_V * 2 * DIFF_DIM, ev_v=ev_v, fw=FOURIER_GROUPS * FOURIER_DIM,
                cw=od_conv_w.shape[2], nw=od_rpb.shape[1] * NA_DIM,
                tq=min(512, s), tk=min(1024, s))
    n_all, n_lat = r // tm, b * s // tm

    hh = jnp.concatenate([x.reshape(b * s, d), ctx.reshape(b * n_ctx, d)], axis=0)
    crow = jnp.concatenate([c, c_ctx[None, :], jnp.zeros((8 - b - 1, d), F32)], axis=0)
    mod_all = _mod_vectors(crow, w_mod, b_mod)
    rope_tabs = _rope_tables(dims)
    w_ffn_in, w_ffn_out, ev_w_in, ev_w_out, od_w_in, od_w_out = (
        w.astype(BF16) for w in (w_ffn_in, w_ffn_out, ev_w_in, ev_w_out, od_w_in, od_w_out))
    tn = dims["tn"]

    for i in range(depth):
        last = i == depth - 1
        n_out = n_lat if last else n_all
        mods = mod_all[i].reshape(8 * N_MOD, 1, d)
        gain1 = norm_gain[i, 0].reshape(1, d)
        gain2 = norm_gain[i, 1].reshape(1, d)
        j = i // 2
        if i % 2 == 0:
            lam_init = 0.8 - 0.6 * math.exp(-0.3 * i)
            gq = jnp.tile(ev_qk_gain[j, 0], tn // DIFF_DIM).reshape(1, tn)
            gk = jnp.tile(ev_qk_gain[j, 1], tn // DIFF_DIM).reshape(1, tn)
            qn, qr, kr, v, f = _inproj_even(hh, mods, gain1, ev_w_in, j, gq, gk, rope_tabs, dims)
            subln = ev_subln_gain[j].reshape(1, DIFF_V)
            o = _diff_attn_latent(ev_lambda[j], subln, qn, qr, kr, v, dims, lam_init)
            fo = _fourier_latent(f, dims)
            if not last:
                o = _diff_attn_ctx(ev_lambda[j], subln, qn, kr, v, o, dims, lam_init)
                fo = _fourier_ctx(f, fo, dims)
            hh = _outproj(o, fo, ev_w_out, j, hh, mods, n_out, dims)
        else:
            gq = jnp.tile(od_qk_gain[j, 0], tn // NA_DIM).reshape(1, tn)
            gk = jnp.tile(od_qk_gain[j, 1], tn // NA_DIM).reshape(1, tn)
            cg, q, k, v = _inproj_odd(hh, mods, gain1, od_w_in, j, gq, gk, dims)
            cb = _short_conv(cg, od_conv_w[j], dims)
            o = _na_latent(q, k, v, _na_bias(od_rpb[j], s // GRID_W), dims)
            if not last:
                o = _na_ctx(q, k, v, o, dims)
            hh = _outproj(cb, o, od_w_out, j, hh, mods, n_out, dims)
        hh = _ffn(hh, mods, gain2, w_ffn_in, w_ffn_out, i, n_out, dims)

    return hh.reshape(b, s, d)
```

```python
import functools
import math

import jax
import jax.numpy as jnp
import numpy as np
from jax import lax
from jax.experimental import pallas as pl
from jax.experimental.pallas import tpu as pltpu

GRID_W = 64
EPS = 1e-6
N_MOD = 6

DIFF_HEADS = 12
DIFF_DIM = 64
DIFF_V = 2 * DIFF_DIM
FOURIER_GROUPS = 4
FOURIER_DIM = 128
ROPE_BASE = 10000.0

CONV_WIDTH = 1024
CONV_K = 3
NA_HEADS = 8
NA_DIM = 128
NA_WIN_ROWS = 8
NA_WIN_COLS = 16

F32 = jnp.float32
BF16 = jnp.bfloat16
LOG2E = 1.4426950408889634
NEG_BIG = -1e30
LANES = 128
MXU_DIM = 256
VMEM_LIMIT = 56 * 1024 * 1024
ATTN_SCORE_LEAD = 1


def _cparams(*sem):
    return pltpu.CompilerParams(dimension_semantics=sem, vmem_limit_bytes=VMEM_LIMIT)


def _mm(a, b):
    return jnp.dot(a, b, preferred_element_type=F32)


def _mm_nt(a, b):
    return lax.dot_general(a, b, (((1,), (1,)), ((), ())), preferred_element_type=F32)


def _tile_lanes(x, n):
    return x if n == 1 else jnp.concatenate([x] * n, axis=1)


def _mod_kernel(c_ref, w_ref, b_ref, o_ref):
    c = c_ref[...]
    s = (c * jax.nn.sigmoid(c)).astype(BF16)
    o_ref[0] = _mm(s, w_ref[0].astype(BF16)) + b_ref[0]


def _mod_vectors(crow, w_mod, b_mod):
    depth, d, nm = w_mod.shape
    tn = 1024 if nm % 1024 == 0 else 512
    return pl.pallas_call(
        _mod_kernel,
        grid=(depth, nm // tn),
        in_specs=[
            pl.BlockSpec((8, d), lambda l, j: (0, 0)),
            pl.BlockSpec((1, d, tn), lambda l, j: (l, 0, j)),
            pl.BlockSpec((1, 1, tn), lambda l, j: (l, 0, j)),
        ],
        out_specs=pl.BlockSpec((1, 8, tn), lambda l, j: (l, 0, j)),
        out_shape=jax.ShapeDtypeStruct((depth, 8, nm), F32),
        compiler_params=_cparams("parallel", "parallel"),
        name="mod_vectors",
    )(crow, w_mod, b_mod.reshape(depth, 1, nm))


def _norm_mod(x, gain, shift, scale):
    y = x * lax.rsqrt(jnp.mean(x * x, axis=-1, keepdims=True) + EPS) * gain
    return y * (1.0 + scale) + shift


def _group_rms(x, ones_bd, gain, group):
    x2 = (x * x).astype(BF16)
    w = ones_bd.shape[0]
    parts = [_mm(x2[:, c:c + w], ones_bd) for c in range(0, x.shape[1], w)]
    ms = (parts[0] if len(parts) == 1 else jnp.concatenate(parts, axis=1)) * (1.0 / group)
    return x * lax.rsqrt(ms + EPS) * gain


def _rope(x, cos, s_lo, s_hi):
    q = DIFF_DIM // 4
    outs = []
    for c in range(0, x.shape[1], LANES):
        xs = x[:, c:c + LANES]
        outs.append(xs * cos + pltpu.roll(xs, q, 1) * s_lo + pltpu.roll(xs, LANES - q, 1) * s_hi)
    return jnp.concatenate(outs, axis=1)


def _inproj_even_kernel(h_ref, sh_ref, sc_ref, gain_ref, w_ref, gk_ref, ones_ref, cos_ref, slo_ref, shi_ref,
                        q_ref, kr_ref, v_ref, f_ref, a_scr, *, nq, nv):
    j = pl.program_id(1)

    @pl.when(j == 0)
    def _():
        a_scr[...] = _norm_mod(h_ref[...], gain_ref[...], sh_ref[0], sc_ref[0]).astype(BF16)

    acc = _mm(a_scr[...], w_ref[...])

    @pl.when(j < nq)
    def _():
        q_ref[...] = acc.astype(BF16)

    @pl.when(jnp.logical_and(j >= nq, j < 2 * nq))
    def _():
        kn = _group_rms(acc, ones_ref[...], gk_ref[...], DIFF_DIM)
        kr_ref[...] = _rope(kn, cos_ref[...], slo_ref[...], shi_ref[...]).astype(BF16)

    @pl.when(jnp.logical_and(j >= 2 * nq, j < 2 * nq + nv))
    def _():
        v_ref[...] = acc.astype(BF16)

    @pl.when(j >= 2 * nq + nv)
    def _():
        f_ref[...] = acc.astype(BF16)


def _inproj_odd_kernel(h_ref, sh_ref, sc_ref, gain_ref, w_ref, gq_ref, gk_ref, ones_ref,
                       cg_ref, q_ref, k_ref, v_ref, a_scr, *, ncg, nh, qscale):
    j = pl.program_id(1)

    @pl.when(j == 0)
    def _():
        a_scr[...] = _norm_mod(h_ref[...], gain_ref[...], sh_ref[0], sc_ref[0]).astype(BF16)

    acc = _mm(a_scr[...], w_ref[...])

    @pl.when(j < ncg)
    def _():
        cg_ref[...] = acc.astype(BF16)

    @pl.when(jnp.logical_and(j >= ncg, j < ncg + nh))
    def _():
        q_ref[...] = (_group_rms(acc, ones_ref[...], gq_ref[...], NA_DIM) * qscale).astype(BF16)

    @pl.when(jnp.logical_and(j >= ncg + nh, j < ncg + 2 * nh))
    def _():
        k_ref[...] = _group_rms(acc, ones_ref[...], gk_ref[...], NA_DIM).astype(BF16)

    @pl.when(j >= ncg + 2 * nh)
    def _():
        v_ref[...] = acc.astype(BF16)


def _row_specs(tm, d, tiles_per_batch, n_batch):
    def mod_spec(k):
        return pl.BlockSpec((1, 1, d), lambda i, j: (jnp.minimum(i // tiles_per_batch, n_batch) * N_MOD + k, 0, 0))

    return pl.BlockSpec((tm, d), lambda i, j: (i, 0)), mod_spec


def _clamped(tm, tn, lo, n):
    return pl.BlockSpec((tm, tn), lambda i, j: (i, jnp.clip(j - lo, 0, n - 1)))


def _ones_blockdiag(group, size=MXU_DIM):
    idx = np.arange(size) // group
    return jnp.asarray((idx[:, None] == idx[None, :]).astype(np.float32), dtype=BF16)


def _inproj_even(hh, mods, gain, w, layer, gk, rope_tabs, dims):
    r, d = hh.shape
    tm, tn = dims["tm_in"], dims["tn"]
    ev_q, ev_v, fw = dims["ev_q"], dims["ev_v"], dims["fw"]
    nq, nv, nf = ev_q // tn, ev_v // tn, fw // tn
    nj = 2 * nq + nv + nf
    hspec, mod_spec = _row_specs(tm, d, dims["s"] // tm, dims["b"])
    const = lambda shape: pl.BlockSpec(shape, lambda i, j: (0,) * len(shape))
    tab = pl.BlockSpec((tm, LANES), lambda i, j: (i, 0))
    kern = functools.partial(_inproj_even_kernel, nq=nq, nv=nv)
    return pl.pallas_call(
        kern,
        grid=(pl.cdiv(r, tm), nj),
        in_specs=[hspec, mod_spec(0), mod_spec(1), const((1, d)),
                  pl.BlockSpec((None, d, tn), lambda i, j: (layer, 0, j)),
                  const((1, tn)), const((MXU_DIM, MXU_DIM)), tab, tab, tab],
        out_specs=[_clamped(tm, tn, 0, nq), _clamped(tm, tn, nq, nq),
                   _clamped(tm, tn, 2 * nq, nv), _clamped(tm, tn, 2 * nq + nv, nf)],
        out_shape=[jax.ShapeDtypeStruct((r, ev_q), BF16)] * 2
        + [jax.ShapeDtypeStruct((r, ev_v), BF16), jax.ShapeDtypeStruct((r, fw), BF16)],
        scratch_shapes=[pltpu.VMEM((tm, d), BF16)],
        compiler_params=_cparams("parallel", "arbitrary"),
        name="inproj_even",
    )(hh, mods, mods, gain, w, gk, _ones_blockdiag(DIFF_DIM), *rope_tabs)


def _inproj_odd(hh, mods, gain, w, layer, gq, gk, dims):
    r, d = hh.shape
    tm, tn = dims["tm_in"], dims["tn"]
    cw, nw = dims["cw"], dims["nw"]
    ncg, nh = 3 * cw // tn, nw // tn
    nj = ncg + 3 * nh
    hspec, mod_spec = _row_specs(tm, d, dims["s"] // tm, dims["b"])
    const = lambda shape: pl.BlockSpec(shape, lambda i, j: (0,) * len(shape))
    kern = functools.partial(_inproj_odd_kernel, ncg=ncg, nh=nh, qscale=NA_DIM ** -0.5 * LOG2E)
    return pl.pallas_call(
        kern,
        grid=(pl.cdiv(r, tm), nj),
        in_specs=[hspec, mod_spec(0), mod_spec(1), const((1, d)),
                  pl.BlockSpec((None, d, tn), lambda i, j: (layer, 0, j)),
                  const((1, tn)), const((1, tn)), const((MXU_DIM, MXU_DIM))],
        out_specs=[_clamped(tm, tn, 0, ncg), _clamped(tm, tn, ncg, nh), _clamped(tm, tn, ncg + nh, nh),
                   _clamped(tm, tn, ncg + 2 * nh, nh)],
        out_shape=[jax.ShapeDtypeStruct((r, 3 * cw), BF16)] + [jax.ShapeDtypeStruct((r, nw), BF16)] * 3,
        scratch_shapes=[pltpu.VMEM((tm, d), BF16)],
        compiler_params=_cparams("parallel", "arbitrary"),
        name="inproj_odd",
    )(hh, mods, mods, gain, w, gq, gk, _ones_blockdiag(NA_DIM))


def _split_components(q):
    lane = lax.broadcasted_iota(jnp.int32, q.shape, 1)
    zero = jnp.zeros_like(q)
    return jnp.concatenate([jnp.where(lane < DIFF_DIM, q, zero), jnp.where(lane >= DIFF_DIM, q, zero)], axis=0)


def _online_softmax(chunks):
    lead = ATTN_SCORE_LEAD
    rows = chunks[0][0].shape[0]
    scores = [_mm_nt(q2, kk) for q2, kk, _ in chunks[:lead]]
    m = jnp.full((rows, LANES), NEG_BIG, F32)
    acc = jnp.zeros((rows, 2 * DIFF_V), F32)
    for c, (_, _, vv) in enumerate(chunks):
        if c + lead < len(chunks):
            scores.append(_mm_nt(chunks[c + lead][0], chunks[c + lead][1]))
        s = scores[c]
        m_new = jnp.maximum(m, jnp.max(s, axis=-1, keepdims=True))
        alpha = jnp.exp2(m - m_new)
        p = jnp.exp2((s - _tile_lanes(m_new, s.shape[1] // LANES)).astype(BF16))
        v_ext = jnp.concatenate([vv, jnp.ones_like(vv)], axis=1)
        acc = _tile_lanes(alpha, 2) * acc + _mm(p, v_ext)
        m = m_new
    return acc


def _diff_attn_kernel(*refs, n_chunks, tk, lam_init):
    if n_chunks:
        (lam_ref, g_ref, gq_ref, ones_ref, q_ref, cos_ref, slo_ref, shi_ref,
         kc_ref, vc_ref, k_ref, v_ref, o_ref) = refs
    else:
        lam_ref, g_ref, gq_ref, ones_ref, q_ref, kc_ref, vc_ref, o_ref = refs
    tq = q_ref.shape[0]
    qn = _group_rms(q_ref[...].astype(F32), ones_ref[...], gq_ref[...], DIFF_DIM) * (DIFF_DIM ** -0.5 * LOG2E)
    chunks = []
    if n_chunks:
        q2 = _split_components(_rope(qn, cos_ref[...], slo_ref[...], shi_ref[...]).astype(BF16))
        chunks = [(q2, k_ref[c * tk:(c + 1) * tk, :], v_ref[c * tk:(c + 1) * tk, :]) for c in range(n_chunks)]
    chunks.append((_split_components(qn.astype(BF16)), kc_ref[...], vc_ref[...]))
    acc = _online_softmax(chunks)

    lv = lam_ref[...]
    lam = (jnp.exp(jnp.sum(lv[0:1] * lv[1:2], axis=-1, keepdims=True))
           - jnp.exp(jnp.sum(lv[2:3] * lv[3:4], axis=-1, keepdims=True)) + lam_init)
    o2 = acc[:, :DIFF_V] / acc[:, DIFF_V:]
    o = o2[:tq] - lam * o2[tq:]
    o = o * lax.rsqrt(jnp.mean(o * o, axis=-1, keepdims=True) + EPS) * g_ref[...] * (1.0 - lam_init)
    o_ref[...] = o.astype(BF16)


def _diff_attn_latent(lam_vec, subln, gq, q, rope_tabs, kr, v, dims, lam_init):
    b, s, ctx, r = dims["b"], dims["s"], dims["ctx"], dims["r"]
    tq, tk = dims["tq"], dims["tk"]
    nh = q.shape[1] // DIFF_V
    ctx_blk0 = b * s // ctx
    kern = functools.partial(_diff_attn_kernel, n_chunks=s // tk, tk=tk, lam_init=lam_init)
    const = lambda shape: pl.BlockSpec(shape, lambda bi, h, qi: (0, 0))
    tab = pl.BlockSpec((tq, LANES), lambda bi, h, qi: (bi * (s // tq) + qi, 0))
    qspec = pl.BlockSpec((tq, DIFF_V), lambda bi, h, qi: (bi * (s // tq) + qi, h))
    cspec = pl.BlockSpec((ctx, DIFF_V), lambda bi, h, qi: (ctx_blk0 + bi, h))
    kspec = pl.BlockSpec((s, DIFF_V), lambda bi, h, qi: (bi, h))
    return pl.pallas_call(
        kern,
        grid=(b, nh, s // tq),
        in_specs=[const((4, DIFF_DIM)), const((1, DIFF_V)), const((1, DIFF_V)), const((DIFF_V, DIFF_V)),
                  qspec, tab, tab, tab, cspec, cspec, kspec, kspec],
        out_specs=qspec,
        out_shape=jax.ShapeDtypeStruct((r, nh * DIFF_V), BF16),
        compiler_params=_cparams("parallel", "parallel", "arbitrary"),
        name="diff_attn_latent",
    )(lam_vec, subln, gq, _ones_blockdiag(DIFF_DIM, DIFF_V), q, *rope_tabs, kr, v, kr, v)


def _diff_attn_ctx(lam_vec, subln, gq, q, kr, v, o_lat, dims, lam_init):
    b, s, ctx = dims["b"], dims["s"], dims["ctx"]
    nh = q.shape[1] // DIFF_V
    ctx_blk0 = b * s // ctx
    kern = functools.partial(_diff_attn_kernel, n_chunks=0, tk=0, lam_init=lam_init)
    const = lambda shape: pl.BlockSpec(shape, lambda bi, h: (0, 0))
    cspec = pl.BlockSpec((ctx, DIFF_V), lambda bi, h: (ctx_blk0 + bi, h))

    def wrapped(lam_ref, g_ref, gq_ref, ones_ref, q_ref, kc_ref, vc_ref, alias_ref, o_ref):
        del alias_ref
        kern(lam_ref, g_ref, gq_ref, ones_ref, q_ref, kc_ref, vc_ref, o_ref)

    return pl.pallas_call(
        wrapped,
        grid=(b, nh),
        in_specs=[const((4, DIFF_DIM)), const((1, DIFF_V)), const((1, DIFF_V)), const((DIFF_V, DIFF_V)),
                  cspec, cspec, cspec, pl.BlockSpec(memory_space=pl.ANY)],
        out_specs=cspec,
        out_shape=jax.ShapeDtypeStruct(o_lat.shape, BF16),
        input_output_aliases={7: 0},
        compiler_params=_cparams("parallel", "parallel"),
        name="diff_attn_ctx",
    )(lam_vec, subln, gq, _ones_blockdiag(DIFF_DIM, DIFF_V), q, kr, v, o_lat)


def _dft_cs(n):
    k = np.arange(n)
    ang = 2.0 * np.pi * ((k[:, None] * k[None, :]) % n) / n
    return np.cos(ang), np.sin(ang)


def _channel_dft(groups, scale):
    c, s = _dft_cs(FOURIER_DIM)
    eye = np.eye(groups)
    return np.kron(eye, c) * scale, np.kron(eye, s) * scale


def _fourier_rows_kernel(x_ref, w_ref, tc_ref, ts_ref, y_ref, *, cg, fw, nr):
    for c in range(cg):
        y = _mm(w_ref[...], x_ref[:, c * fw:(c + 1) * fw])
        yr, yi = y[:nr], y[nr:]
        tc = _tile_lanes(tc_ref[c], fw // LANES)
        ts = _tile_lanes(ts_ref[c], fw // LANES)
        y_ref[0, 0, :, c * fw:(c + 1) * fw] = (yr * tc + yi * ts).astype(BF16)
        y_ref[0, 1, :, c * fw:(c + 1) * fw] = (yi * tc - yr * ts).astype(BF16)


def _fourier_cols_kernel(y_ref, w2_ref, wc_ref, ws_ref, o_ref, *, ag, fw, nc):
    for a in range(ag):
        yy = jnp.concatenate([y_ref[0, 0, a], y_ref[0, 1, a]], axis=0)
        z = _mm(w2_ref[...], yy)
        out = _mm(z[:nc].astype(BF16), wc_ref[...]) + _mm(z[nc:].astype(BF16), ws_ref[...])
        o_ref[:, a * fw:(a + 1) * fw] = out.astype(BF16)


def _fourier_dense_kernel(x_ref, wc_ref, ws_ref, cn_ref, sn_ref, alias_ref, o_ref):
    del alias_ref
    x = x_ref[...]
    gc = _mm(x, wc_ref[...]).astype(BF16)
    gs = _mm(x, ws_ref[...]).astype(BF16)
    o_ref[...] = (_mm(cn_ref[...], gc) - _mm(sn_ref[...], gs)).astype(BF16)


def _fourier_latent(f, dims):
    b, s, r = dims["b"], dims["s"], dims["r"]
    fw = f.shape[1]
    groups = fw // FOURIER_DIM
    nc = GRID_W
    nr = s // nc
    cg = 4
    ag = 8
    c1, s1 = _dft_cs(nr)
    w1 = jnp.asarray(np.concatenate([c1, -s1], axis=0), dtype=BF16)
    ang = 2.0 * np.pi * (np.arange(nc)[:, None] * np.arange(nr)[None, :]) / s
    tcos = jnp.asarray(np.repeat(np.cos(ang)[:, :, None], LANES, axis=2), dtype=F32)
    tsin = jnp.asarray(np.repeat(np.sin(ang)[:, :, None], LANES, axis=2), dtype=F32)
    x2d = f.reshape(r // nc, nc * fw)
    y = pl.pallas_call(
        functools.partial(_fourier_rows_kernel, cg=cg, fw=fw, nr=nr),
        grid=(b, nc // cg),
        in_specs=[pl.BlockSpec((nr, cg * fw), lambda bi, j: (bi, j)),
                  pl.BlockSpec((2 * nr, nr), lambda bi, j: (0, 0)),
                  pl.BlockSpec((cg, nr, LANES), lambda bi, j: (j, 0, 0)),
                  pl.BlockSpec((cg, nr, LANES), lambda bi, j: (j, 0, 0))],
        out_specs=pl.BlockSpec((1, 2, nr, cg * fw), lambda bi, j: (bi, 0, 0, j)),
        out_shape=jax.ShapeDtypeStruct((b, 2, nr, nc * fw), BF16),
        compiler_params=_cparams("parallel", "parallel"),
        name="fourier_rows",
    )(x2d, w1, tcos, tsin)
    c2, s2 = _dft_cs(nc)
    w2 = jnp.asarray(np.block([[c2, s2], [-s2, c2]]), dtype=BF16)
    wc, ws = _channel_dft(groups, 1.0 / math.sqrt(s * FOURIER_DIM))
    out = pl.pallas_call(
        functools.partial(_fourier_cols_kernel, ag=ag, fw=fw, nc=nc),
        grid=(b, nr // ag),
        in_specs=[pl.BlockSpec((1, 2, ag, nc, fw), lambda bi, j: (bi, 0, j, 0, 0)),
                  pl.BlockSpec((2 * nc, 2 * nc), lambda bi, j: (0, 0)),
                  pl.BlockSpec((fw, fw), lambda bi, j: (0, 0)),
                  pl.BlockSpec((fw, fw), lambda bi, j: (0, 0))],
        out_specs=pl.BlockSpec((nc, ag * fw), lambda bi, j: (bi, j)),
        out_shape=jax.ShapeDtypeStruct((r // nr, nr * fw), BF16),
        compiler_params=_cparams("parallel", "parallel"),
        name="fourier_cols",
    )(y.reshape(b, 2, nr, nc, fw), w2, jnp.asarray(wc, dtype=BF16), jnp.asarray(ws, dtype=BF16))
    return out.reshape(r, fw)


def _fourier_ctx(f, fo, dims):
    b, s, ctx = dims["b"], dims["s"], dims["ctx"]
    fw = f.shape[1]
    groups = fw // FOURIER_DIM
    wc, ws = _channel_dft(groups, 1.0 / math.sqrt(ctx * FOURIER_DIM))
    cn, sn = _dft_cs(ctx)
    blk0 = b * s // ctx
    rows = pl.BlockSpec((ctx, fw), lambda bi: (blk0 + bi, 0))
    const = lambda shape: pl.BlockSpec(shape, lambda bi: (0, 0))
    return pl.pallas_call(
        _fourier_dense_kernel,
        grid=(b,),
        in_specs=[rows, const((fw, fw)), const((fw, fw)), const((ctx, ctx)), const((ctx, ctx)),
                  pl.BlockSpec(memory_space=pl.ANY)],
        out_specs=rows,
        out_shape=jax.ShapeDtypeStruct(fo.shape, BF16),
        input_output_aliases={5: 0},
        compiler_params=_cparams("parallel"),
        name="fourier_ctx",
    )(f, jnp.asarray(wc, dtype=BF16), jnp.asarray(ws, dtype=BF16),
      jnp.asarray(cn, dtype=BF16), jnp.asarray(sn, dtype=BF16), fo)


def _gated_conv(gb_ref, gc_ref, hh_ref, gcp_ref, hhp_ref, gcn_ref, hhn_ref, w_ref, *, n_lat_tiles, s, ctx):
    i = pl.program_id(0)
    tm = gb_ref.shape[0]
    is_lat = i < n_lat_tiles
    seq = jnp.where(is_lat, s, ctx)
    off = lax.rem(i * tm, seq)
    u = gc_ref[...].astype(F32) * hh_ref[...].astype(F32)
    halo = gcp_ref.shape[0]
    u_prev = gcp_ref[halo - 1:halo, :].astype(F32) * hhp_ref[halo - 1:halo, :].astype(F32)
    u_next = gcn_ref[0:1, :].astype(F32) * hhn_ref[0:1, :].astype(F32)
    u_prev = jnp.where(off == 0, 0.0, u_prev)
    u_next = jnp.where(lax.rem(off + tm, seq) == 0, 0.0, u_next)
    row = lax.broadcasted_iota(jnp.int32, (tm, 1), 0)
    dn = jnp.where(row == 0, u_prev, pltpu.roll(u, 1, 0))
    up = jnp.where(row == tm - 1, u_next, pltpu.roll(u, tm - 1, 0))
    if ctx < tm:
        inner_start = functools.reduce(jnp.logical_or, [row == k * ctx for k in range(1, tm // ctx)])
        inner_end = functools.reduce(jnp.logical_or, [row == k * ctx - 1 for k in range(1, tm // ctx)])
        is_ctx = jnp.logical_not(is_lat)
        dn = jnp.where(jnp.logical_and(is_ctx, inner_start), 0.0, dn)
        up = jnp.where(jnp.logical_and(is_ctx, inner_end), 0.0, up)
    w = w_ref[...]
    y = dn * w[0:1] + u * w[1:2] + up * w[2:3]
    return gb_ref[...].astype(F32) * y


def _conv_specs(cw, tm, r):
    halo = 16
    nhb = tm // halo
    last = r // halo - 1
    cur = lambda off: pl.BlockSpec((tm, cw), lambda i: (i, off))
    prev = lambda off: pl.BlockSpec((halo, cw), lambda i: (jnp.maximum(i * nhb - 1, 0), off))
    nxt = lambda off: pl.BlockSpec((halo, cw), lambda i: (jnp.minimum((i + 1) * nhb, last), off))
    return [cur(0), cur(1), cur(2), prev(1), prev(2), nxt(1), nxt(2), pl.BlockSpec((CONV_K, cw), lambda i: (0, 0))]


NA_BLOCK_ROWS = 8
NA_KEY_ROWS = 16
NA_BLOCKS_PER_TRIP = 4


def _na_plan(rows):
    wr = min(NA_WIN_ROWS, rows)
    starts, variants, keys = [], [], {}
    for jb in range(rows // NA_BLOCK_ROWS):
        r0 = jb * NA_BLOCK_ROWS
        ks = int(np.clip(r0 - wr // 2, 0, rows - NA_KEY_ROWS))
        rs = [int(np.clip(r0 + t - wr // 2, 0, rows - wr)) for t in range(NA_BLOCK_ROWS)]
        key = (r0 - ks, tuple(x - ks for x in rs))
        variants.append(keys.setdefault(key, len(keys)))
        starts.append(ks)
    return starts, variants, list(keys), wr


def _na_bias(rpb, rows):
    _, _, keys, wr = _na_plan(rows)
    col = np.arange(GRID_W)
    cs = np.clip(col - NA_WIN_COLS // 2, 0, GRID_W - NA_WIN_COLS)
    kc = np.arange(GRID_W)
    dc = kc[None, :] - col[:, None] + (NA_WIN_COLS - 1)
    col_ok = (kc[None, :] >= cs[:, None]) & (kc[None, :] < cs[:, None] + NA_WIN_COLS)
    n_dr, n_dc = 2 * NA_WIN_ROWS - 1, 2 * NA_WIN_COLS - 1
    col_sel = ((dc[None] == np.arange(n_dc)[:, None, None]) & col_ok[None]).astype(np.float32)
    t = np.arange(NA_BLOCK_ROWS)
    i = np.arange(NA_KEY_ROWS)
    row_sel, row_okv = [], []
    for r0_rel, rs_rel in keys:
        dr = i[None, :] - (r0_rel + t[:, None]) + (NA_WIN_ROWS - 1)
        rs_arr = np.asarray(rs_rel)
        row_ok = (i[None, :] >= rs_arr[:, None]) & (i[None, :] < rs_arr[:, None] + wr)
        row_sel.append(((dr[None] == np.arange(n_dr)[:, None, None]) & row_ok[None]).astype(np.float32))
        row_okv.append(row_ok)
    row_sel = np.stack(row_sel)
    ok = (jnp.asarray(np.stack(row_okv))[:, None, :, None, :, None]
          & jnp.asarray(col_ok)[None, None, None, :, None, :])
    hi = lax.Precision.HIGHEST
    col_exp = jnp.einsum("hrd,dck->hrck", rpb.astype(F32) * LOG2E, jnp.asarray(col_sel), precision=hi)
    vals = jnp.einsum("vrti,hrck->vhtcik", jnp.asarray(row_sel), col_exp, precision=hi)
    vals = jnp.where(ok, vals, NEG_BIG)
    return vals.reshape(len(keys), rpb.shape[0], NA_BLOCK_ROWS * GRID_W, NA_KEY_ROWS * GRID_W)


def _na_kernel(q_ref, k_ref, v_ref, kc_ref, vc_ref, bias_ref, o_ref, *, starts, variants):
    qb = NA_BLOCK_ROWS * GRID_W
    kb = NA_KEY_ROWS * GRID_W

    def lookup(table, jb):
        out = jnp.int32(table[0])
        for idx in range(1, len(table)):
            out = jnp.where(jb == idx, jnp.int32(table[idx]), out)
        return out

    def block(jb):
        q0 = pl.multiple_of(jb * qb, qb)
        k0 = pl.multiple_of(lookup(starts, jb) * GRID_W, GRID_W)
        q = q_ref[pl.ds(q0, qb), :]
        s_nb = _mm_nt(q, k_ref[pl.ds(k0, kb), :]) + bias_ref[lookup(variants, jb), 0]
        s_cx = _mm_nt(q, kc_ref[...])
        m = jnp.maximum(jnp.max(s_nb, axis=-1, keepdims=True), jnp.max(s_cx, axis=-1, keepdims=True))
        p_nb = jnp.exp2(s_nb - m)
        p_cx = jnp.exp2(s_cx - m)
        l = jnp.sum(p_nb, axis=-1, keepdims=True) + jnp.sum(p_cx, axis=-1, keepdims=True)
        o = _mm(p_nb.astype(BF16), v_ref[pl.ds(k0, kb), :]) + _mm(p_cx.astype(BF16), vc_ref[...])
        o_ref[pl.ds(q0, qb), :] = (o / l).astype(BF16)

    per_trip = math.gcd(NA_BLOCKS_PER_TRIP, len(starts))

    def body(t, carry):
        for u in range(per_trip):
            block(t * per_trip + u)
        return carry

    lax.fori_loop(0, len(starts) // per_trip, body, 0)


def _na_ctx_kernel(q_ref, kc_ref, vc_ref, alias_ref, o_ref):
    del alias_ref
    s = _mm_nt(q_ref[...], kc_ref[...])
    p = jnp.exp2(s - jnp.max(s, axis=-1, keepdims=True))
    o = _mm(p.astype(BF16), vc_ref[...]) / jnp.sum(p, axis=-1, keepdims=True)
    o_ref[...] = o.astype(BF16)


def _na_latent(q, k, v, bias, dims):
    b, s, ctx, r = dims["b"], dims["s"], dims["ctx"], dims["r"]
    nh = q.shape[1] // NA_DIM
    starts, variants, keys, _ = _na_plan(s // GRID_W)
    blk0 = b * s // ctx
    lat = pl.BlockSpec((s, NA_DIM), lambda bi, h: (bi, h))
    cx = pl.BlockSpec((ctx, NA_DIM), lambda bi, h: (blk0 + bi, h))
    nvar = len(keys)
    return pl.pallas_call(
        functools.partial(_na_kernel, starts=tuple(starts), variants=tuple(variants)),
        grid=(b, nh),
        in_specs=[lat, lat, lat, cx, cx,
                  pl.BlockSpec((nvar, 1) + bias.shape[2:], lambda bi, h: (0, h, 0, 0))],
        out_specs=lat,
        out_shape=jax.ShapeDtypeStruct((r, nh * NA_DIM), BF16),
        compiler_params=_cparams("parallel", "arbitrary"),
        name="na_latent",
    )(q, k, v, k, v, bias)


def _na_ctx(q, k, v, o_lat, dims):
    b, s, ctx = dims["b"], dims["s"], dims["ctx"]
    nh = q.shape[1] // NA_DIM
    blk0 = b * s // ctx
    cx = pl.BlockSpec((ctx, NA_DIM), lambda bi, h: (blk0 + bi, h))
    return pl.pallas_call(
        _na_ctx_kernel,
        grid=(b, nh),
        in_specs=[cx, cx, cx, pl.BlockSpec(memory_space=pl.ANY)],
        out_specs=cx,
        out_shape=jax.ShapeDtypeStruct(o_lat.shape, BF16),
        input_output_aliases={3: 0},
        compiler_params=_cparams("parallel", "parallel"),
        name="na_ctx",
    )(q, k, v, o_lat)


def _outproj_kernel(xa_ref, xb_ref, wa_ref, wb_ref, h_ref, gate_ref, o_ref):
    y = _mm(xa_ref[...], wa_ref[...]) + _mm(xb_ref[...], wb_ref[...])
    o_ref[...] = h_ref[...] + gate_ref[0] * y


def _outproj_conv_kernel(*refs, conv_args):
    conv_refs, (xb_ref, wa_ref, wb_ref, h_ref, gate_ref, o_ref) = refs[:8], refs[8:]
    xa = _gated_conv(*conv_refs, **conv_args).astype(BF16)
    y = _mm(xa, wa_ref[...]) + _mm(xb_ref[...], wb_ref[...])
    o_ref[...] = h_ref[...] + gate_ref[0] * y


def _outproj(xa, xb, w, layer, hh, mods, n_tiles, dims, conv_w=None):
    r, d = hh.shape
    tm = dims["tm"]
    kb = xb.shape[1]
    ka = w.shape[1] - kb
    assert ka % kb == 0
    hspec, mod_spec = _row_specs(tm, d, dims["tpb"], dims["b"])
    one = lambda spec: pl.BlockSpec(spec.block_shape, lambda i: spec.index_map(i, 0))
    if conv_w is None:
        kern, xa_specs, xa_args = _outproj_kernel, [pl.BlockSpec((tm, ka), lambda i: (i, 0))], [xa]
    else:
        conv_args = dict(n_lat_tiles=dims["b"] * dims["tpb"], s=dims["s"], ctx=dims["ctx"])
        kern = functools.partial(_outproj_conv_kernel, conv_args=conv_args)
        xa_specs, xa_args = _conv_specs(ka, tm, r), [xa] * 7 + [conv_w]
    return pl.pallas_call(
        kern,
        grid=(n_tiles,),
        in_specs=xa_specs + [pl.BlockSpec((tm, kb), lambda i: (i, 0)),
                             pl.BlockSpec((None, ka, d), lambda i: (layer, 0, 0)),
                             pl.BlockSpec((None, kb, d), lambda i: (layer, ka // kb, 0)),
                             one(hspec), one(mod_spec(2))],
        out_specs=one(hspec),
        out_shape=jax.ShapeDtypeStruct((r, d), F32),
        input_output_aliases={len(xa_specs) + 3: 0},
        compiler_params=_cparams("parallel"),
        name="outproj",
    )(*xa_args, xb, w, w, hh, mods)


def _ffn_kernel(h_ref, hn_ref, sh_ref, sc_ref, shn_ref, scn_ref, gate_ref, gain_ref, wg_ref, wv_ref, wo_ref,
                o_ref, u_scr, un_scr, acc_scr, *, slice_rows):
    i, j = pl.program_id(0), pl.program_id(1)

    @pl.when(jnp.logical_and(i == 0, j == 0))
    def _():
        u_scr[...] = _norm_mod(h_ref[...], gain_ref[...], sh_ref[0], sc_ref[0]).astype(BF16)

    @pl.when(jnp.logical_and(i > 0, j == 0))
    def _():
        u_scr[...] = un_scr[...]

    @pl.when(j == 0)
    def _():
        acc_scr[...] = jnp.zeros(acc_scr.shape, F32)

    n_slices = h_ref.shape[0] // slice_rows
    r0 = pl.multiple_of(jnp.minimum(j, n_slices - 1) * slice_rows, slice_rows)
    nxt = _norm_mod(hn_ref[pl.ds(r0, slice_rows), :], gain_ref[...], shn_ref[0], scn_ref[0])
    un_scr[pl.ds(r0, slice_rows), :] = nxt.astype(BF16)

    u = u_scr[...]
    g = _mm(u, wg_ref[...])
    val = _mm(u, wv_ref[...])
    act = (g * jax.nn.sigmoid(g) * val).astype(BF16)
    acc_scr[...] += _mm(act, wo_ref[...])

    @pl.when(j == pl.num_programs(1) - 1)
    def _():
        o_ref[...] = h_ref[...] + gate_ref[0] * acc_scr[...]


def _ffn(hh, mods, gain, w_in, w_out, layer, n_tiles, dims):
    r, d = hh.shape
    tm = dims["tm"]
    hid = w_out.shape[1]
    th = 512 if hid % 512 == 0 else 256
    nj = hid // th
    n_slices = max(n for n in range(1, nj + 1) if tm % (n * 16) == 0)
    tpb, nb = dims["tpb"], dims["b"]
    nxt = lambda i: jnp.minimum(i + 1, n_tiles - 1)

    def mod_spec(k, tile):
        return pl.BlockSpec((1, 1, d), lambda i, j: (jnp.minimum(tile(i) // tpb, nb) * N_MOD + k, 0, 0))

    cur = lambda i: i
    hspec = pl.BlockSpec((tm, d), lambda i, j: (i, 0))
    in_place = n_tiles * tm == r
    return pl.pallas_call(
        functools.partial(_ffn_kernel, slice_rows=tm // n_slices),
        grid=(n_tiles, nj),
        in_specs=[hspec, pl.BlockSpec((tm, d), lambda i, j: (nxt(i), 0)),
                  mod_spec(3, cur), mod_spec(4, cur), mod_spec(3, nxt), mod_spec(4, nxt), mod_spec(5, cur),
                  pl.BlockSpec((1, d), lambda i, j: (0, 0)),
                  pl.BlockSpec((None, d, th), lambda i, j: (layer, 0, j)),
                  pl.BlockSpec((None, d, th), lambda i, j: (layer, 0, nj + j)),
                  pl.BlockSpec((None, th, d), lambda i, j: (layer, j, 0))],
        out_specs=hspec,
        out_shape=jax.ShapeDtypeStruct((n_tiles * tm, d), F32),
        input_output_aliases={0: 0} if in_place else {},
        scratch_shapes=[pltpu.VMEM((tm, d), BF16), pltpu.VMEM((tm, d), BF16), pltpu.VMEM((tm, d), F32)],
        compiler_params=_cparams("arbitrary", "arbitrary"),
        name="ffn",
    )(hh, hh, mods, mods, mods, mods, mods, gain, w_in, w_in, w_out)


def _rope_tables(dims):
    s, b, ctx = dims["s"], dims["b"], dims["ctx"]
    t = np.arange(s)
    n_freq = DIFF_DIM // 4
    inv = (ROPE_BASE ** (-np.arange(n_freq, dtype=np.float32) / n_freq)).astype(np.float32)
    lane = np.arange(LANES) % DIFF_DIM
    chunk = lane // n_freq
    pos = np.where((chunk // 2)[None, :] == 0, (t // GRID_W)[:, None], (t % GRID_W)[:, None]).astype(np.float32)
    ang = pos * inv[lane % n_freq][None, :]
    cos, sin = np.cos(ang), np.sin(ang)
    s_lo = np.where((chunk % 2)[None, :] == 1, sin, 0.0)
    s_hi = np.where((chunk % 2)[None, :] == 0, -sin, 0.0)
    pad = b * ctx

    def full(tab, fill):
        return jnp.asarray(np.concatenate([np.tile(tab, (b, 1)), np.full((pad, LANES), fill)], axis=0), dtype=F32)

    return full(cos, 1.0), full(s_lo, 0.0), full(s_hi, 0.0)


def kernel(x, c, ctx, c_ctx, w_mod, b_mod, norm_gain, w_ffn_in, w_ffn_out, ev_w_in, ev_w_out, ev_qk_gain,
           ev_lambda, ev_subln_gain, od_w_in, od_w_out, od_qk_gain, od_conv_w, od_rpb):
    b, s, d = x.shape
    n_ctx = ctx.shape[1]
    depth = w_mod.shape[0]
    tm = b * n_ctx
    r = b * s + b * n_ctx
    assert s % tm == 0 and s % GRID_W == 0
    ev_v = ev_w_out.shape[1] - FOURIER_GROUPS * FOURIER_DIM
    dims = dict(b=b, s=s, ctx=n_ctx, r=r, tm=tm, tm_in=min(1024, s), tn=512, tpb=s // tm,
                ev_q=ev_v // DIFF_V * 2 * DIFF_DIM, ev_v=ev_v, fw=FOURIER_GROUPS * FOURIER_DIM,
                cw=od_conv_w.shape[2], nw=od_rpb.shape[1] * NA_DIM,
                tq=min(512, s), tk=min(1024, s))
    n_all, n_lat = r // tm, b * s // tm

    hh = jnp.concatenate([x.reshape(b * s, d), ctx.reshape(b * n_ctx, d)], axis=0)
    crow = jnp.concatenate([c, c_ctx[None, :], jnp.zeros((8 - b - 1, d), F32)], axis=0)
    mod_all = _mod_vectors(crow, w_mod, b_mod)
    rope_tabs = _rope_tables(dims)
    w_ffn_in, w_ffn_out, ev_w_in, ev_w_out, od_w_in, od_w_out = (
        w.astype(BF16) for w in (w_ffn_in, w_ffn_out, ev_w_in, ev_w_out, od_w_in, od_w_out))
    tn = dims["tn"]

    for i in range(depth):
        last = i == depth - 1
        n_out = n_lat if last else n_all
        mods = mod_all[i].reshape(8 * N_MOD, 1, d)
        gain1 = norm_gain[i, 0].reshape(1, d)
        gain2 = norm_gain[i, 1].reshape(1, d)
        j = i // 2
        if i % 2 == 0:
            lam_init = 0.8 - 0.6 * math.exp(-0.3 * i)
            gq = jnp.tile(ev_qk_gain[j, 0], DIFF_V // DIFF_DIM).reshape(1, DIFF_V)
            gk = jnp.tile(ev_qk_gain[j, 1], tn // DIFF_DIM).reshape(1, tn)
            q, kr, v, f = _inproj_even(hh, mods, gain1, ev_w_in, j, gk, rope_tabs, dims)
            subln = ev_subln_gain[j].reshape(1, DIFF_V)
            o = _diff_attn_latent(ev_lambda[j], subln, gq, q, rope_tabs, kr, v, dims, lam_init)
            fo = _fourier_latent(f, dims)
            if not last:
                o = _diff_attn_ctx(ev_lambda[j], subln, gq, q, kr, v, o, dims, lam_init)
                fo = _fourier_ctx(f, fo, dims)
            hh = _outproj(o, fo, ev_w_out, j, hh, mods, n_out, dims)
        else:
            gq = jnp.tile(od_qk_gain[j, 0], tn // NA_DIM).reshape(1, tn)
            gk = jnp.tile(od_qk_gain[j, 1], tn // NA_DIM).reshape(1, tn)
            cg, q, k, v = _inproj_odd(hh, mods, gain1, od_w_in, j, gq, gk, dims)
            o = _na_latent(q, k, v, _na_bias(od_rpb[j], s // GRID_W), dims)
            if not last:
                o = _na_ctx(q, k, v, o, dims)
            hh = _outproj(cg, o, od_w_out, j, hh, mods, n_out, dims, conv_w=od_conv_w[j])
        hh = _ffn(hh, mods, gain2, w_ffn_in, w_ffn_out, i, n_out, dims)

    return hh.reshape(b, s, d)
```

```python
import functools
import math

import jax
import jax.numpy as jnp
import numpy as np
from jax import lax
from jax.experimental import pallas as pl
from jax.experimental.pallas import tpu as pltpu

GRID_W = 64
EPS = 1e-6
N_MOD = 6

DIFF_HEADS = 12
DIFF_DIM = 64
DIFF_V = 2 * DIFF_DIM
FOURIER_GROUPS = 4
FOURIER_DIM = 128
ROPE_BASE = 10000.0

CONV_WIDTH = 1024
CONV_K = 3
NA_HEADS = 8
NA_DIM = 128
NA_WIN_ROWS = 8
NA_WIN_COLS = 16

F32 = jnp.float32
BF16 = jnp.bfloat16
LOG2E = 1.4426950408889634
NEG_BIG = -1e30
LANES = 128
MXU_DIM = 256
VMEM_LIMIT = 56 * 1024 * 1024
ATTN_SCORE_LEAD = 1


def _cparams(*sem):
    return pltpu.CompilerParams(dimension_semantics=sem, vmem_limit_bytes=VMEM_LIMIT)


def _mm(a, b):
    return jnp.dot(a, b, preferred_element_type=F32)


def _mm_nt(a, b):
    return lax.dot_general(a, b, (((1,), (1,)), ((), ())), preferred_element_type=F32)


def _tile_lanes(x, n):
    return x if n == 1 else jnp.concatenate([x] * n, axis=1)


def _mod_kernel(c_ref, w_ref, b_ref, o_ref):
    c = c_ref[...]
    s = (c * jax.nn.sigmoid(c)).astype(BF16)
    o_ref[0] = _mm(s, w_ref[0].astype(BF16)) + b_ref[0]


def _mod_vectors(crow, w_mod, b_mod):
    depth, d, nm = w_mod.shape
    tn = 1024 if nm % 1024 == 0 else 512
    return pl.pallas_call(
        _mod_kernel,
        grid=(depth, nm // tn),
        in_specs=[
            pl.BlockSpec((8, d), lambda l, j: (0, 0)),
            pl.BlockSpec((1, d, tn), lambda l, j: (l, 0, j)),
            pl.BlockSpec((1, 1, tn), lambda l, j: (l, 0, j)),
        ],
        out_specs=pl.BlockSpec((1, 8, tn), lambda l, j: (l, 0, j)),
        out_shape=jax.ShapeDtypeStruct((depth, 8, nm), F32),
        compiler_params=_cparams("parallel", "parallel"),
        name="mod_vectors",
    )(crow, w_mod, b_mod.reshape(depth, 1, nm))


def _norm_mod(x, gain, shift, scale):
    r = lax.rsqrt(jnp.mean(x * x, axis=-1, keepdims=True) + EPS)
    return (x * r) * (gain * (1.0 + scale)) + shift


def _group_rms(x, ones_bd, gain, group):
    x2 = (x * x).astype(BF16)
    w = ones_bd.shape[0]
    parts = [_mm(x2[:, c:c + w], ones_bd) for c in range(0, x.shape[1], w)]
    ms = (parts[0] if len(parts) == 1 else jnp.concatenate(parts, axis=1)) * (1.0 / group)
    return x * lax.rsqrt(ms + EPS) * gain


def _rope(x, cos, s_lo, s_hi):
    q = DIFF_DIM // 4
    outs = []
    for c in range(0, x.shape[1], LANES):
        xs = x[:, c:c + LANES]
        outs.append(xs * cos + pltpu.roll(xs, q, 1) * s_lo + pltpu.roll(xs, LANES - q, 1) * s_hi)
    return jnp.concatenate(outs, axis=1)


def _inproj_even_kernel(h_ref, sh_ref, sc_ref, gain_ref, w_ref, gk_ref, ones_ref, cos_ref, slo_ref, shi_ref,
                        q_ref, kr_ref, v_ref, f_ref, a_scr, *, nq, nv):
    j = pl.program_id(1)

    @pl.when(j == 0)
    def _():
        a_scr[...] = _norm_mod(h_ref[...], gain_ref[...], sh_ref[0], sc_ref[0]).astype(BF16)

    acc = _mm(a_scr[...], w_ref[...])

    @pl.when(j < nq)
    def _():
        q_ref[...] = acc.astype(BF16)

    @pl.when(jnp.logical_and(j >= nq, j < 2 * nq))
    def _():
        kn = _group_rms(acc, ones_ref[...], gk_ref[...], DIFF_DIM)
        kr_ref[...] = _rope(kn, cos_ref[...], slo_ref[...], shi_ref[...]).astype(BF16)

    @pl.when(jnp.logical_and(j >= 2 * nq, j < 2 * nq + nv))
    def _():
        v_ref[...] = acc.astype(BF16)

    @pl.when(j >= 2 * nq + nv)
    def _():
        f_ref[...] = acc.astype(BF16)


def _inproj_odd_kernel(h_ref, sh_ref, sc_ref, gain_ref, w_ref, gq_ref, gk_ref, ones_ref,
                       cg_ref, q_ref, k_ref, v_ref, a_scr, *, ncg, nh, qscale):
    j = pl.program_id(1)

    @pl.when(j == 0)
    def _():
        a_scr[...] = _norm_mod(h_ref[...], gain_ref[...], sh_ref[0], sc_ref[0]).astype(BF16)

    acc = _mm(a_scr[...], w_ref[...])

    @pl.when(j < ncg)
    def _():
        cg_ref[...] = acc.astype(BF16)

    @pl.when(jnp.logical_and(j >= ncg, j < ncg + nh))
    def _():
        q_ref[...] = (_group_rms(acc, ones_ref[...], gq_ref[...], NA_DIM) * qscale).astype(BF16)

    @pl.when(jnp.logical_and(j >= ncg + nh, j < ncg + 2 * nh))
    def _():
        k_ref[...] = _group_rms(acc, ones_ref[...], gk_ref[...], NA_DIM).astype(BF16)

    @pl.when(j >= ncg + 2 * nh)
    def _():
        v_ref[...] = acc.astype(BF16)


def _row_specs(tm, d, tiles_per_batch, n_batch):
    def mod_spec(k):
        return pl.BlockSpec((1, 1, d), lambda i, j: (jnp.minimum(i // tiles_per_batch, n_batch) * N_MOD + k, 0, 0))

    return pl.BlockSpec((tm, d), lambda i, j: (i, 0)), mod_spec


def _clamped(tm, tn, lo, n):
    return pl.BlockSpec((tm, tn), lambda i, j: (i, jnp.clip(j - lo, 0, n - 1)))


def _ones_blockdiag(group, size=MXU_DIM):
    idx = np.arange(size) // group
    return jnp.asarray((idx[:, None] == idx[None, :]).astype(np.float32), dtype=BF16)


def _inproj_even(hh, mods, gain, w, layer, gk, rope_tabs, dims):
    r, d = hh.shape
    tm, tn = dims["tm_in"], dims["tn"]
    ev_q, ev_v, fw = dims["ev_q"], dims["ev_v"], dims["fw"]
    nq, nv, nf = ev_q // tn, ev_v // tn, fw // tn
    nj = 2 * nq + nv + nf
    hspec, mod_spec = _row_specs(tm, d, dims["s"] // tm, dims["b"])
    const = lambda shape: pl.BlockSpec(shape, lambda i, j: (0,) * len(shape))
    tab = pl.BlockSpec((tm, LANES), lambda i, j: (i, 0))
    kern = functools.partial(_inproj_even_kernel, nq=nq, nv=nv)
    return pl.pallas_call(
        kern,
        grid=(pl.cdiv(r, tm), nj),
        in_specs=[hspec, mod_spec(0), mod_spec(1), const((1, d)),
                  pl.BlockSpec((None, d, tn), lambda i, j: (layer, 0, j)),
                  const((1, tn)), const((MXU_DIM, MXU_DIM)), tab, tab, tab],
        out_specs=[_clamped(tm, tn, 0, nq), _clamped(tm, tn, nq, nq),
                   _clamped(tm, tn, 2 * nq, nv), _clamped(tm, tn, 2 * nq + nv, nf)],
        out_shape=[jax.ShapeDtypeStruct((r, ev_q), BF16)] * 2
        + [jax.ShapeDtypeStruct((r, ev_v), BF16), jax.ShapeDtypeStruct((r, fw), BF16)],
        scratch_shapes=[pltpu.VMEM((tm, d), BF16)],
        compiler_params=_cparams("parallel", "arbitrary"),
        name="inproj_even",
    )(hh, mods, mods, gain, w, gk, _ones_blockdiag(DIFF_DIM), *rope_tabs)


def _inproj_odd(hh, mods, gain, w, layer, gq, gk, dims):
    r, d = hh.shape
    tm, tn = dims["tm_in"], dims["tn"]
    cw, nw = dims["cw"], dims["nw"]
    ncg, nh = 3 * cw // tn, nw // tn
    nj = ncg + 3 * nh
    hspec, mod_spec = _row_specs(tm, d, dims["s"] // tm, dims["b"])
    const = lambda shape: pl.BlockSpec(shape, lambda i, j: (0,) * len(shape))
    kern = functools.partial(_inproj_odd_kernel, ncg=ncg, nh=nh, qscale=NA_DIM ** -0.5 * LOG2E)
    return pl.pallas_call(
        kern,
        grid=(pl.cdiv(r, tm), nj),
        in_specs=[hspec, mod_spec(0), mod_spec(1), const((1, d)),
                  pl.BlockSpec((None, d, tn), lambda i, j: (layer, 0, j)),
                  const((1, tn)), const((1, tn)), const((MXU_DIM, MXU_DIM))],
        out_specs=[_clamped(tm, tn, 0, ncg), _clamped(tm, tn, ncg, nh), _clamped(tm, tn, ncg + nh, nh),
                   _clamped(tm, tn, ncg + 2 * nh, nh)],
        out_shape=[jax.ShapeDtypeStruct((r, 3 * cw), BF16)] + [jax.ShapeDtypeStruct((r, nw), BF16)] * 3,
        scratch_shapes=[pltpu.VMEM((tm, d), BF16)],
        compiler_params=_cparams("parallel", "arbitrary"),
        name="inproj_odd",
    )(hh, mods, mods, gain, w, gq, gk, _ones_blockdiag(NA_DIM))


def _split_components(q):
    lane = lax.broadcasted_iota(jnp.int32, q.shape, 1)
    zero = jnp.zeros_like(q)
    return jnp.concatenate([jnp.where(lane < DIFF_DIM, q, zero), jnp.where(lane >= DIFF_DIM, q, zero)], axis=0)


def _online_softmax(chunks):
    lead = ATTN_SCORE_LEAD
    rows = chunks[0][0].shape[0]
    scores = [_mm_nt(q2, kk) for q2, kk, _ in chunks[:lead]]
    m = jnp.full((rows, LANES), NEG_BIG, F32)
    acc = jnp.zeros((rows, 2 * DIFF_V), F32)
    for c, (_, _, vv) in enumerate(chunks):
        if c + lead < len(chunks):
            scores.append(_mm_nt(chunks[c + lead][0], chunks[c + lead][1]))
        s = scores[c]
        m_new = jnp.maximum(m, jnp.max(s, axis=-1, keepdims=True))
        alpha = jnp.exp2(m - m_new)
        p = jnp.exp2((s - _tile_lanes(m_new, s.shape[1] // LANES)).astype(BF16))
        v_ext = jnp.concatenate([vv, jnp.ones_like(vv)], axis=1)
        acc = _tile_lanes(alpha, 2) * acc + _mm(p, v_ext)
        m = m_new
    return acc


def _diff_attn_kernel(*refs, n_chunks, tk, lam_init):
    if n_chunks:
        (lam_ref, g_ref, gq_ref, ones_ref, q_ref, cos_ref, slo_ref, shi_ref,
         kc_ref, vc_ref, k_ref, v_ref, o_ref) = refs
    else:
        lam_ref, g_ref, gq_ref, ones_ref, q_ref, kc_ref, vc_ref, o_ref = refs
    tq = q_ref.shape[0]
    qn = _group_rms(q_ref[...].astype(F32), ones_ref[...], gq_ref[...], DIFF_DIM) * (DIFF_DIM ** -0.5 * LOG2E)
    chunks = []
    if n_chunks:
        q2 = _split_components(_rope(qn, cos_ref[...], slo_ref[...], shi_ref[...]).astype(BF16))
        chunks = [(q2, k_ref[c * tk:(c + 1) * tk, :], v_ref[c * tk:(c + 1) * tk, :]) for c in range(n_chunks)]
    chunks.append((_split_components(qn.astype(BF16)), kc_ref[...], vc_ref[...]))
    acc = _online_softmax(chunks)

    lv = lam_ref[...]
    lam = (jnp.exp(jnp.sum(lv[0:1] * lv[1:2], axis=-1, keepdims=True))
           - jnp.exp(jnp.sum(lv[2:3] * lv[3:4], axis=-1, keepdims=True)) + lam_init)
    o2 = acc[:, :DIFF_V] / acc[:, DIFF_V:]
    o = o2[:tq] - lam * o2[tq:]
    o = o * lax.rsqrt(jnp.mean(o * o, axis=-1, keepdims=True) + EPS) * g_ref[...] * (1.0 - lam_init)
    o_ref[...] = o.astype(BF16)


def _diff_attn_latent(lam_vec, subln, gq, q, rope_tabs, kr, v, dims, lam_init):
    b, s, ctx, r = dims["b"], dims["s"], dims["ctx"], dims["r"]
    tq, tk = dims["tq"], dims["tk"]
    nh = q.shape[1] // DIFF_V
    ctx_blk0 = b * s // ctx
    kern = functools.partial(_diff_attn_kernel, n_chunks=s // tk, tk=tk, lam_init=lam_init)
    const = lambda shape: pl.BlockSpec(shape, lambda bi, h, qi: (0, 0))
    tab = pl.BlockSpec((tq, LANES), lambda bi, h, qi: (bi * (s // tq) + qi, 0))
    qspec = pl.BlockSpec((tq, DIFF_V), lambda bi, h, qi: (bi * (s // tq) + qi, h))
    cspec = pl.BlockSpec((ctx, DIFF_V), lambda bi, h, qi: (ctx_blk0 + bi, h))
    kspec = pl.BlockSpec((s, DIFF_V), lambda bi, h, qi: (bi, h))
    return pl.pallas_call(
        kern,
        grid=(b, nh, s // tq),
        in_specs=[const((4, DIFF_DIM)), const((1, DIFF_V)), const((1, DIFF_V)), const((DIFF_V, DIFF_V)),
                  qspec, tab, tab, tab, cspec, cspec, kspec, kspec],
        out_specs=qspec,
        out_shape=jax.ShapeDtypeStruct((r, nh * DIFF_V), BF16),
        compiler_params=_cparams("parallel", "parallel", "arbitrary"),
        name="diff_attn_latent",
    )(lam_vec, subln, gq, _ones_blockdiag(DIFF_DIM, DIFF_V), q, *rope_tabs, kr, v, kr, v)


def _diff_attn_ctx(lam_vec, subln, gq, q, kr, v, o_lat, dims, lam_init):
    b, s, ctx = dims["b"], dims["s"], dims["ctx"]
    nh = q.shape[1] // DIFF_V
    ctx_blk0 = b * s // ctx
    kern = functools.partial(_diff_attn_kernel, n_chunks=0, tk=0, lam_init=lam_init)
    const = lambda shape: pl.BlockSpec(shape, lambda bi, h: (0, 0))
    cspec = pl.BlockSpec((ctx, DIFF_V), lambda bi, h: (ctx_blk0 + bi, h))

    def wrapped(lam_ref, g_ref, gq_ref, ones_ref, q_ref, kc_ref, vc_ref, alias_ref, o_ref):
        del alias_ref
        kern(lam_ref, g_ref, gq_ref, ones_ref, q_ref, kc_ref, vc_ref, o_ref)

    return pl.pallas_call(
        wrapped,
        grid=(b, nh),
        in_specs=[const((4, DIFF_DIM)), const((1, DIFF_V)), const((1, DIFF_V)), const((DIFF_V, DIFF_V)),
                  cspec, cspec, cspec, pl.BlockSpec(memory_space=pl.ANY)],
        out_specs=cspec,
        out_shape=jax.ShapeDtypeStruct(o_lat.shape, BF16),
        input_output_aliases={7: 0},
        compiler_params=_cparams("parallel", "parallel"),
        name="diff_attn_ctx",
    )(lam_vec, subln, gq, _ones_blockdiag(DIFF_DIM, DIFF_V), q, kr, v, o_lat)


def _dft_cs(n):
    k = np.arange(n)
    ang = 2.0 * np.pi * ((k[:, None] * k[None, :]) % n) / n
    return np.cos(ang), np.sin(ang)


def _channel_dft(groups, scale):
    c, s = _dft_cs(FOURIER_DIM)
    eye = np.eye(groups)
    return np.kron(eye, c) * scale, np.kron(eye, s) * scale


def _fourier_rows_kernel(x_ref, w_ref, tc_ref, ts_ref, y_ref, *, cg, fw, nr):
    for c in range(cg):
        y = _mm(w_ref[...], x_ref[:, c * fw:(c + 1) * fw])
        yr, yi = y[:nr], y[nr:]
        tc = _tile_lanes(tc_ref[c], fw // LANES)
        ts = _tile_lanes(ts_ref[c], fw // LANES)
        y_ref[0, 0, :, c * fw:(c + 1) * fw] = (yr * tc + yi * ts).astype(BF16)
        y_ref[0, 1, :, c * fw:(c + 1) * fw] = (yi * tc - yr * ts).astype(BF16)


def _fourier_cols_kernel(y_ref, w2_ref, wc_ref, ws_ref, o_ref, *, ag, fw, nc):
    for a in range(ag):
        yy = jnp.concatenate([y_ref[0, 0, a], y_ref[0, 1, a]], axis=0)
        z = _mm(w2_ref[...], yy)
        out = _mm(z[:nc].astype(BF16), wc_ref[...]) + _mm(z[nc:].astype(BF16), ws_ref[...])
        o_ref[:, a * fw:(a + 1) * fw] = out.astype(BF16)


def _fourier_dense_kernel(x_ref, wc_ref, ws_ref, cn_ref, sn_ref, alias_ref, o_ref):
    del alias_ref
    x = x_ref[...]
    gc = _mm(x, wc_ref[...]).astype(BF16)
    gs = _mm(x, ws_ref[...]).astype(BF16)
    o_ref[...] = (_mm(cn_ref[...], gc) - _mm(sn_ref[...], gs)).astype(BF16)


def _fourier_latent(f, dims):
    b, s, r = dims["b"], dims["s"], dims["r"]
    fw = f.shape[1]
    groups = fw // FOURIER_DIM
    nc = GRID_W
    nr = s // nc
    cg = 4
    ag = 8
    c1, s1 = _dft_cs(nr)
    w1 = jnp.asarray(np.concatenate([c1, -s1], axis=0), dtype=BF16)
    ang = 2.0 * np.pi * (np.arange(nc)[:, None] * np.arange(nr)[None, :]) / s
    tcos = jnp.asarray(np.repeat(np.cos(ang)[:, :, None], LANES, axis=2), dtype=F32)
    tsin = jnp.asarray(np.repeat(np.sin(ang)[:, :, None], LANES, axis=2), dtype=F32)
    x2d = f.reshape(r // nc, nc * fw)
    y = pl.pallas_call(
        functools.partial(_fourier_rows_kernel, cg=cg, fw=fw, nr=nr),
        grid=(b, nc // cg),
        in_specs=[pl.BlockSpec((nr, cg * fw), lambda bi, j: (bi, j)),
                  pl.BlockSpec((2 * nr, nr), lambda bi, j: (0, 0)),
                  pl.BlockSpec((cg, nr, LANES), lambda bi, j: (j, 0, 0)),
                  pl.BlockSpec((cg, nr, LANES), lambda bi, j: (j, 0, 0))],
        out_specs=pl.BlockSpec((1, 2, nr, cg * fw), lambda bi, j: (bi, 0, 0, j)),
        out_shape=jax.ShapeDtypeStruct((b, 2, nr, nc * fw), BF16),
        compiler_params=_cparams("parallel", "parallel"),
        name="fourier_rows",
    )(x2d, w1, tcos, tsin)
    c2, s2 = _dft_cs(nc)
    w2 = jnp.asarray(np.block([[c2, s2], [-s2, c2]]), dtype=BF16)
    wc, ws = _channel_dft(groups, 1.0 / math.sqrt(s * FOURIER_DIM))
    out = pl.pallas_call(
        functools.partial(_fourier_cols_kernel, ag=ag, fw=fw, nc=nc),
        grid=(b, nr // ag),
        in_specs=[pl.BlockSpec((1, 2, ag, nc, fw), lambda bi, j: (bi, 0, j, 0, 0)),
                  pl.BlockSpec((2 * nc, 2 * nc), lambda bi, j: (0, 0)),
                  pl.BlockSpec((fw, fw), lambda bi, j: (0, 0)),
                  pl.BlockSpec((fw, fw), lambda bi, j: (0, 0))],
        out_specs=pl.BlockSpec((nc, ag * fw), lambda bi, j: (bi, j)),
        out_shape=jax.ShapeDtypeStruct((r // nr, nr * fw), BF16),
        compiler_params=_cparams("parallel", "parallel"),
        name="fourier_cols",
    )(y.reshape(b, 2, nr, nc, fw), w2, jnp.asarray(wc, dtype=BF16), jnp.asarray(ws, dtype=BF16))
    return out.reshape(r, fw)


def _fourier_ctx(f, fo, dims):
    b, s, ctx = dims["b"], dims["s"], dims["ctx"]
    fw = f.shape[1]
    groups = fw // FOURIER_DIM
    wc, ws = _channel_dft(groups, 1.0 / math.sqrt(ctx * FOURIER_DIM))
    cn, sn = _dft_cs(ctx)
    blk0 = b * s // ctx
    rows = pl.BlockSpec((ctx, fw), lambda bi: (blk0 + bi, 0))
    const = lambda shape: pl.BlockSpec(shape, lambda bi: (0, 0))
    return pl.pallas_call(
        _fourier_dense_kernel,
        grid=(b,),
        in_specs=[rows, const((fw, fw)), const((fw, fw)), const((ctx, ctx)), const((ctx, ctx)),
                  pl.BlockSpec(memory_space=pl.ANY)],
        out_specs=rows,
        out_shape=jax.ShapeDtypeStruct(fo.shape, BF16),
        input_output_aliases={5: 0},
        compiler_params=_cparams("parallel"),
        name="fourier_ctx",
    )(f, jnp.asarray(wc, dtype=BF16), jnp.asarray(ws, dtype=BF16),
      jnp.asarray(cn, dtype=BF16), jnp.asarray(sn, dtype=BF16), fo)


def _gated_conv(gb_ref, gc_ref, hh_ref, gcp_ref, hhp_ref, gcn_ref, hhn_ref, w_ref, *, n_lat_tiles, s, ctx):
    i = pl.program_id(0)
    tm = gb_ref.shape[0]
    is_lat = i < n_lat_tiles
    seq = jnp.where(is_lat, s, ctx)
    off = lax.rem(i * tm, seq)
    u = gc_ref[...].astype(F32) * hh_ref[...].astype(F32)
    halo = gcp_ref.shape[0]
    u_prev = gcp_ref[halo - 1:halo, :].astype(F32) * hhp_ref[halo - 1:halo, :].astype(F32)
    u_next = gcn_ref[0:1, :].astype(F32) * hhn_ref[0:1, :].astype(F32)
    u_prev = jnp.where(off == 0, 0.0, u_prev)
    u_next = jnp.where(lax.rem(off + tm, seq) == 0, 0.0, u_next)
    row = lax.broadcasted_iota(jnp.int32, (tm, 1), 0)
    dn = jnp.where(row == 0, u_prev, pltpu.roll(u, 1, 0))
    up = jnp.where(row == tm - 1, u_next, pltpu.roll(u, tm - 1, 0))
    if ctx < tm:
        inner_start = functools.reduce(jnp.logical_or, [row == k * ctx for k in range(1, tm // ctx)])
        inner_end = functools.reduce(jnp.logical_or, [row == k * ctx - 1 for k in range(1, tm // ctx)])
        is_ctx = jnp.logical_not(is_lat)
        dn = jnp.where(jnp.logical_and(is_ctx, inner_start), 0.0, dn)
        up = jnp.where(jnp.logical_and(is_ctx, inner_end), 0.0, up)
    w = w_ref[...]
    y = dn * w[0:1] + u * w[1:2] + up * w[2:3]
    return gb_ref[...].astype(F32) * y


def _conv_specs(cw, tm, r):
    halo = 16
    nhb = tm // halo
    last = r // halo - 1
    cur = lambda off: pl.BlockSpec((tm, cw), lambda i: (i, off))
    prev = lambda off: pl.BlockSpec((halo, cw), lambda i: (jnp.maximum(i * nhb - 1, 0), off))
    nxt = lambda off: pl.BlockSpec((halo, cw), lambda i: (jnp.minimum((i + 1) * nhb, last), off))
    return [cur(0), cur(1), cur(2), prev(1), prev(2), nxt(1), nxt(2), pl.BlockSpec((CONV_K, cw), lambda i: (0, 0))]


NA_BLOCK_ROWS = 8
NA_KEY_ROWS = 16
NA_BLOCKS_PER_TRIP = 4


def _na_plan(rows):
    wr = min(NA_WIN_ROWS, rows)
    starts, variants, keys = [], [], {}
    for jb in range(rows // NA_BLOCK_ROWS):
        r0 = jb * NA_BLOCK_ROWS
        ks = int(np.clip(r0 - wr // 2, 0, rows - NA_KEY_ROWS))
        rs = [int(np.clip(r0 + t - wr // 2, 0, rows - wr)) for t in range(NA_BLOCK_ROWS)]
        key = (r0 - ks, tuple(x - ks for x in rs))
        variants.append(keys.setdefault(key, len(keys)))
        starts.append(ks)
    return starts, variants, list(keys), wr


def _na_bias(rpb, rows):
    _, _, keys, wr = _na_plan(rows)
    col = np.arange(GRID_W)
    cs = np.clip(col - NA_WIN_COLS // 2, 0, GRID_W - NA_WIN_COLS)
    kc = np.arange(GRID_W)
    dc = kc[None, :] - col[:, None] + (NA_WIN_COLS - 1)
    col_ok = (kc[None, :] >= cs[:, None]) & (kc[None, :] < cs[:, None] + NA_WIN_COLS)
    n_dr, n_dc = 2 * NA_WIN_ROWS - 1, 2 * NA_WIN_COLS - 1
    col_sel = ((dc[None] == np.arange(n_dc)[:, None, None]) & col_ok[None]).astype(np.float32)
    t = np.arange(NA_BLOCK_ROWS)
    i = np.arange(NA_KEY_ROWS)
    row_sel, row_okv = [], []
    for r0_rel, rs_rel in keys:
        dr = i[None, :] - (r0_rel + t[:, None]) + (NA_WIN_ROWS - 1)
        rs_arr = np.asarray(rs_rel)
        row_ok = (i[None, :] >= rs_arr[:, None]) & (i[None, :] < rs_arr[:, None] + wr)
        row_sel.append(((dr[None] == np.arange(n_dr)[:, None, None]) & row_ok[None]).astype(np.float32))
        row_okv.append(row_ok)
    row_sel = np.stack(row_sel)
    ok = (jnp.asarray(np.stack(row_okv))[:, None, :, None, :, None]
          & jnp.asarray(col_ok)[None, None, None, :, None, :])
    hi = lax.Precision.HIGHEST
    col_exp = jnp.einsum("hrd,dck->hrck", rpb.astype(F32) * LOG2E, jnp.asarray(col_sel), precision=hi)
    vals = jnp.einsum("vrti,hrck->vhtcik", jnp.asarray(row_sel), col_exp, precision=hi)
    vals = jnp.where(ok, vals, NEG_BIG)
    return vals.reshape(len(keys), rpb.shape[0], NA_BLOCK_ROWS * GRID_W, NA_KEY_ROWS * GRID_W)


def _na_kernel(q_ref, k_ref, v_ref, kc_ref, vc_ref, bias_ref, o_ref, *, starts, variants):
    qb = NA_BLOCK_ROWS * GRID_W
    kb = NA_KEY_ROWS * GRID_W

    def lookup(table, jb):
        out = jnp.int32(table[0])
        for idx in range(1, len(table)):
            out = jnp.where(jb == idx, jnp.int32(table[idx]), out)
        return out

    def block(jb):
        q0 = pl.multiple_of(jb * qb, qb)
        k0 = pl.multiple_of(lookup(starts, jb) * GRID_W, GRID_W)
        q = q_ref[pl.ds(q0, qb), :]
        s_nb = _mm_nt(q, k_ref[pl.ds(k0, kb), :]) + bias_ref[lookup(variants, jb), 0]
        s_cx = _mm_nt(q, kc_ref[...])
        m = jnp.maximum(jnp.max(s_nb, axis=-1, keepdims=True), jnp.max(s_cx, axis=-1, keepdims=True))
        p_nb = jnp.exp2(s_nb - m)
        p_cx = jnp.exp2(s_cx - m)
        l = jnp.sum(p_nb, axis=-1, keepdims=True) + jnp.sum(p_cx, axis=-1, keepdims=True)
        o = _mm(p_nb.astype(BF16), v_ref[pl.ds(k0, kb), :]) + _mm(p_cx.astype(BF16), vc_ref[...])
        o_ref[pl.ds(q0, qb), :] = (o / l).astype(BF16)

    per_trip = math.gcd(NA_BLOCKS_PER_TRIP, len(starts))

    def body(t, carry):
        for u in range(per_trip):
            block(t * per_trip + u)
        return carry

    lax.fori_loop(0, len(starts) // per_trip, body, 0)


def _na_ctx_kernel(q_ref, kc_ref, vc_ref, alias_ref, o_ref):
    del alias_ref
    s = _mm_nt(q_ref[...], kc_ref[...])
    p = jnp.exp2(s - jnp.max(s, axis=-1, keepdims=True))
    o = _mm(p.astype(BF16), vc_ref[...]) / jnp.sum(p, axis=-1, keepdims=True)
    o_ref[...] = o.astype(BF16)


def _na_latent(q, k, v, bias, dims):
    b, s, ctx, r = dims["b"], dims["s"], dims["ctx"], dims["r"]
    nh = q.shape[1] // NA_DIM
    starts, variants, keys, _ = _na_plan(s // GRID_W)
    blk0 = b * s // ctx
    lat = pl.BlockSpec((s, NA_DIM), lambda bi, h: (bi, h))
    cx = pl.BlockSpec((ctx, NA_DIM), lambda bi, h: (blk0 + bi, h))
    nvar = len(keys)
    return pl.pallas_call(
        functools.partial(_na_kernel, starts=tuple(starts), variants=tuple(variants)),
        grid=(b, nh),
        in_specs=[lat, lat, lat, cx, cx,
                  pl.BlockSpec((nvar, 1) + bias.shape[2:], lambda bi, h: (0, h, 0, 0))],
        out_specs=lat,
        out_shape=jax.ShapeDtypeStruct((r, nh * NA_DIM), BF16),
        compiler_params=_cparams("parallel", "arbitrary"),
        name="na_latent",
    )(q, k, v, k, v, bias)


def _na_ctx(q, k, v, o_lat, dims):
    b, s, ctx = dims["b"], dims["s"], dims["ctx"]
    nh = q.shape[1] // NA_DIM
    blk0 = b * s // ctx
    cx = pl.BlockSpec((ctx, NA_DIM), lambda bi, h: (blk0 + bi, h))
    return pl.pallas_call(
        _na_ctx_kernel,
        grid=(b, nh),
        in_specs=[cx, cx, cx, pl.BlockSpec(memory_space=pl.ANY)],
        out_specs=cx,
        out_shape=jax.ShapeDtypeStruct(o_lat.shape, BF16),
        input_output_aliases={3: 0},
        compiler_params=_cparams("parallel", "parallel"),
        name="na_ctx",
    )(q, k, v, o_lat)


def _outproj_kernel(xa_ref, xb_ref, wa_ref, wb_ref, h_ref, gate_ref, o_ref):
    y = _mm(xa_ref[...], wa_ref[...]) + _mm(xb_ref[...], wb_ref[...])
    o_ref[...] = h_ref[...] + gate_ref[0] * y


def _outproj_conv_kernel(*refs, conv_args):
    conv_refs, (xb_ref, wa_ref, wb_ref, h_ref, gate_ref, o_ref) = refs[:8], refs[8:]
    xa = _gated_conv(*conv_refs, **conv_args).astype(BF16)
    y = _mm(xa, wa_ref[...]) + _mm(xb_ref[...], wb_ref[...])
    o_ref[...] = h_ref[...] + gate_ref[0] * y


def _outproj(xa, xb, w, layer, hh, mods, n_tiles, dims, conv_w=None):
    r, d = hh.shape
    tm = dims["tm"]
    kb = xb.shape[1]
    ka = w.shape[1] - kb
    assert ka % kb == 0
    hspec, mod_spec = _row_specs(tm, d, dims["tpb"], dims["b"])
    one = lambda spec: pl.BlockSpec(spec.block_shape, lambda i: spec.index_map(i, 0))
    if conv_w is None:
        kern, xa_specs, xa_args = _outproj_kernel, [pl.BlockSpec((tm, ka), lambda i: (i, 0))], [xa]
    else:
        conv_args = dict(n_lat_tiles=dims["b"] * dims["tpb"], s=dims["s"], ctx=dims["ctx"])
        kern = functools.partial(_outproj_conv_kernel, conv_args=conv_args)
        xa_specs, xa_args = _conv_specs(ka, tm, r), [xa] * 7 + [conv_w]
    return pl.pallas_call(
        kern,
        grid=(n_tiles,),
        in_specs=xa_specs + [pl.BlockSpec((tm, kb), lambda i: (i, 0)),
                             pl.BlockSpec((None, ka, d), lambda i: (layer, 0, 0)),
                             pl.BlockSpec((None, kb, d), lambda i: (layer, ka // kb, 0)),
                             one(hspec), one(mod_spec(2))],
        out_specs=one(hspec),
        out_shape=jax.ShapeDtypeStruct((r, d), F32),
        input_output_aliases={len(xa_specs) + 3: 0},
        compiler_params=_cparams("parallel"),
        name="outproj",
    )(*xa_args, xb, w, w, hh, mods)


def _ffn_kernel(h_ref, sh_ref, sc_ref, gate_ref, gain_ref, wg_ref, wv_ref, wo_ref, o_ref, u_scr, acc_scr):
    j = pl.program_id(1)

    @pl.when(j == 0)
    def _():
        u_scr[...] = _norm_mod(h_ref[...], gain_ref[...], sh_ref[0], sc_ref[0]).astype(BF16)
        acc_scr[...] = jnp.zeros(acc_scr.shape, F32)

    u = u_scr[...]
    g = _mm(u, wg_ref[...])
    val = _mm(u, wv_ref[...])
    act = (g * jax.nn.sigmoid(g) * val).astype(BF16)
    acc_scr[...] += _mm(act, wo_ref[...])

    @pl.when(j == pl.num_programs(1) - 1)
    def _():
        o_ref[...] = h_ref[...] + gate_ref[0] * acc_scr[...]


def _ffn(hh, mods, gain, w_in, w_out, layer, n_tiles, dims):
    r, d = hh.shape
    tm = dims["tm"]
    hid = w_out.shape[1]
    th = 512 if hid % 512 == 0 else 256
    nj = hid // th
    hspec, mod_spec = _row_specs(tm, d, dims["tpb"], dims["b"])
    in_place = n_tiles * tm == r
    return pl.pallas_call(
        _ffn_kernel,
        grid=(n_tiles, nj),
        in_specs=[hspec, mod_spec(3), mod_spec(4), mod_spec(5), pl.BlockSpec((1, d), lambda i, j: (0, 0)),
                  pl.BlockSpec((None, d, th), lambda i, j: (layer, 0, j)),
                  pl.BlockSpec((None, d, th), lambda i, j: (layer, 0, nj + j)),
                  pl.BlockSpec((None, th, d), lambda i, j: (layer, j, 0))],
        out_specs=hspec,
        out_shape=jax.ShapeDtypeStruct((n_tiles * tm, d), F32),
        input_output_aliases={0: 0} if in_place else {},
        scratch_shapes=[pltpu.VMEM((tm, d), BF16), pltpu.VMEM((tm, d), F32)],
        compiler_params=_cparams("parallel", "arbitrary"),
        name="ffn",
    )(hh, mods, mods, mods, gain, w_in, w_in, w_out)


def _rope_tables(dims):
    s, b, ctx = dims["s"], dims["b"], dims["ctx"]
    t = np.arange(s)
    n_freq = DIFF_DIM // 4
    inv = (ROPE_BASE ** (-np.arange(n_freq, dtype=np.float32) / n_freq)).astype(np.float32)
    lane = np.arange(LANES) % DIFF_DIM
    chunk = lane // n_freq
    pos = np.where((chunk // 2)[None, :] == 0, (t // GRID_W)[:, None], (t % GRID_W)[:, None]).astype(np.float32)
    ang = pos * inv[lane % n_freq][None, :]
    cos, sin = np.cos(ang), np.sin(ang)
    s_lo = np.where((chunk % 2)[None, :] == 1, sin, 0.0)
    s_hi = np.where((chunk % 2)[None, :] == 0, -sin, 0.0)
    pad = b * ctx

    def full(tab, fill):
        return jnp.asarray(np.concatenate([np.tile(tab, (b, 1)), np.full((pad, LANES), fill)], axis=0), dtype=F32)

    return full(cos, 1.0), full(s_lo, 0.0), full(s_hi, 0.0)


def kernel(x, c, ctx, c_ctx, w_mod, b_mod, norm_gain, w_ffn_in, w_ffn_out, ev_w_in, ev_w_out, ev_qk_gain,
           ev_lambda, ev_subln_gain, od_w_in, od_w_out, od_qk_gain, od_conv_w, od_rpb):
    b, s, d = x.shape
    n_ctx = ctx.shape[1]
    depth = w_mod.shape[0]
    tm = b * n_ctx
    r = b * s + b * n_ctx
    assert s % tm == 0 and s % GRID_W == 0
    ev_v = ev_w_out.shape[1] - FOURIER_GROUPS * FOURIER_DIM
    dims = dict(b=b, s=s, ctx=n_ctx, r=r, tm=tm, tm_in=min(1024, s), tn=512, tpb=s // tm,
                ev_q=ev_v // DIFF_V * 2 * DIFF_DIM, ev_v=ev_v, fw=FOURIER_GROUPS * FOURIER_DIM,
                cw=od_conv_w.shape[2], nw=od_rpb.shape[1] * NA_DIM,
                tq=min(512, s), tk=min(1024, s))
    n_all, n_lat = r // tm, b * s // tm

    hh = jnp.concatenate([x.reshape(b * s, d), ctx.reshape(b * n_ctx, d)], axis=0)
    crow = jnp.concatenate([c, c_ctx[None, :], jnp.zeros((8 - b - 1, d), F32)], axis=0)
    mod_all = _mod_vectors(crow, w_mod, b_mod)
    rope_tabs = _rope_tables(dims)
    w_ffn_in, w_ffn_out, ev_w_in, ev_w_out, od_w_in, od_w_out = (
        w.astype(BF16) for w in (w_ffn_in, w_ffn_out, ev_w_in, ev_w_out, od_w_in, od_w_out))
    tn = dims["tn"]

    for i in range(depth):
        last = i == depth - 1
        n_out = n_lat if last else n_all
        mods = mod_all[i].reshape(8 * N_MOD, 1, d)
        gain1 = norm_gain[i, 0].reshape(1, d)
        gain2 = norm_gain[i, 1].reshape(1, d)
        j = i // 2
        if i % 2 == 0:
            lam_init = 0.8 - 0.6 * math.exp(-0.3 * i)
            gq = jnp.tile(ev_qk_gain[j, 0], DIFF_V // DIFF_DIM).reshape(1, DIFF_V)
            gk = jnp.tile(ev_qk_gain[j, 1], tn // DIFF_DIM).reshape(1, tn)
            q, kr, v, f = _inproj_even(hh, mods, gain1, ev_w_in, j, gk, rope_tabs, dims)
            subln = ev_subln_gain[j].reshape(1, DIFF_V)
            o = _diff_attn_latent(ev_lambda[j], subln, gq, q, rope_tabs, kr, v, dims, lam_init)
            fo = _fourier_latent(f, dims)
            if not last:
                o = _diff_attn_ctx(ev_lambda[j], subln, gq, q, kr, v, o, dims, lam_init)
                fo = _fourier_ctx(f, fo, dims)
            hh = _outproj(o, fo, ev_w_out, j, hh, mods, n_out, dims)
        else:
            gq = jnp.tile(od_qk_gain[j, 0], tn // NA_DIM).reshape(1, tn)
            gk = jnp.tile(od_qk_gain[j, 1], tn // NA_DIM).reshape(1, tn)
            cg, q, k, v = _inproj_odd(hh, mods, gain1, od_w_in, j, gq, gk, dims)
            o = _na_latent(q, k, v, _na_bias(od_rpb[j], s // GRID_W), dims)
            if not last:
                o = _na_ctx(q, k, v, o, dims)
            hh = _outproj(cg, o, od_w_out, j, hh, mods, n_out, dims, conv_w=od_conv_w[j])
        hh = _ffn(hh, mods, gain2, w_ffn_in, w_ffn_out, i, n_out, dims)

    return hh.reshape(b, s, d)
```

```python
import functools
import math

import jax
import jax.numpy as jnp
import numpy as np
from jax import lax
from jax.experimental import pallas as pl
from jax.experimental.pallas import tpu as pltpu

GRID_W = 64
EPS = 1e-6
N_MOD = 6

DIFF_HEADS = 12
DIFF_DIM = 64
DIFF_V = 2 * DIFF_DIM
FOURIER_GROUPS = 4
FOURIER_DIM = 128
ROPE_BASE = 10000.0

CONV_WIDTH = 1024
CONV_K = 3
NA_HEADS = 8
NA_DIM = 128
NA_WIN_ROWS = 8
NA_WIN_COLS = 16

F32 = jnp.float32
BF16 = jnp.bfloat16
LOG2E = 1.4426950408889634
NEG_BIG = -1e30
LANES = 128
MXU_DIM = 256
VMEM_LIMIT = 56 * 1024 * 1024
ATTN_SCORE_LEAD = 1


def _cparams(*sem):
    return pltpu.CompilerParams(dimension_semantics=sem, vmem_limit_bytes=VMEM_LIMIT)


def _mm(a, b):
    return jnp.dot(a, b, preferred_element_type=F32)


def _mm_nt(a, b):
    return lax.dot_general(a, b, (((1,), (1,)), ((), ())), preferred_element_type=F32)


def _tile_lanes(x, n):
    return x if n == 1 else jnp.concatenate([x] * n, axis=1)


def _mod_kernel(c_ref, w_ref, b_ref, o_ref):
    c = c_ref[...]
    s = (c * jax.nn.sigmoid(c)).astype(BF16)
    o_ref[0] = _mm(s, w_ref[0].astype(BF16)) + b_ref[0]


def _mod_side_specs(w_mod, first_layer, n_steps, step_of):
    depth, d, nm = w_mod.shape
    n_layers = depth - first_layer
    cw = LANES * pl.cdiv(n_layers * nm // LANES, n_steps)
    while nm % cw:
        cw += LANES
    per_layer = nm // cw
    last = n_layers * per_layer - 1

    def blk(*idx):
        t = jnp.minimum(step_of(*idx), last)
        return t // per_layer, t % per_layer

    return [pl.BlockSpec((8, d), lambda *idx: (0, 0)),
            pl.BlockSpec((1, d, cw), lambda *idx: (first_layer + blk(*idx)[0], 0, blk(*idx)[1])),
            pl.BlockSpec((1, 1, cw), lambda *idx: (first_layer + blk(*idx)[0], 0, blk(*idx)[1])),
            pl.BlockSpec((1, 8, cw), lambda *idx: (blk(*idx)[0], 0, blk(*idx)[1]))]


def _mod_vectors(crow, w_mod, b_mod, n_layers):
    depth, d, nm = w_mod.shape
    tn = 1024 if nm % 1024 == 0 else 512
    return pl.pallas_call(
        _mod_kernel,
        grid=(n_layers, nm // tn),
        in_specs=[
            pl.BlockSpec((8, d), lambda l, j: (0, 0)),
            pl.BlockSpec((1, d, tn), lambda l, j: (l, 0, j)),
            pl.BlockSpec((1, 1, tn), lambda l, j: (l, 0, j)),
        ],
        out_specs=pl.BlockSpec((1, 8, tn), lambda l, j: (l, 0, j)),
        out_shape=jax.ShapeDtypeStruct((n_layers, 8, nm), F32),
        compiler_params=_cparams("parallel", "parallel"),
        name="mod_vectors",
    )(crow, w_mod, b_mod.reshape(depth, 1, nm))


def _norm_mod(x, gain, shift, scale):
    r = lax.rsqrt(jnp.mean(x * x, axis=-1, keepdims=True) + EPS)
    return (x * r) * (gain * (1.0 + scale)) + shift


def _group_rms(x, ones_bd, gain, group):
    x2 = (x * x).astype(BF16)
    w = ones_bd.shape[0]
    parts = [_mm(x2[:, c:c + w], ones_bd) for c in range(0, x.shape[1], w)]
    ms = (parts[0] if len(parts) == 1 else jnp.concatenate(parts, axis=1)) * (1.0 / group)
    return x * lax.rsqrt(ms + EPS) * gain


def _rope(x, cos, s_lo, s_hi):
    q = DIFF_DIM // 4
    outs = []
    for c in range(0, x.shape[1], LANES):
        xs = x[:, c:c + LANES]
        outs.append(xs * cos + pltpu.roll(xs, q, 1) * s_lo + pltpu.roll(xs, LANES - q, 1) * s_hi)
    return jnp.concatenate(outs, axis=1)


def _inproj_even_kernel(h_ref, sh_ref, sc_ref, gain_ref, w_ref, gk_ref, ones_ref, cos_ref, slo_ref, shi_ref,
                        q_ref, kr_ref, v_ref, f_ref, a_scr, *, nq, nv):
    j = pl.program_id(1)

    @pl.when(j == 0)
    def _():
        a_scr[...] = _norm_mod(h_ref[...], gain_ref[...], sh_ref[0], sc_ref[0]).astype(BF16)

    acc = _mm(a_scr[...], w_ref[...])

    @pl.when(j < nq)
    def _():
        q_ref[...] = acc.astype(BF16)

    @pl.when(jnp.logical_and(j >= nq, j < 2 * nq))
    def _():
        kn = _group_rms(acc, ones_ref[...], gk_ref[...], DIFF_DIM)
        kr_ref[...] = _rope(kn, cos_ref[...], slo_ref[...], shi_ref[...]).astype(BF16)

    @pl.when(jnp.logical_and(j >= 2 * nq, j < 2 * nq + nv))
    def _():
        v_ref[...] = acc.astype(BF16)

    @pl.when(j >= 2 * nq + nv)
    def _():
        f_ref[...] = acc.astype(BF16)


def _inproj_odd_kernel(h_ref, sh_ref, sc_ref, gain_ref, w_ref, gq_ref, gk_ref, ones_ref,
                       cg_ref, q_ref, k_ref, v_ref, a_scr, *, ncg, nh, qscale):
    j = pl.program_id(1)

    @pl.when(j == 0)
    def _():
        a_scr[...] = _norm_mod(h_ref[...], gain_ref[...], sh_ref[0], sc_ref[0]).astype(BF16)

    acc = _mm(a_scr[...], w_ref[...])

    @pl.when(j < ncg)
    def _():
        cg_ref[...] = acc.astype(BF16)

    @pl.when(jnp.logical_and(j >= ncg, j < ncg + nh))
    def _():
        q_ref[...] = (_group_rms(acc, ones_ref[...], gq_ref[...], NA_DIM) * qscale).astype(BF16)

    @pl.when(jnp.logical_and(j >= ncg + nh, j < ncg + 2 * nh))
    def _():
        k_ref[...] = _group_rms(acc, ones_ref[...], gk_ref[...], NA_DIM).astype(BF16)

    @pl.when(j >= ncg + 2 * nh)
    def _():
        v_ref[...] = acc.astype(BF16)


def _row_specs(tm, d, tiles_per_batch, n_batch):
    def mod_spec(k):
        return pl.BlockSpec((1, 1, d), lambda i, j: (jnp.minimum(i // tiles_per_batch, n_batch) * N_MOD + k, 0, 0))

    return pl.BlockSpec((tm, d), lambda i, j: (i, 0)), mod_spec


def _clamped(tm, tn, lo, n):
    return pl.BlockSpec((tm, tn), lambda i, j: (i, jnp.clip(j - lo, 0, n - 1)))


def _ones_blockdiag(group, size=MXU_DIM):
    idx = np.arange(size) // group
    return jnp.asarray((idx[:, None] == idx[None, :]).astype(np.float32), dtype=BF16)


def _inproj_even(hh, mods, gain, w, layer, gk, rope_tabs, dims):
    r, d = hh.shape
    tm, tn = dims["tm_in"], dims["tn"]
    ev_q, ev_v, fw = dims["ev_q"], dims["ev_v"], dims["fw"]
    nq, nv, nf = ev_q // tn, ev_v // tn, fw // tn
    nj = 2 * nq + nv + nf
    hspec, mod_spec = _row_specs(tm, d, dims["s"] // tm, dims["b"])
    const = lambda shape: pl.BlockSpec(shape, lambda i, j: (0,) * len(shape))
    tab = pl.BlockSpec((tm, LANES), lambda i, j: (i, 0))
    kern = functools.partial(_inproj_even_kernel, nq=nq, nv=nv)
    return pl.pallas_call(
        kern,
        grid=(pl.cdiv(r, tm), nj),
        in_specs=[hspec, mod_spec(0), mod_spec(1), const((1, d)),
                  pl.BlockSpec((None, d, tn), lambda i, j: (layer, 0, j)),
                  const((1, tn)), const((MXU_DIM, MXU_DIM)), tab, tab, tab],
        out_specs=[_clamped(tm, tn, 0, nq), _clamped(tm, tn, nq, nq),
                   _clamped(tm, tn, 2 * nq, nv), _clamped(tm, tn, 2 * nq + nv, nf)],
        out_shape=[jax.ShapeDtypeStruct((r, ev_q), BF16)] * 2
        + [jax.ShapeDtypeStruct((r, ev_v), BF16), jax.ShapeDtypeStruct((r, fw), BF16)],
        scratch_shapes=[pltpu.VMEM((tm, d), BF16)],
        compiler_params=_cparams("parallel", "arbitrary"),
        name="inproj_even",
    )(hh, mods, mods, gain, w, gk, _ones_blockdiag(DIFF_DIM), *rope_tabs)


def _inproj_odd(hh, mods, gain, w, layer, gq, gk, dims):
    r, d = hh.shape
    tm, tn = dims["tm_in"], dims["tn"]
    cw, nw = dims["cw"], dims["nw"]
    ncg, nh = 3 * cw // tn, nw // tn
    nj = ncg + 3 * nh
    hspec, mod_spec = _row_specs(tm, d, dims["s"] // tm, dims["b"])
    const = lambda shape: pl.BlockSpec(shape, lambda i, j: (0,) * len(shape))
    kern = functools.partial(_inproj_odd_kernel, ncg=ncg, nh=nh, qscale=NA_DIM ** -0.5 * LOG2E)
    return pl.pallas_call(
        kern,
        grid=(pl.cdiv(r, tm), nj),
        in_specs=[hspec, mod_spec(0), mod_spec(1), const((1, d)),
                  pl.BlockSpec((None, d, tn), lambda i, j: (layer, 0, j)),
                  const((1, tn)), const((1, tn)), const((MXU_DIM, MXU_DIM))],
        out_specs=[_clamped(tm, tn, 0, ncg), _clamped(tm, tn, ncg, nh), _clamped(tm, tn, ncg + nh, nh),
                   _clamped(tm, tn, ncg + 2 * nh, nh)],
        out_shape=[jax.ShapeDtypeStruct((r, 3 * cw), BF16)] + [jax.ShapeDtypeStruct((r, nw), BF16)] * 3,
        scratch_shapes=[pltpu.VMEM((tm, d), BF16)],
        compiler_params=_cparams("parallel", "arbitrary"),
        name="inproj_odd",
    )(hh, mods, mods, gain, w, gq, gk, _ones_blockdiag(NA_DIM))


def _split_components(q):
    lane = lax.broadcasted_iota(jnp.int32, q.shape, 1)
    zero = jnp.zeros_like(q)
    return jnp.concatenate([jnp.where(lane < DIFF_DIM, q, zero), jnp.where(lane >= DIFF_DIM, q, zero)], axis=0)


def _online_softmax(chunks):
    lead = ATTN_SCORE_LEAD
    rows = chunks[0][0].shape[0]
    scores = [_mm_nt(q2, kk) for q2, kk, _ in chunks[:lead]]
    m = jnp.full((rows, LANES), NEG_BIG, F32)
    acc = jnp.zeros((rows, 2 * DIFF_V), F32)
    for c, (_, _, vv) in enumerate(chunks):
        if c + lead < len(chunks):
            scores.append(_mm_nt(chunks[c + lead][0], chunks[c + lead][1]))
        s = scores[c]
        m_new = jnp.maximum(m, jnp.max(s, axis=-1, keepdims=True))
        alpha = jnp.exp2(m - m_new)
        p = jnp.exp2((s - _tile_lanes(m_new, s.shape[1] // LANES)).astype(BF16))
        v_ext = jnp.concatenate([vv, jnp.ones_like(vv)], axis=1)
        acc = _tile_lanes(alpha, 2) * acc + _mm(p, v_ext)
        m = m_new
    return acc


def _diff_attn_kernel(*refs, n_chunks, tk, lam_init, n_cast=0, with_mods=False):
    if n_chunks:
        (lam_ref, g_ref, gq_ref, ones_ref, q_ref, cos_ref, slo_ref, shi_ref,
         kc_ref, vc_ref, k_ref, v_ref) = refs[:12]
        n_side = n_cast + 3 * with_mods
        o_ref = refs[12 + n_side]
        for w_ref, wo_ref in zip(refs[12:12 + n_cast], refs[13 + n_side:]):
            wo_ref[...] = w_ref[...].astype(BF16)
        if with_mods:
            _mod_kernel(*refs[12 + n_cast:12 + n_side], refs[-1])
    else:
        lam_ref, g_ref, gq_ref, ones_ref, q_ref, kc_ref, vc_ref, o_ref = refs
    tq = q_ref.shape[0]
    qn = _group_rms(q_ref[...].astype(F32), ones_ref[...], gq_ref[...], DIFF_DIM) * (DIFF_DIM ** -0.5 * LOG2E)
    chunks = []
    if n_chunks:
        q2 = _split_components(_rope(qn, cos_ref[...], slo_ref[...], shi_ref[...]).astype(BF16))
        chunks = [(q2, k_ref[c * tk:(c + 1) * tk, :], v_ref[c * tk:(c + 1) * tk, :]) for c in range(n_chunks)]
    chunks.append((_split_components(qn.astype(BF16)), kc_ref[...], vc_ref[...]))
    acc = _online_softmax(chunks)

    lv = lam_ref[...]
    lam = (jnp.exp(jnp.sum(lv[0:1] * lv[1:2], axis=-1, keepdims=True))
           - jnp.exp(jnp.sum(lv[2:3] * lv[3:4], axis=-1, keepdims=True)) + lam_init)
    o2 = acc[:, :DIFF_V] / acc[:, DIFF_V:]
    o = o2[:tq] - lam * o2[tq:]
    o = o * lax.rsqrt(jnp.mean(o * o, axis=-1, keepdims=True) + EPS) * g_ref[...] * (1.0 - lam_init)
    o_ref[...] = o.astype(BF16)


def _cast_block_rows(rows, n_steps):
    rb = 16 * pl.cdiv(pl.cdiv(rows, 16), n_steps)
    while rows % rb:
        rb += 16
    return rb


def _diff_attn_latent(lam_vec, subln, gq, q, rope_tabs, kr, v, dims, lam_init, cast_ws=(), mod_args=None):
    b, s, ctx, r = dims["b"], dims["s"], dims["ctx"], dims["r"]
    tq, tk = dims["tq"], dims["tk"]
    nh = q.shape[1] // DIFF_V
    ctx_blk0 = b * s // ctx
    nq = s // tq
    kern = functools.partial(_diff_attn_kernel, n_chunks=s // tk, tk=tk, lam_init=lam_init, n_cast=len(cast_ws),
                             with_mods=mod_args is not None)
    step_of = lambda bi, h, qi: (bi * nh + h) * nq + qi
    w2d = [w.reshape(-1, w.shape[-1]) for w in cast_ws]
    cast_specs = []
    for w in w2d:
        rb = _cast_block_rows(w.shape[0], b * nh * nq)
        nblk = w.shape[0] // rb
        cast_specs.append(pl.BlockSpec(
            (rb, w.shape[1]), lambda bi, h, qi, nblk=nblk: (jnp.minimum(step_of(bi, h, qi), nblk - 1), 0)))
    mod_in_specs, mod_out_specs, mod_out_shapes, mod_in = [], [], [], []
    if mod_args is not None:
        crow, w_mod, b_mod, first_layer = mod_args
        depth, _, nm = w_mod.shape
        *mod_in_specs, mod_out = _mod_side_specs(w_mod, first_layer, b * nh * nq, step_of)
        mod_out_specs = [mod_out]
        mod_out_shapes = [jax.ShapeDtypeStruct((depth - first_layer, 8, nm), F32)]
        mod_in = [crow, w_mod, b_mod.reshape(depth, 1, nm)]
    const = lambda shape: pl.BlockSpec(shape, lambda bi, h, qi: (0, 0))
    tab = pl.BlockSpec((tq, LANES), lambda bi, h, qi: (bi * (s // tq) + qi, 0))
    qspec = pl.BlockSpec((tq, DIFF_V), lambda bi, h, qi: (bi * (s // tq) + qi, h))
    cspec = pl.BlockSpec((ctx, DIFF_V), lambda bi, h, qi: (ctx_blk0 + bi, h))
    kspec = pl.BlockSpec((s, DIFF_V), lambda bi, h, qi: (bi, h))
    out = pl.pallas_call(
        kern,
        grid=(b, nh, nq),
        in_specs=[const((4, DIFF_DIM)), const((1, DIFF_V)), const((1, DIFF_V)), const((DIFF_V, DIFF_V)),
                  qspec, tab, tab, tab, cspec, cspec, kspec, kspec] + cast_specs + mod_in_specs,
        out_specs=[qspec] + cast_specs + mod_out_specs,
        out_shape=[jax.ShapeDtypeStruct((r, nh * DIFF_V), BF16)]
        + [jax.ShapeDtypeStruct(w.shape, BF16) for w in w2d] + mod_out_shapes,
        compiler_params=_cparams("arbitrary", "arbitrary", "arbitrary"),
        name="diff_attn_latent",
    )(lam_vec, subln, gq, _ones_blockdiag(DIFF_DIM, DIFF_V), q, *rope_tabs, kr, v, kr, v, *w2d, *mod_in)
    casts = [o.reshape(w.shape) for o, w in zip(out[1:1 + len(w2d)], cast_ws)]
    return out[0], casts, (out[-1] if mod_args is not None else None)


def _diff_attn_ctx(lam_vec, subln, gq, q, kr, v, o_lat, dims, lam_init):
    b, s, ctx = dims["b"], dims["s"], dims["ctx"]
    nh = q.shape[1] // DIFF_V
    ctx_blk0 = b * s // ctx
    kern = functools.partial(_diff_attn_kernel, n_chunks=0, tk=0, lam_init=lam_init)
    const = lambda shape: pl.BlockSpec(shape, lambda bi, h: (0, 0))
    cspec = pl.BlockSpec((ctx, DIFF_V), lambda bi, h: (ctx_blk0 + bi, h))

    def wrapped(lam_ref, g_ref, gq_ref, ones_ref, q_ref, kc_ref, vc_ref, alias_ref, o_ref):
        del alias_ref
        kern(lam_ref, g_ref, gq_ref, ones_ref, q_ref, kc_ref, vc_ref, o_ref)

    return pl.pallas_call(
        wrapped,
        grid=(b, nh),
        in_specs=[const((4, DIFF_DIM)), const((1, DIFF_V)), const((1, DIFF_V)), const((DIFF_V, DIFF_V)),
                  cspec, cspec, cspec, pl.BlockSpec(memory_space=pl.ANY)],
        out_specs=cspec,
        out_shape=jax.ShapeDtypeStruct(o_lat.shape, BF16),
        input_output_aliases={7: 0},
        compiler_params=_cparams("parallel", "parallel"),
        name="diff_attn_ctx",
    )(lam_vec, subln, gq, _ones_blockdiag(DIFF_DIM, DIFF_V), q, kr, v, o_lat)


def _dft_cs(n):
    k = np.arange(n)
    ang = 2.0 * np.pi * ((k[:, None] * k[None, :]) % n) / n
    return np.cos(ang), np.sin(ang)


def _channel_dft(groups, scale):
    c, s = _dft_cs(FOURIER_DIM)
    eye = np.eye(groups)
    return np.kron(eye, c) * scale, np.kron(eye, s) * scale


def _fourier_rows_kernel(x_ref, w_ref, tc_ref, ts_ref, y_ref, *, cg, fw, nr):
    for c in range(cg):
        y = _mm(w_ref[...], x_ref[:, c * fw:(c + 1) * fw])
        yr, yi = y[:nr], y[nr:]
        tc = _tile_lanes(tc_ref[c], fw // LANES)
        ts = _tile_lanes(ts_ref[c], fw // LANES)
        y_ref[0, 0, :, c * fw:(c + 1) * fw] = (yr * tc + yi * ts).astype(BF16)
        y_ref[0, 1, :, c * fw:(c + 1) * fw] = (yi * tc - yr * ts).astype(BF16)


def _fourier_cols_kernel(y_ref, w2_ref, wc_ref, ws_ref, o_ref, *, ag, fw, nc):
    for a in range(ag):
        yy = jnp.concatenate([y_ref[0, 0, a], y_ref[0, 1, a]], axis=0)
        z = _mm(w2_ref[...], yy)
        out = _mm(z[:nc].astype(BF16), wc_ref[...]) + _mm(z[nc:].astype(BF16), ws_ref[...])
        o_ref[:, a * fw:(a + 1) * fw] = out.astype(BF16)


def _fourier_dense_kernel(x_ref, wc_ref, ws_ref, cn_ref, sn_ref, alias_ref, o_ref):
    del alias_ref
    x = x_ref[...]
    gc = _mm(x, wc_ref[...]).astype(BF16)
    gs = _mm(x, ws_ref[...]).astype(BF16)
    o_ref[...] = (_mm(cn_ref[...], gc) - _mm(sn_ref[...], gs)).astype(BF16)


def _fourier_latent(f, dims):
    b, s, r = dims["b"], dims["s"], dims["r"]
    fw = f.shape[1]
    groups = fw // FOURIER_DIM
    nc = GRID_W
    nr = s // nc
    cg = 4
    ag = 8
    c1, s1 = _dft_cs(nr)
    w1 = jnp.asarray(np.concatenate([c1, -s1], axis=0), dtype=BF16)
    ang = 2.0 * np.pi * (np.arange(nc)[:, None] * np.arange(nr)[None, :]) / s
    tcos = jnp.asarray(np.repeat(np.cos(ang)[:, :, None], LANES, axis=2), dtype=F32)
    tsin = jnp.asarray(np.repeat(np.sin(ang)[:, :, None], LANES, axis=2), dtype=F32)
    x2d = f.reshape(r // nc, nc * fw)
    y = pl.pallas_call(
        functools.partial(_fourier_rows_kernel, cg=cg, fw=fw, nr=nr),
        grid=(b, nc // cg),
        in_specs=[pl.BlockSpec((nr, cg * fw), lambda bi, j: (bi, j)),
                  pl.BlockSpec((2 * nr, nr), lambda bi, j: (0, 0)),
                  pl.BlockSpec((cg, nr, LANES), lambda bi, j: (j, 0, 0)),
                  pl.BlockSpec((cg, nr, LANES), lambda bi, j: (j, 0, 0))],
        out_specs=pl.BlockSpec((1, 2, nr, cg * fw), lambda bi, j: (bi, 0, 0, j)),
        out_shape=jax.ShapeDtypeStruct((b, 2, nr, nc * fw), BF16),
        compiler_params=_cparams("parallel", "parallel"),
        name="fourier_rows",
    )(x2d, w1, tcos, tsin)
    c2, s2 = _dft_cs(nc)
    w2 = jnp.asarray(np.block([[c2, s2], [-s2, c2]]), dtype=BF16)
    wc, ws = _channel_dft(groups, 1.0 / math.sqrt(s * FOURIER_DIM))
    out = pl.pallas_call(
        functools.partial(_fourier_cols_kernel, ag=ag, fw=fw, nc=nc),
        grid=(b, nr // ag),
        in_specs=[pl.BlockSpec((1, 2, ag, nc, fw), lambda bi, j: (bi, 0, j, 0, 0)),
                  pl.BlockSpec((2 * nc, 2 * nc), lambda bi, j: (0, 0)),
                  pl.BlockSpec((fw, fw), lambda bi, j: (0, 0)),
                  pl.BlockSpec((fw, fw), lambda bi, j: (0, 0))],
        out_specs=pl.BlockSpec((nc, ag * fw), lambda bi, j: (bi, j)),
        out_shape=jax.ShapeDtypeStruct((r // nr, nr * fw), BF16),
        compiler_params=_cparams("parallel", "parallel"),
        name="fourier_cols",
    )(y.reshape(b, 2, nr, nc, fw), w2, jnp.asarray(wc, dtype=BF16), jnp.asarray(ws, dtype=BF16))
    return out.reshape(r, fw)


def _fourier_ctx(f, fo, dims):
    b, s, ctx = dims["b"], dims["s"], dims["ctx"]
    fw = f.shape[1]
    groups = fw // FOURIER_DIM
    wc, ws = _channel_dft(groups, 1.0 / math.sqrt(ctx * FOURIER_DIM))
    cn, sn = _dft_cs(ctx)
    blk0 = b * s // ctx
    rows = pl.BlockSpec((ctx, fw), lambda bi: (blk0 + bi, 0))
    const = lambda shape: pl.BlockSpec(shape, lambda bi: (0, 0))
    return pl.pallas_call(
        _fourier_dense_kernel,
        grid=(b,),
        in_specs=[rows, const((fw, fw)), const((fw, fw)), const((ctx, ctx)), const((ctx, ctx)),
                  pl.BlockSpec(memory_space=pl.ANY)],
        out_specs=rows,
        out_shape=jax.ShapeDtypeStruct(fo.shape, BF16),
        input_output_aliases={5: 0},
        compiler_params=_cparams("parallel"),
        name="fourier_ctx",
    )(f, jnp.asarray(wc, dtype=BF16), jnp.asarray(ws, dtype=BF16),
      jnp.asarray(cn, dtype=BF16), jnp.asarray(sn, dtype=BF16), fo)


def _gated_conv(gb_ref, gc_ref, hh_ref, gcp_ref, hhp_ref, gcn_ref, hhn_ref, w_ref, *, n_lat_tiles, s, ctx):
    i = pl.program_id(0)
    tm = gb_ref.shape[0]
    is_lat = i < n_lat_tiles
    seq = jnp.where(is_lat, s, ctx)
    off = lax.rem(i * tm, seq)
    u = gc_ref[...].astype(F32) * hh_ref[...].astype(F32)
    halo = gcp_ref.shape[0]
    u_prev = gcp_ref[halo - 1:halo, :].astype(F32) * hhp_ref[halo - 1:halo, :].astype(F32)
    u_next = gcn_ref[0:1, :].astype(F32) * hhn_ref[0:1, :].astype(F32)
    u_prev = jnp.where(off == 0, 0.0, u_prev)
    u_next = jnp.where(lax.rem(off + tm, seq) == 0, 0.0, u_next)
    row = lax.broadcasted_iota(jnp.int32, (tm, 1), 0)
    dn = jnp.where(row == 0, u_prev, pltpu.roll(u, 1, 0))
    up = jnp.where(row == tm - 1, u_next, pltpu.roll(u, tm - 1, 0))
    if ctx < tm:
        inner_start = functools.reduce(jnp.logical_or, [row == k * ctx for k in range(1, tm // ctx)])
        inner_end = functools.reduce(jnp.logical_or, [row == k * ctx - 1 for k in range(1, tm // ctx)])
        is_ctx = jnp.logical_not(is_lat)
        dn = jnp.where(jnp.logical_and(is_ctx, inner_start), 0.0, dn)
        up = jnp.where(jnp.logical_and(is_ctx, inner_end), 0.0, up)
    w = w_ref[...]
    y = dn * w[0:1] + u * w[1:2] + up * w[2:3]
    return gb_ref[...].astype(F32) * y


def _conv_specs(cw, tm, r):
    halo = 16
    nhb = tm // halo
    last = r // halo - 1
    cur = lambda off: pl.BlockSpec((tm, cw), lambda i: (i, off))
    prev = lambda off: pl.BlockSpec((halo, cw), lambda i: (jnp.maximum(i * nhb - 1, 0), off))
    nxt = lambda off: pl.BlockSpec((halo, cw), lambda i: (jnp.minimum((i + 1) * nhb, last), off))
    return [cur(0), cur(1), cur(2), prev(1), prev(2), nxt(1), nxt(2), pl.BlockSpec((CONV_K, cw), lambda i: (0, 0))]


NA_BLOCK_ROWS = 8
NA_KEY_ROWS = 16
NA_BLOCKS_PER_TRIP = 4


def _na_plan(rows):
    wr = min(NA_WIN_ROWS, rows)
    starts, variants, keys = [], [], {}
    for jb in range(rows // NA_BLOCK_ROWS):
        r0 = jb * NA_BLOCK_ROWS
        ks = int(np.clip(r0 - wr // 2, 0, rows - NA_KEY_ROWS))
        rs = [int(np.clip(r0 + t - wr // 2, 0, rows - wr)) for t in range(NA_BLOCK_ROWS)]
        key = (r0 - ks, tuple(x - ks for x in rs))
        variants.append(keys.setdefault(key, len(keys)))
        starts.append(ks)
    return starts, variants, list(keys), wr


def _na_bias(rpb, rows):
    _, _, keys, wr = _na_plan(rows)
    col = np.arange(GRID_W)
    cs = np.clip(col - NA_WIN_COLS // 2, 0, GRID_W - NA_WIN_COLS)
    kc = np.arange(GRID_W)
    dc = kc[None, :] - col[:, None] + (NA_WIN_COLS - 1)
    col_ok = (kc[None, :] >= cs[:, None]) & (kc[None, :] < cs[:, None] + NA_WIN_COLS)
    n_dr, n_dc = 2 * NA_WIN_ROWS - 1, 2 * NA_WIN_COLS - 1
    col_sel = ((dc[None] == np.arange(n_dc)[:, None, None]) & col_ok[None]).astype(np.float32)
    t = np.arange(NA_BLOCK_ROWS)
    i = np.arange(NA_KEY_ROWS)
    row_sel, row_okv = [], []
    for r0_rel, rs_rel in keys:
        dr = i[None, :] - (r0_rel + t[:, None]) + (NA_WIN_ROWS - 1)
        rs_arr = np.asarray(rs_rel)
        row_ok = (i[None, :] >= rs_arr[:, None]) & (i[None, :] < rs_arr[:, None] + wr)
        row_sel.append(((dr[None] == np.arange(n_dr)[:, None, None]) & row_ok[None]).astype(np.float32))
        row_okv.append(row_ok)
    row_sel = np.stack(row_sel)
    ok = (jnp.asarray(np.stack(row_okv))[:, None, :, None, :, None]
          & jnp.asarray(col_ok)[None, None, None, :, None, :])
    hi = lax.Precision.HIGHEST
    col_exp = jnp.einsum("hrd,dck->hrck", rpb.astype(F32) * LOG2E, jnp.asarray(col_sel), precision=hi)
    vals = jnp.einsum("vrti,hrck->vhtcik", jnp.asarray(row_sel), col_exp, precision=hi)
    vals = jnp.where(ok, vals, NEG_BIG)
    return vals.reshape(len(keys), rpb.shape[0], NA_BLOCK_ROWS * GRID_W, NA_KEY_ROWS * GRID_W)


def _na_kernel(q_ref, k_ref, v_ref, kc_ref, vc_ref, bias_ref, o_ref, *, starts, variants):
    qb = NA_BLOCK_ROWS * GRID_W
    kb = NA_KEY_ROWS * GRID_W

    def lookup(table, jb):
        out = jnp.int32(table[0])
        for idx in range(1, len(table)):
            out = jnp.where(jb == idx, jnp.int32(table[idx]), out)
        return out

    def block(jb):
        q0 = pl.multiple_of(jb * qb, qb)
        k0 = pl.multiple_of(lookup(starts, jb) * GRID_W, GRID_W)
        q = q_ref[pl.ds(q0, qb), :]
        s_nb = _mm_nt(q, k_ref[pl.ds(k0, kb), :]) + bias_ref[lookup(variants, jb), 0]
        s_cx = _mm_nt(q, kc_ref[...])
        m = jnp.maximum(jnp.max(s_nb, axis=-1, keepdims=True), jnp.max(s_cx, axis=-1, keepdims=True))
        p_nb = jnp.exp2(s_nb - m)
        p_cx = jnp.exp2(s_cx - m)
        l = jnp.sum(p_nb, axis=-1, keepdims=True) + jnp.sum(p_cx, axis=-1, keepdims=True)
        o = _mm(p_nb.astype(BF16), v_ref[pl.ds(k0, kb), :]) + _mm(p_cx.astype(BF16), vc_ref[...])
        o_ref[pl.ds(q0, qb), :] = (o / l).astype(BF16)

    per_trip = math.gcd(NA_BLOCKS_PER_TRIP, len(starts))

    def body(t, carry):
        for u in range(per_trip):
            block(t * per_trip + u)
        return carry

    lax.fori_loop(0, len(starts) // per_trip, body, 0)


def _na_ctx_kernel(q_ref, kc_ref, vc_ref, alias_ref, o_ref):
    del alias_ref
    s = _mm_nt(q_ref[...], kc_ref[...])
    p = jnp.exp2(s - jnp.max(s, axis=-1, keepdims=True))
    o = _mm(p.astype(BF16), vc_ref[...]) / jnp.sum(p, axis=-1, keepdims=True)
    o_ref[...] = o.astype(BF16)


def _na_latent(q, k, v, bias, dims):
    b, s, ctx, r = dims["b"], dims["s"], dims["ctx"], dims["r"]
    nh = q.shape[1] // NA_DIM
    starts, variants, keys, _ = _na_plan(s // GRID_W)
    blk0 = b * s // ctx
    lat = pl.BlockSpec((s, NA_DIM), lambda bi, h: (bi, h))
    cx = pl.BlockSpec((ctx, NA_DIM), lambda bi, h: (blk0 + bi, h))
    nvar = len(keys)
    return pl.pallas_call(
        functools.partial(_na_kernel, starts=tuple(starts), variants=tuple(variants)),
        grid=(b, nh),
        in_specs=[lat, lat, lat, cx, cx,
                  pl.BlockSpec((nvar, 1) + bias.shape[2:], lambda bi, h: (0, h, 0, 0))],
        out_specs=lat,
        out_shape=jax.ShapeDtypeStruct((r, nh * NA_DIM), BF16),
        compiler_params=_cparams("parallel", "arbitrary"),
        name="na_latent",
    )(q, k, v, k, v, bias)


def _na_ctx(q, k, v, o_lat, dims):
    b, s, ctx = dims["b"], dims["s"], dims["ctx"]
    nh = q.shape[1] // NA_DIM
    blk0 = b * s // ctx
    cx = pl.BlockSpec((ctx, NA_DIM), lambda bi, h: (blk0 + bi, h))
    return pl.pallas_call(
        _na_ctx_kernel,
        grid=(b, nh),
        in_specs=[cx, cx, cx, pl.BlockSpec(memory_space=pl.ANY)],
        out_specs=cx,
        out_shape=jax.ShapeDtypeStruct(o_lat.shape, BF16),
        input_output_aliases={3: 0},
        compiler_params=_cparams("parallel", "parallel"),
        name="na_ctx",
    )(q, k, v, o_lat)


def _outproj_kernel(xa_ref, xb_ref, wa_ref, wb_ref, h_ref, gate_ref, o_ref):
    y = _mm(xa_ref[...], wa_ref[...]) + _mm(xb_ref[...], wb_ref[...])
    o_ref[...] = h_ref[...] + gate_ref[0] * y


def _outproj_conv_kernel(*refs, conv_args):
    conv_refs, (xb_ref, wa_ref, wb_ref, h_ref, gate_ref, o_ref) = refs[:8], refs[8:]
    xa = _gated_conv(*conv_refs, **conv_args).astype(BF16)
    y = _mm(xa, wa_ref[...]) + _mm(xb_ref[...], wb_ref[...])
    o_ref[...] = h_ref[...] + gate_ref[0] * y


def _outproj(xa, xb, w, layer, hh, mods, n_tiles, dims, conv_w=None):
    r, d = hh.shape
    tm = dims["tm"]
    kb = xb.shape[1]
    ka = w.shape[1] - kb
    assert ka % kb == 0
    hspec, mod_spec = _row_specs(tm, d, dims["tpb"], dims["b"])
    one = lambda spec: pl.BlockSpec(spec.block_shape, lambda i: spec.index_map(i, 0))
    if conv_w is None:
        kern, xa_specs, xa_args = _outproj_kernel, [pl.BlockSpec((tm, ka), lambda i: (i, 0))], [xa]
    else:
        conv_args = dict(n_lat_tiles=dims["b"] * dims["tpb"], s=dims["s"], ctx=dims["ctx"])
        kern = functools.partial(_outproj_conv_kernel, conv_args=conv_args)
        xa_specs, xa_args = _conv_specs(ka, tm, r), [xa] * 7 + [conv_w]
    return pl.pallas_call(
        kern,
        grid=(n_tiles,),
        in_specs=xa_specs + [pl.BlockSpec((tm, kb), lambda i: (i, 0)),
                             pl.BlockSpec((None, ka, d), lambda i: (layer, 0, 0)),
                             pl.BlockSpec((None, kb, d), lambda i: (layer, ka // kb, 0)),
                             one(hspec), one(mod_spec(2))],
        out_specs=one(hspec),
        out_shape=jax.ShapeDtypeStruct((r, d), F32),
        input_output_aliases={len(xa_specs) + 3: 0},
        compiler_params=_cparams("parallel"),
        name="outproj",
    )(*xa_args, xb, w, w, hh, mods)


def _ffn_kernel(h_ref, sh_ref, sc_ref, gate_ref, gain_ref, wg_ref, wv_ref, wo_ref, o_ref, u_scr, acc_scr):
    j = pl.program_id(1)

    @pl.when(j == 0)
    def _():
        u_scr[...] = _norm_mod(h_ref[...], gain_ref[...], sh_ref[0], sc_ref[0]).astype(BF16)
        acc_scr[...] = jnp.zeros(acc_scr.shape, F32)

    u = u_scr[...]
    g = _mm(u, wg_ref[...])
    val = _mm(u, wv_ref[...])
    act = (g * jax.nn.sigmoid(g) * val).astype(BF16)
    acc_scr[...] += _mm(act, wo_ref[...])

    @pl.when(j == pl.num_programs(1) - 1)
    def _():
        o_ref[...] = h_ref[...] + gate_ref[0] * acc_scr[...]


def _ffn(hh, mods, gain, w_in, w_out, layer, n_tiles, dims):
    r, d = hh.shape
    tm = dims["tm"]
    hid = w_out.shape[1]
    th = 512 if hid % 512 == 0 else 256
    nj = hid // th
    hspec, mod_spec = _row_specs(tm, d, dims["tpb"], dims["b"])
    in_place = n_tiles * tm == r
    return pl.pallas_call(
        _ffn_kernel,
        grid=(n_tiles, nj),
        in_specs=[hspec, mod_spec(3), mod_spec(4), mod_spec(5), pl.BlockSpec((1, d), lambda i, j: (0, 0)),
                  pl.BlockSpec((None, d, th), lambda i, j: (layer, 0, j)),
                  pl.BlockSpec((None, d, th), lambda i, j: (layer, 0, nj + j)),
                  pl.BlockSpec((None, th, d), lambda i, j: (layer, j, 0))],
        out_specs=hspec,
        out_shape=jax.ShapeDtypeStruct((n_tiles * tm, d), F32),
        input_output_aliases={0: 0} if in_place else {},
        scratch_shapes=[pltpu.VMEM((tm, d), BF16), pltpu.VMEM((tm, d), F32)],
        compiler_params=_cparams("parallel", "arbitrary"),
        name="ffn",
    )(hh, mods, mods, mods, gain, w_in, w_in, w_out)


def _rope_tables(dims):
    s, b, ctx = dims["s"], dims["b"], dims["ctx"]
    t = np.arange(s)
    n_freq = DIFF_DIM // 4
    inv = (ROPE_BASE ** (-np.arange(n_freq, dtype=np.float32) / n_freq)).astype(np.float32)
    lane = np.arange(LANES) % DIFF_DIM
    chunk = lane // n_freq
    pos = np.where((chunk // 2)[None, :] == 0, (t // GRID_W)[:, None], (t % GRID_W)[:, None]).astype(np.float32)
    ang = pos * inv[lane % n_freq][None, :]
    cos, sin = np.cos(ang), np.sin(ang)
    s_lo = np.where((chunk % 2)[None, :] == 1, sin, 0.0)
    s_hi = np.where((chunk % 2)[None, :] == 0, -sin, 0.0)
    pad = b * ctx

    def full(tab, fill):
        return jnp.asarray(np.concatenate([np.tile(tab, (b, 1)), np.full((pad, LANES), fill)], axis=0), dtype=F32)

    return full(cos, 1.0), full(s_lo, 0.0), full(s_hi, 0.0)


def kernel(x, c, ctx, c_ctx, w_mod, b_mod, norm_gain, w_ffn_in, w_ffn_out, ev_w_in, ev_w_out, ev_qk_gain,
           ev_lambda, ev_subln_gain, od_w_in, od_w_out, od_qk_gain, od_conv_w, od_rpb):
    b, s, d = x.shape
    n_ctx = ctx.shape[1]
    depth = w_mod.shape[0]
    tm = b * n_ctx
    r = b * s + b * n_ctx
    assert s % tm == 0 and s % GRID_W == 0
    ev_v = ev_w_out.shape[1] - FOURIER_GROUPS * FOURIER_DIM
    dims = dict(b=b, s=s, ctx=n_ctx, r=r, tm=tm, tm_in=min(1024, s), tn=512, tpb=s // tm,
                ev_q=ev_v // DIFF_V * 2 * DIFF_DIM, ev_v=ev_v, fw=FOURIER_GROUPS * FOURIER_DIM,
                cw=od_conv_w.shape[2], nw=od_rpb.shape[1] * NA_DIM,
                tq=min(512, s), tk=min(1024, s))
    n_all, n_lat = r // tm, b * s // tm

    hh = jnp.concatenate([x.reshape(b * s, d), ctx.reshape(b * n_ctx, d)], axis=0)
    crow = jnp.concatenate([c, c_ctx[None, :], jnp.zeros((8 - b - 1, d), F32)], axis=0)
    mod_first = _mod_vectors(crow, w_mod, b_mod, 1)
    mod_later = None
    rope_tabs = _rope_tables(dims)
    ev_w_in = ev_w_in.astype(BF16)
    later_ws = None
    tn = dims["tn"]

    for i in range(depth):
        last = i == depth - 1
        n_out = n_lat if last else n_all
        mods = (mod_first[0] if i == 0 else mod_later[i - 1]).reshape(8 * N_MOD, 1, d)
        gain1 = norm_gain[i, 0].reshape(1, d)
        gain2 = norm_gain[i, 1].reshape(1, d)
        j = i // 2
        if i % 2 == 0:
            lam_init = 0.8 - 0.6 * math.exp(-0.3 * i)
            gq = jnp.tile(ev_qk_gain[j, 0], DIFF_V // DIFF_DIM).reshape(1, DIFF_V)
            gk = jnp.tile(ev_qk_gain[j, 1], tn // DIFF_DIM).reshape(1, tn)
            q, kr, v, f = _inproj_even(hh, mods, gain1, ev_w_in, j, gk, rope_tabs, dims)
            subln = ev_subln_gain[j].reshape(1, DIFF_V)
            first = later_ws is None
            o, cast, side_mods = _diff_attn_latent(
                ev_lambda[j], subln, gq, q, rope_tabs, kr, v, dims, lam_init,
                cast_ws=(w_ffn_in, w_ffn_out, ev_w_out, od_w_in, od_w_out) if first else (),
                mod_args=(crow, w_mod, b_mod, 1) if first and depth > 1 else None)
            if first:
                later_ws, mod_later = cast, side_mods
                w_ffn_in, w_ffn_out, ev_w_out, od_w_in, od_w_out = cast
            fo = _fourier_latent(f, dims)
            if not last:
                o = _diff_attn_ctx(ev_lambda[j], subln, gq, q, kr, v, o, dims, lam_init)
                fo = _fourier_ctx(f, fo, dims)
            hh = _outproj(o, fo, ev_w_out, j, hh, mods, n_out, dims)
        else:
            gq = jnp.tile(od_qk_gain[j, 0], tn // NA_DIM).reshape(1, tn)
            gk = jnp.tile(od_qk_gain[j, 1], tn // NA_DIM).reshape(1, tn)
            cg, q, k, v = _inproj_odd(hh, mods, gain1, od_w_in, j, gq, gk, dims)
            o = _na_latent(q, k, v, _na_bias(od_rpb[j], s // GRID_W), dims)
            if not last:
                o = _na_ctx(q, k, v, o, dims)
            hh = _outproj(cg, o, od_w_out, j, hh, mods, n_out, dims, conv_w=od_conv_w[j])
        hh = _ffn(hh, mods, gain2, w_ffn_in, w_ffn_out, i, n_out, dims)

    return hh.reshape(b, s, d)
```

```python
import functools
import math

import jax
import jax.numpy as jnp
import numpy as np
from jax import lax
from jax.experimental import pallas as pl
from jax.experimental.pallas import tpu as pltpu

GRID_W = 64
EPS = 1e-6
N_MOD = 6

DIFF_HEADS = 12
DIFF_DIM = 64
DIFF_V = 2 * DIFF_DIM
FOURIER_GROUPS = 4
FOURIER_DIM = 128
ROPE_BASE = 10000.0

CONV_WIDTH = 1024
CONV_K = 3
NA_HEADS = 8
NA_DIM = 128
NA_WIN_ROWS = 8
NA_WIN_COLS = 16

F32 = jnp.float32
BF16 = jnp.bfloat16
LOG2E = 1.4426950408889634
NEG_BIG = -1e30
LANES = 128
MXU_DIM = 256
VMEM_LIMIT = 56 * 1024 * 1024
ATTN_SCORE_LEAD = 1


def _cparams(*sem):
    return pltpu.CompilerParams(dimension_semantics=sem, vmem_limit_bytes=VMEM_LIMIT)


def _mm(a, b):
    return jnp.dot(a, b, preferred_element_type=F32)


def _mm_nt(a, b):
    return lax.dot_general(a, b, (((1,), (1,)), ((), ())), preferred_element_type=F32)


def _tile_lanes(x, n):
    return x if n == 1 else jnp.concatenate([x] * n, axis=1)


def _mod_kernel(c_ref, w_ref, b_ref, o_ref):
    c = c_ref[...]
    s = (c * jax.nn.sigmoid(c)).astype(BF16)
    o_ref[0] = _mm(s, w_ref[0].astype(BF16)) + b_ref[0]


def _mod_side_specs(w_mod, first_layer, n_steps, step_of):
    depth, d, nm = w_mod.shape
    n_layers = depth - first_layer
    cw = LANES * pl.cdiv(n_layers * nm // LANES, n_steps)
    while nm % cw:
        cw += LANES
    per_layer = nm // cw
    last = n_layers * per_layer - 1

    def blk(*idx):
        t = jnp.minimum(step_of(*idx), last)
        return t // per_layer, t % per_layer

    return [pl.BlockSpec((8, d), lambda *idx: (0, 0)),
            pl.BlockSpec((1, d, cw), lambda *idx: (first_layer + blk(*idx)[0], 0, blk(*idx)[1])),
            pl.BlockSpec((1, 1, cw), lambda *idx: (first_layer + blk(*idx)[0], 0, blk(*idx)[1])),
            pl.BlockSpec((1, 8, cw), lambda *idx: (blk(*idx)[0], 0, blk(*idx)[1]))]


def _mod_vectors(crow, w_mod, b_mod, n_layers):
    depth, d, nm = w_mod.shape
    tn = 1024 if nm % 1024 == 0 else 512
    return pl.pallas_call(
        _mod_kernel,
        grid=(n_layers, nm // tn),
        in_specs=[
            pl.BlockSpec((8, d), lambda l, j: (0, 0)),
            pl.BlockSpec((1, d, tn), lambda l, j: (l, 0, j)),
            pl.BlockSpec((1, 1, tn), lambda l, j: (l, 0, j)),
        ],
        out_specs=pl.BlockSpec((1, 8, tn), lambda l, j: (l, 0, j)),
        out_shape=jax.ShapeDtypeStruct((n_layers, 8, nm), F32),
        compiler_params=_cparams("parallel", "parallel"),
        name="mod_vectors",
    )(crow, w_mod, b_mod.reshape(depth, 1, nm))


def _norm_mod(x, gain, shift, scale):
    r = lax.rsqrt(jnp.mean(x * x, axis=-1, keepdims=True) + EPS)
    return (x * r) * (gain * (1.0 + scale)) + shift


def _group_rms(x, ones_bd, gain, group):
    x2 = (x * x).astype(BF16)
    w = ones_bd.shape[0]
    parts = [_mm(x2[:, c:c + w], ones_bd) for c in range(0, x.shape[1], w)]
    ms = (parts[0] if len(parts) == 1 else jnp.concatenate(parts, axis=1)) * (1.0 / group)
    return x * lax.rsqrt(ms + EPS) * gain


def _rope(x, cos, s_lo, s_hi):
    q = DIFF_DIM // 4
    outs = []
    for c in range(0, x.shape[1], LANES):
        xs = x[:, c:c + LANES]
        outs.append(xs * cos + pltpu.roll(xs, q, 1) * s_lo + pltpu.roll(xs, LANES - q, 1) * s_hi)
    return jnp.concatenate(outs, axis=1)


def _inproj_even_kernel(h_ref, sh_ref, sc_ref, gain_ref, w_ref, gk_ref, ones_ref, cos_ref, slo_ref, shi_ref,
                        q_ref, kr_ref, v_ref, f_ref, a_scr, *, nq, nv):
    j = pl.program_id(1)

    @pl.when(j == 0)
    def _():
        a_scr[...] = _norm_mod(h_ref[...], gain_ref[...], sh_ref[0], sc_ref[0]).astype(BF16)

    acc = _mm(a_scr[...], w_ref[...])

    @pl.when(j < nq)
    def _():
        q_ref[...] = acc.astype(BF16)

    @pl.when(jnp.logical_and(j >= nq, j < 2 * nq))
    def _():
        kn = _group_rms(acc, ones_ref[...], gk_ref[...], DIFF_DIM)
        kr_ref[...] = _rope(kn, cos_ref[...], slo_ref[...], shi_ref[...]).astype(BF16)

    @pl.when(jnp.logical_and(j >= 2 * nq, j < 2 * nq + nv))
    def _():
        v_ref[...] = acc.astype(BF16)

    @pl.when(j >= 2 * nq + nv)
    def _():
        f_ref[...] = acc.astype(BF16)


def _inproj_odd_kernel(h_ref, sh_ref, sc_ref, gain_ref, w_ref, gq_ref, gk_ref, ones_ref,
                       cg_ref, q_ref, k_ref, v_ref, a_scr, *, ncg, nh, qscale):
    j = pl.program_id(1)

    @pl.when(j == 0)
    def _():
        a_scr[...] = _norm_mod(h_ref[...], gain_ref[...], sh_ref[0], sc_ref[0]).astype(BF16)

    acc = _mm(a_scr[...], w_ref[...])

    @pl.when(j < ncg)
    def _():
        cg_ref[...] = acc.astype(BF16)

    @pl.when(jnp.logical_and(j >= ncg, j < ncg + nh))
    def _():
        q_ref[...] = (_group_rms(acc, ones_ref[...], gq_ref[...], NA_DIM) * qscale).astype(BF16)

    @pl.when(jnp.logical_and(j >= ncg + nh, j < ncg + 2 * nh))
    def _():
        k_ref[...] = _group_rms(acc, ones_ref[...], gk_ref[...], NA_DIM).astype(BF16)

    @pl.when(j >= ncg + 2 * nh)
    def _():
        v_ref[...] = acc.astype(BF16)


def _row_specs(tm, d, tiles_per_batch, n_batch):
    def mod_spec(k):
        return pl.BlockSpec((1, 1, d), lambda i, j: (jnp.minimum(i // tiles_per_batch, n_batch) * N_MOD + k, 0, 0))

    return pl.BlockSpec((tm, d), lambda i, j: (i, 0)), mod_spec


def _clamped(tm, tn, lo, n):
    return pl.BlockSpec((tm, tn), lambda i, j: (i, jnp.clip(j - lo, 0, n - 1)))


def _ones_blockdiag(group, size=MXU_DIM):
    idx = np.arange(size) // group
    return jnp.asarray((idx[:, None] == idx[None, :]).astype(np.float32), dtype=BF16)


def _inproj_even(hh, mods, gain, w, layer, gk, rope_tabs, dims):
    r, d = hh.shape
    tm, tn = dims["tm_in"], dims["tn"]
    ev_q, ev_v, fw = dims["ev_q"], dims["ev_v"], dims["fw"]
    nq, nv, nf = ev_q // tn, ev_v // tn, fw // tn
    nj = 2 * nq + nv + nf
    hspec, mod_spec = _row_specs(tm, d, dims["s"] // tm, dims["b"])
    const = lambda shape: pl.BlockSpec(shape, lambda i, j: (0,) * len(shape))
    tab = pl.BlockSpec((tm, LANES), lambda i, j: (i, 0))
    kern = functools.partial(_inproj_even_kernel, nq=nq, nv=nv)
    return pl.pallas_call(
        kern,
        grid=(pl.cdiv(r, tm), nj),
        in_specs=[hspec, mod_spec(0), mod_spec(1), const((1, d)),
                  pl.BlockSpec((None, d, tn), lambda i, j: (layer, 0, j)),
                  const((1, tn)), const((MXU_DIM, MXU_DIM)), tab, tab, tab],
        out_specs=[_clamped(tm, tn, 0, nq), _clamped(tm, tn, nq, nq),
                   _clamped(tm, tn, 2 * nq, nv), _clamped(tm, tn, 2 * nq + nv, nf)],
        out_shape=[jax.ShapeDtypeStruct((r, ev_q), BF16)] * 2
        + [jax.ShapeDtypeStruct((r, ev_v), BF16), jax.ShapeDtypeStruct((r, fw), BF16)],
        scratch_shapes=[pltpu.VMEM((tm, d), BF16)],
        compiler_params=_cparams("parallel", "arbitrary"),
        name="inproj_even",
    )(hh, mods, mods, gain, w, gk, _ones_blockdiag(DIFF_DIM), *rope_tabs)


def _inproj_odd(hh, mods, gain, w, layer, gq, gk, dims):
    r, d = hh.shape
    tm, tn = dims["tm_in"], dims["tn"]
    cw, nw = dims["cw"], dims["nw"]
    ncg, nh = 3 * cw // tn, nw // tn
    nj = ncg + 3 * nh
    hspec, mod_spec = _row_specs(tm, d, dims["s"] // tm, dims["b"])
    const = lambda shape: pl.BlockSpec(shape, lambda i, j: (0,) * len(shape))
    kern = functools.partial(_inproj_odd_kernel, ncg=ncg, nh=nh, qscale=NA_DIM ** -0.5 * LOG2E)
    return pl.pallas_call(
        kern,
        grid=(pl.cdiv(r, tm), nj),
        in_specs=[hspec, mod_spec(0), mod_spec(1), const((1, d)),
                  pl.BlockSpec((None, d, tn), lambda i, j: (layer, 0, j)),
                  const((1, tn)), const((1, tn)), const((MXU_DIM, MXU_DIM))],
        out_specs=[_clamped(tm, tn, 0, ncg), _clamped(tm, tn, ncg, nh), _clamped(tm, tn, ncg + nh, nh),
                   _clamped(tm, tn, ncg + 2 * nh, nh)],
        out_shape=[jax.ShapeDtypeStruct((r, 3 * cw), BF16)] + [jax.ShapeDtypeStruct((r, nw), BF16)] * 3,
        scratch_shapes=[pltpu.VMEM((tm, d), BF16)],
        compiler_params=_cparams("parallel", "arbitrary"),
        name="inproj_odd",
    )(hh, mods, mods, gain, w, gq, gk, _ones_blockdiag(NA_DIM))


def _split_components(q):
    lane = lax.broadcasted_iota(jnp.int32, q.shape, 1)
    zero = jnp.zeros_like(q)
    return jnp.concatenate([jnp.where(lane < DIFF_DIM, q, zero), jnp.where(lane >= DIFF_DIM, q, zero)], axis=0)


def _online_softmax(chunks):
    lead = ATTN_SCORE_LEAD
    rows = chunks[0][0].shape[0]
    scores = [_mm_nt(q2, kk) for q2, kk, _ in chunks[:lead]]
    m = jnp.full((rows, LANES), NEG_BIG, F32)
    acc = jnp.zeros((rows, 2 * DIFF_V), F32)
    for c, (_, _, vv) in enumerate(chunks):
        if c + lead < len(chunks):
            scores.append(_mm_nt(chunks[c + lead][0], chunks[c + lead][1]))
        s = scores[c]
        m_new = jnp.maximum(m, jnp.max(s, axis=-1, keepdims=True))
        alpha = jnp.exp2(m - m_new)
        p = jnp.exp2((s - _tile_lanes(m_new, s.shape[1] // LANES)).astype(BF16))
        v_ext = jnp.concatenate([vv, jnp.ones_like(vv)], axis=1)
        acc = _tile_lanes(alpha, 2) * acc + _mm(p, v_ext)
        m = m_new
    return acc


def _diff_attn_kernel(*refs, n_chunks, tk, lam_init, n_cast=0, with_mods=False):
    if n_chunks:
        (lam_ref, g_ref, gq_ref, ones_ref, q_ref, cos_ref, slo_ref, shi_ref,
         kc_ref, vc_ref, k_ref, v_ref) = refs[:12]
        n_side = n_cast + 3 * with_mods
        o_ref = refs[12 + n_side]
        for w_ref, wo_ref in zip(refs[12:12 + n_cast], refs[13 + n_side:]):
            wo_ref[...] = w_ref[...].astype(BF16)
        if with_mods:
            _mod_kernel(*refs[12 + n_cast:12 + n_side], refs[-1])
    else:
        lam_ref, g_ref, gq_ref, ones_ref, q_ref, kc_ref, vc_ref, o_ref = refs
    tq = q_ref.shape[0]
    qn = _group_rms(q_ref[...].astype(F32), ones_ref[...], gq_ref[...], DIFF_DIM) * (DIFF_DIM ** -0.5 * LOG2E)
    chunks = []
    if n_chunks:
        q2 = _split_components(_rope(qn, cos_ref[...], slo_ref[...], shi_ref[...]).astype(BF16))
        chunks = [(q2, k_ref[c * tk:(c + 1) * tk, :], v_ref[c * tk:(c + 1) * tk, :]) for c in range(n_chunks)]
    chunks.append((_split_components(qn.astype(BF16)), kc_ref[...], vc_ref[...]))
    acc = _online_softmax(chunks)

    lv = lam_ref[...]
    lam = (jnp.exp(jnp.sum(lv[0:1] * lv[1:2], axis=-1, keepdims=True))
           - jnp.exp(jnp.sum(lv[2:3] * lv[3:4], axis=-1, keepdims=True)) + lam_init)
    o2 = acc[:, :DIFF_V] / acc[:, DIFF_V:]
    o = o2[:tq] - lam * o2[tq:]
    o = o * lax.rsqrt(jnp.mean(o * o, axis=-1, keepdims=True) + EPS) * g_ref[...] * (1.0 - lam_init)
    o_ref[...] = o.astype(BF16)


def _cast_block_rows(rows, n_steps):
    rb = 16 * pl.cdiv(pl.cdiv(rows, 16), n_steps)
    while rows % rb:
        rb += 16
    return rb


def _diff_attn_latent(lam_vec, subln, gq, q, rope_tabs, kr, v, dims, lam_init, cast_ws=(), mod_args=None):
    b, s, ctx, r = dims["b"], dims["s"], dims["ctx"], dims["r"]
    tq, tk = dims["tq"], dims["tk"]
    nh = q.shape[1] // DIFF_V
    ctx_blk0 = b * s // ctx
    nq = s // tq
    kern = functools.partial(_diff_attn_kernel, n_chunks=s // tk, tk=tk, lam_init=lam_init, n_cast=len(cast_ws),
                             with_mods=mod_args is not None)
    step_of = lambda bi, h, qi: (bi * nh + h) * nq + qi
    w2d = [w.reshape(-1, w.shape[-1]) for w in cast_ws]
    cast_specs = []
    for w in w2d:
        rb = _cast_block_rows(w.shape[0], b * nh * nq)
        nblk = w.shape[0] // rb
        cast_specs.append(pl.BlockSpec(
            (rb, w.shape[1]), lambda bi, h, qi, nblk=nblk: (jnp.minimum(step_of(bi, h, qi), nblk - 1), 0)))
    mod_in_specs, mod_out_specs, mod_out_shapes, mod_in = [], [], [], []
    if mod_args is not None:
        crow, w_mod, b_mod, first_layer = mod_args
        depth, _, nm = w_mod.shape
        *mod_in_specs, mod_out = _mod_side_specs(w_mod, first_layer, b * nh * nq, step_of)
        mod_out_specs = [mod_out]
        mod_out_shapes = [jax.ShapeDtypeStruct((depth - first_layer, 8, nm), F32)]
        mod_in = [crow, w_mod, b_mod.reshape(depth, 1, nm)]
    const = lambda shape: pl.BlockSpec(shape, lambda bi, h, qi: (0, 0))
    tab = pl.BlockSpec((tq, LANES), lambda bi, h, qi: (bi * (s // tq) + qi, 0))
    qspec = pl.BlockSpec((tq, DIFF_V), lambda bi, h, qi: (bi * (s // tq) + qi, h))
    cspec = pl.BlockSpec((ctx, DIFF_V), lambda bi, h, qi: (ctx_blk0 + bi, h))
    kspec = pl.BlockSpec((s, DIFF_V), lambda bi, h, qi: (bi, h))
    out = pl.pallas_call(
        kern,
        grid=(b, nh, nq),
        in_specs=[const((4, DIFF_DIM)), const((1, DIFF_V)), const((1, DIFF_V)), const((DIFF_V, DIFF_V)),
                  qspec, tab, tab, tab, cspec, cspec, kspec, kspec] + cast_specs + mod_in_specs,
        out_specs=[qspec] + cast_specs + mod_out_specs,
        out_shape=[jax.ShapeDtypeStruct((r, nh * DIFF_V), BF16)]
        + [jax.ShapeDtypeStruct(w.shape, BF16) for w in w2d] + mod_out_shapes,
        compiler_params=_cparams("arbitrary", "arbitrary", "arbitrary"),
        name="diff_attn_latent",
    )(lam_vec, subln, gq, _ones_blockdiag(DIFF_DIM, DIFF_V), q, *rope_tabs, kr, v, kr, v, *w2d, *mod_in)
    casts = [o.reshape(w.shape) for o, w in zip(out[1:1 + len(w2d)], cast_ws)]
    return out[0], casts, (out[-1] if mod_args is not None else None)


def _diff_attn_ctx(lam_vec, subln, gq, q, kr, v, o_lat, dims, lam_init):
    b, s, ctx = dims["b"], dims["s"], dims["ctx"]
    nh = q.shape[1] // DIFF_V
    ctx_blk0 = b * s // ctx
    kern = functools.partial(_diff_attn_kernel, n_chunks=0, tk=0, lam_init=lam_init)
    const = lambda shape: pl.BlockSpec(shape, lambda bi, h: (0, 0))
    cspec = pl.BlockSpec((ctx, DIFF_V), lambda bi, h: (ctx_blk0 + bi, h))

    def wrapped(lam_ref, g_ref, gq_ref, ones_ref, q_ref, kc_ref, vc_ref, alias_ref, o_ref):
        del alias_ref
        kern(lam_ref, g_ref, gq_ref, ones_ref, q_ref, kc_ref, vc_ref, o_ref)

    return pl.pallas_call(
        wrapped,
        grid=(b, nh),
        in_specs=[const((4, DIFF_DIM)), const((1, DIFF_V)), const((1, DIFF_V)), const((DIFF_V, DIFF_V)),
                  cspec, cspec, cspec, pl.BlockSpec(memory_space=pl.ANY)],
        out_specs=cspec,
        out_shape=jax.ShapeDtypeStruct(o_lat.shape, BF16),
        input_output_aliases={7: 0},
        compiler_params=_cparams("parallel", "parallel"),
        name="diff_attn_ctx",
    )(lam_vec, subln, gq, _ones_blockdiag(DIFF_DIM, DIFF_V), q, kr, v, o_lat)


def _dft_cs(n):
    k = np.arange(n)
    ang = 2.0 * np.pi * ((k[:, None] * k[None, :]) % n) / n
    return np.cos(ang), np.sin(ang)


def _channel_dft(groups, scale):
    c, s = _dft_cs(FOURIER_DIM)
    eye = np.eye(groups)
    return np.kron(eye, c) * scale, np.kron(eye, s) * scale


def _fourier_rows_kernel(x_ref, w_ref, tc_ref, ts_ref, y_ref, *, cg, fw, nr):
    for c in range(cg):
        y = _mm(w_ref[...], x_ref[:, c * fw:(c + 1) * fw])
        yr, yi = y[:nr], y[nr:]
        tc = _tile_lanes(tc_ref[c], fw // LANES)
        ts = _tile_lanes(ts_ref[c], fw // LANES)
        y_ref[0, 0, :, c * fw:(c + 1) * fw] = (yr * tc + yi * ts).astype(BF16)
        y_ref[0, 1, :, c * fw:(c + 1) * fw] = (yi * tc - yr * ts).astype(BF16)


def _fourier_cols_kernel(y_ref, w2_ref, wc_ref, ws_ref, o_ref, *, ag, fw, nc):
    for a in range(ag):
        yy = jnp.concatenate([y_ref[0, 0, a], y_ref[0, 1, a]], axis=0)
        z = _mm(w2_ref[...], yy)
        out = _mm(z[:nc].astype(BF16), wc_ref[...]) + _mm(z[nc:].astype(BF16), ws_ref[...])
        o_ref[:, a * fw:(a + 1) * fw] = out.astype(BF16)


def _fourier_dense_kernel(x_ref, wc_ref, ws_ref, cn_ref, sn_ref, alias_ref, o_ref):
    del alias_ref
    x = x_ref[...]
    gc = _mm(x, wc_ref[...]).astype(BF16)
    gs = _mm(x, ws_ref[...]).astype(BF16)
    o_ref[...] = (_mm(cn_ref[...], gc) - _mm(sn_ref[...], gs)).astype(BF16)


def _fourier_latent(f, dims):
    b, s, r = dims["b"], dims["s"], dims["r"]
    fw = f.shape[1]
    groups = fw // FOURIER_DIM
    nc = GRID_W
    nr = s // nc
    cg = 4
    ag = 8
    c1, s1 = _dft_cs(nr)
    w1 = jnp.asarray(np.concatenate([c1, -s1], axis=0), dtype=BF16)
    ang = 2.0 * np.pi * (np.arange(nc)[:, None] * np.arange(nr)[None, :]) / s
    tcos = jnp.asarray(np.repeat(np.cos(ang)[:, :, None], LANES, axis=2), dtype=F32)
    tsin = jnp.asarray(np.repeat(np.sin(ang)[:, :, None], LANES, axis=2), dtype=F32)
    x2d = f.reshape(r // nc, nc * fw)
    y = pl.pallas_call(
        functools.partial(_fourier_rows_kernel, cg=cg, fw=fw, nr=nr),
        grid=(b, nc // cg),
        in_specs=[pl.BlockSpec((nr, cg * fw), lambda bi, j: (bi, j)),
                  pl.BlockSpec((2 * nr, nr), lambda bi, j: (0, 0)),
                  pl.BlockSpec((cg, nr, LANES), lambda bi, j: (j, 0, 0)),
                  pl.BlockSpec((cg, nr, LANES), lambda bi, j: (j, 0, 0))],
        out_specs=pl.BlockSpec((1, 2, nr, cg * fw), lambda bi, j: (bi, 0, 0, j)),
        out_shape=jax.ShapeDtypeStruct((b, 2, nr, nc * fw), BF16),
        compiler_params=_cparams("parallel", "parallel"),
        name="fourier_rows",
    )(x2d, w1, tcos, tsin)
    c2, s2 = _dft_cs(nc)
    w2 = jnp.asarray(np.block([[c2, s2], [-s2, c2]]), dtype=BF16)
    wc, ws = _channel_dft(groups, 1.0 / math.sqrt(s * FOURIER_DIM))
    out = pl.pallas_call(
        functools.partial(_fourier_cols_kernel, ag=ag, fw=fw, nc=nc),
        grid=(b, nr // ag),
        in_specs=[pl.BlockSpec((1, 2, ag, nc, fw), lambda bi, j: (bi, 0, j, 0, 0)),
                  pl.BlockSpec((2 * nc, 2 * nc), lambda bi, j: (0, 0)),
                  pl.BlockSpec((fw, fw), lambda bi, j: (0, 0)),
                  pl.BlockSpec((fw, fw), lambda bi, j: (0, 0))],
        out_specs=pl.BlockSpec((nc, ag * fw), lambda bi, j: (bi, j)),
        out_shape=jax.ShapeDtypeStruct((r // nr, nr * fw), BF16),
        compiler_params=_cparams("parallel", "parallel"),
        name="fourier_cols",
    )(y.reshape(b, 2, nr, nc, fw), w2, jnp.asarray(wc, dtype=BF16), jnp.asarray(ws, dtype=BF16))
    return out.reshape(r, fw)


def _fourier_ctx(f, fo, dims):
    b, s, ctx = dims["b"], dims["s"], dims["ctx"]
    fw = f.shape[1]
    groups = fw // FOURIER_DIM
    wc, ws = _channel_dft(groups, 1.0 / math.sqrt(ctx * FOURIER_DIM))
    cn, sn = _dft_cs(ctx)
    blk0 = b * s // ctx
    rows = pl.BlockSpec((ctx, fw), lambda bi: (blk0 + bi, 0))
    const = lambda shape: pl.BlockSpec(shape, lambda bi: (0, 0))
    return pl.pallas_call(
        _fourier_dense_kernel,
        grid=(b,),
        in_specs=[rows, const((fw, fw)), const((fw, fw)), const((ctx, ctx)), const((ctx, ctx)),
                  pl.BlockSpec(memory_space=pl.ANY)],
        out_specs=rows,
        out_shape=jax.ShapeDtypeStruct(fo.shape, BF16),
        input_output_aliases={5: 0},
        compiler_params=_cparams("parallel"),
        name="fourier_ctx",
    )(f, jnp.asarray(wc, dtype=BF16), jnp.asarray(ws, dtype=BF16),
      jnp.asarray(cn, dtype=BF16), jnp.asarray(sn, dtype=BF16), fo)


def _gated_conv(gb_ref, gc_ref, hh_ref, gcp_ref, hhp_ref, gcn_ref, hhn_ref, w_ref, *, n_lat_tiles, s, ctx):
    i = pl.program_id(0)
    tm = gb_ref.shape[0]
    is_lat = i < n_lat_tiles
    seq = jnp.where(is_lat, s, ctx)
    off = lax.rem(i * tm, seq)
    u = gc_ref[...].astype(F32) * hh_ref[...].astype(F32)
    halo = gcp_ref.shape[0]
    u_prev = gcp_ref[halo - 1:halo, :].astype(F32) * hhp_ref[halo - 1:halo, :].astype(F32)
    u_next = gcn_ref[0:1, :].astype(F32) * hhn_ref[0:1, :].astype(F32)
    u_prev = jnp.where(off == 0, 0.0, u_prev)
    u_next = jnp.where(lax.rem(off + tm, seq) == 0, 0.0, u_next)
    row = lax.broadcasted_iota(jnp.int32, (tm, 1), 0)
    dn = jnp.where(row == 0, u_prev, pltpu.roll(u, 1, 0))
    up = jnp.where(row == tm - 1, u_next, pltpu.roll(u, tm - 1, 0))
    if ctx < tm:
        inner_start = functools.reduce(jnp.logical_or, [row == k * ctx for k in range(1, tm // ctx)])
        inner_end = functools.reduce(jnp.logical_or, [row == k * ctx - 1 for k in range(1, tm // ctx)])
        is_ctx = jnp.logical_not(is_lat)
        dn = jnp.where(jnp.logical_and(is_ctx, inner_start), 0.0, dn)
        up = jnp.where(jnp.logical_and(is_ctx, inner_end), 0.0, up)
    w = w_ref[...]
    y = dn * w[0:1] + u * w[1:2] + up * w[2:3]
    return gb_ref[...].astype(F32) * y


def _conv_specs(cw, tm, r):
    halo = 16
    nhb = tm // halo
    last = r // halo - 1
    cur = lambda off: pl.BlockSpec((tm, cw), lambda i: (i, off))
    prev = lambda off: pl.BlockSpec((halo, cw), lambda i: (jnp.maximum(i * nhb - 1, 0), off))
    nxt = lambda off: pl.BlockSpec((halo, cw), lambda i: (jnp.minimum((i + 1) * nhb, last), off))
    return [cur(0), cur(1), cur(2), prev(1), prev(2), nxt(1), nxt(2), pl.BlockSpec((CONV_K, cw), lambda i: (0, 0))]


NA_BLOCK_ROWS = 8
NA_KEY_ROWS = 16
NA_BLOCKS_PER_TRIP = 16


def _na_plan(rows):
    wr = min(NA_WIN_ROWS, rows)
    starts, variants, keys = [], [], {}
    for jb in range(rows // NA_BLOCK_ROWS):
        r0 = jb * NA_BLOCK_ROWS
        ks = int(np.clip(r0 - wr // 2, 0, rows - NA_KEY_ROWS))
        rs = [int(np.clip(r0 + t - wr // 2, 0, rows - wr)) for t in range(NA_BLOCK_ROWS)]
        key = (r0 - ks, tuple(x - ks for x in rs))
        variants.append(keys.setdefault(key, len(keys)))
        starts.append(ks)
    return starts, variants, list(keys), wr


def _na_bias(rpb, rows):
    _, _, keys, wr = _na_plan(rows)
    col = np.arange(GRID_W)
    cs = np.clip(col - NA_WIN_COLS // 2, 0, GRID_W - NA_WIN_COLS)
    kc = np.arange(GRID_W)
    dc = kc[None, :] - col[:, None] + (NA_WIN_COLS - 1)
    col_ok = (kc[None, :] >= cs[:, None]) & (kc[None, :] < cs[:, None] + NA_WIN_COLS)
    n_dr, n_dc = 2 * NA_WIN_ROWS - 1, 2 * NA_WIN_COLS - 1
    col_sel = ((dc[None] == np.arange(n_dc)[:, None, None]) & col_ok[None]).astype(np.float32)
    t = np.arange(NA_BLOCK_ROWS)
    i = np.arange(NA_KEY_ROWS)
    row_sel, row_okv = [], []
    for r0_rel, rs_rel in keys:
        dr = i[None, :] - (r0_rel + t[:, None]) + (NA_WIN_ROWS - 1)
        rs_arr = np.asarray(rs_rel)
        row_ok = (i[None, :] >= rs_arr[:, None]) & (i[None, :] < rs_arr[:, None] + wr)
        row_sel.append(((dr[None] == np.arange(n_dr)[:, None, None]) & row_ok[None]).astype(np.float32))
        row_okv.append(row_ok)
    row_sel = np.stack(row_sel)
    ok = (jnp.asarray(np.stack(row_okv))[:, None, :, None, :, None]
          & jnp.asarray(col_ok)[None, None, None, :, None, :])
    hi = lax.Precision.HIGHEST
    col_exp = jnp.einsum("hrd,dck->hrck", rpb.astype(F32) * LOG2E, jnp.asarray(col_sel), precision=hi)
    vals = jnp.einsum("vrti,hrck->vhtcik", jnp.asarray(row_sel), col_exp, precision=hi)
    vals = jnp.where(ok, vals, NEG_BIG)
    return vals.reshape(len(keys), rpb.shape[0], NA_BLOCK_ROWS * GRID_W, NA_KEY_ROWS * GRID_W)


def _na_kernel(q_ref, k_ref, v_ref, kc_ref, vc_ref, bias_ref, o_ref, *, starts, variants):
    qb = NA_BLOCK_ROWS * GRID_W
    kb = NA_KEY_ROWS * GRID_W

    def lookup(table, jb):
        out = jnp.int32(table[0])
        for idx in range(1, len(table)):
            out = jnp.where(jb == idx, jnp.int32(table[idx]), out)
        return out

    def block(jb):
        q0 = pl.multiple_of(jb * qb, qb)
        k0 = pl.multiple_of(lookup(starts, jb) * GRID_W, GRID_W)
        q = q_ref[pl.ds(q0, qb), :]
        s_nb = _mm_nt(q, k_ref[pl.ds(k0, kb), :]) + bias_ref[lookup(variants, jb), 0]
        s_cx = _mm_nt(q, kc_ref[...])
        m = jnp.maximum(jnp.max(s_nb, axis=-1, keepdims=True), jnp.max(s_cx, axis=-1, keepdims=True))
        p_nb = jnp.exp2((s_nb - m).astype(BF16))
        p_cx = jnp.exp2((s_cx - m).astype(BF16))
        v_nb, v_cx = v_ref[pl.ds(k0, kb), :], vc_ref[...]
        o = (_mm(p_nb, jnp.concatenate([v_nb, jnp.ones_like(v_nb)], axis=1))
             + _mm(p_cx, jnp.concatenate([v_cx, jnp.ones_like(v_cx)], axis=1)))
        o_ref[pl.ds(q0, qb), :] = (o[:, :NA_DIM] / o[:, NA_DIM:]).astype(BF16)

    per_trip = math.gcd(NA_BLOCKS_PER_TRIP, len(starts))

    def body(t, carry):
        for u in range(per_trip):
            block(t * per_trip + u)
        return carry

    lax.fori_loop(0, len(starts) // per_trip, body, 0)


def _na_ctx_kernel(q_ref, kc_ref, vc_ref, alias_ref, o_ref):
    del alias_ref
    s = _mm_nt(q_ref[...], kc_ref[...])
    p = jnp.exp2(s - jnp.max(s, axis=-1, keepdims=True))
    o = _mm(p.astype(BF16), vc_ref[...]) / jnp.sum(p, axis=-1, keepdims=True)
    o_ref[...] = o.astype(BF16)


def _na_latent(q, k, v, bias, dims):
    b, s, ctx, r = dims["b"], dims["s"], dims["ctx"], dims["r"]
    nh = q.shape[1] // NA_DIM
    starts, variants, keys, _ = _na_plan(s // GRID_W)
    blk0 = b * s // ctx
    lat = pl.BlockSpec((s, NA_DIM), lambda bi, h: (bi, h))
    cx = pl.BlockSpec((ctx, NA_DIM), lambda bi, h: (blk0 + bi, h))
    nvar = len(keys)
    return pl.pallas_call(
        functools.partial(_na_kernel, starts=tuple(starts), variants=tuple(variants)),
        grid=(b, nh),
        in_specs=[lat, lat, lat, cx, cx,
                  pl.BlockSpec((nvar, 1) + bias.shape[2:], lambda bi, h: (0, h, 0, 0))],
        out_specs=lat,
        out_shape=jax.ShapeDtypeStruct((r, nh * NA_DIM), BF16),
        compiler_params=_cparams("parallel", "arbitrary"),
        name="na_latent",
    )(q, k, v, k, v, bias)


def _na_ctx(q, k, v, o_lat, dims):
    b, s, ctx = dims["b"], dims["s"], dims["ctx"]
    nh = q.shape[1] // NA_DIM
    blk0 = b * s // ctx
    cx = pl.BlockSpec((ctx, NA_DIM), lambda bi, h: (blk0 + bi, h))
    return pl.pallas_call(
        _na_ctx_kernel,
        grid=(b, nh),
        in_specs=[cx, cx, cx, pl.BlockSpec(memory_space=pl.ANY)],
        out_specs=cx,
        out_shape=jax.ShapeDtypeStruct(o_lat.shape, BF16),
        input_output_aliases={3: 0},
        compiler_params=_cparams("parallel", "parallel"),
        name="na_ctx",
    )(q, k, v, o_lat)


def _outproj_kernel(xa_ref, xb_ref, wa_ref, wb_ref, h_ref, gate_ref, o_ref):
    y = _mm(xa_ref[...], wa_ref[...]) + _mm(xb_ref[...], wb_ref[...])
    o_ref[...] = h_ref[...] + gate_ref[0] * y


def _outproj_conv_kernel(*refs, conv_args):
    conv_refs, (xb_ref, wa_ref, wb_ref, h_ref, gate_ref, o_ref) = refs[:8], refs[8:]
    xa = _gated_conv(*conv_refs, **conv_args).astype(BF16)
    y = _mm(xa, wa_ref[...]) + _mm(xb_ref[...], wb_ref[...])
    o_ref[...] = h_ref[...] + gate_ref[0] * y


def _outproj(xa, xb, w, layer, hh, mods, n_tiles, dims, conv_w=None):
    r, d = hh.shape
    tm = dims["tm"]
    kb = xb.shape[1]
    ka = w.shape[1] - kb
    assert ka % kb == 0
    hspec, mod_spec = _row_specs(tm, d, dims["tpb"], dims["b"])
    one = lambda spec: pl.BlockSpec(spec.block_shape, lambda i: spec.index_map(i, 0))
    if conv_w is None:
        kern, xa_specs, xa_args = _outproj_kernel, [pl.BlockSpec((tm, ka), lambda i: (i, 0))], [xa]
    else:
        conv_args = dict(n_lat_tiles=dims["b"] * dims["tpb"], s=dims["s"], ctx=dims["ctx"])
        kern = functools.partial(_outproj_conv_kernel, conv_args=conv_args)
        xa_specs, xa_args = _conv_specs(ka, tm, r), [xa] * 7 + [conv_w]
    return pl.pallas_call(
        kern,
        grid=(n_tiles,),
        in_specs=xa_specs + [pl.BlockSpec((tm, kb), lambda i: (i, 0)),
                             pl.BlockSpec((None, ka, d), lambda i: (layer, 0, 0)),
                             pl.BlockSpec((None, kb, d), lambda i: (layer, ka // kb, 0)),
                             one(hspec), one(mod_spec(2))],
        out_specs=one(hspec),
        out_shape=jax.ShapeDtypeStruct((r, d), F32),
        input_output_aliases={len(xa_specs) + 3: 0},
        compiler_params=_cparams("parallel"),
        name="outproj",
    )(*xa_args, xb, w, w, hh, mods)


def _ffn_kernel(h_ref, sh_ref, sc_ref, gate_ref, gain_ref, wg_ref, wv_ref, wo_ref, o_ref, u_scr, acc_scr):
    j = pl.program_id(1)

    @pl.when(j == 0)
    def _():
        u_scr[...] = _norm_mod(h_ref[...], gain_ref[...], sh_ref[0], sc_ref[0]).astype(BF16)
        acc_scr[...] = jnp.zeros(acc_scr.shape, F32)

    u = u_scr[...]
    g = _mm(u, wg_ref[...])
    val = _mm(u, wv_ref[...])
    act = (g * jax.nn.sigmoid(g) * val).astype(BF16)
    acc_scr[...] += _mm(act, wo_ref[...])

    @pl.when(j == pl.num_programs(1) - 1)
    def _():
        o_ref[...] = h_ref[...] + gate_ref[0] * acc_scr[...]


def _ffn(hh, mods, gain, w_in, w_out, layer, n_tiles, dims):
    r, d = hh.shape
    tm = dims["tm"]
    hid = w_out.shape[1]
    th = 512 if hid % 512 == 0 else 256
    nj = hid // th
    hspec, mod_spec = _row_specs(tm, d, dims["tpb"], dims["b"])
    in_place = n_tiles * tm == r
    return pl.pallas_call(
        _ffn_kernel,
        grid=(n_tiles, nj),
        in_specs=[hspec, mod_spec(3), mod_spec(4), mod_spec(5), pl.BlockSpec((1, d), lambda i, j: (0, 0)),
                  pl.BlockSpec((None, d, th), lambda i, j: (layer, 0, j)),
                  pl.BlockSpec((None, d, th), lambda i, j: (layer, 0, nj + j)),
                  pl.BlockSpec((None, th, d), lambda i, j: (layer, j, 0))],
        out_specs=hspec,
        out_shape=jax.ShapeDtypeStruct((n_tiles * tm, d), F32),
        input_output_aliases={0: 0} if in_place else {},
        scratch_shapes=[pltpu.VMEM((tm, d), BF16), pltpu.VMEM((tm, d), F32)],
        compiler_params=_cparams("parallel", "arbitrary"),
        name="ffn",
    )(hh, mods, mods, mods, gain, w_in, w_in, w_out)


def _rope_tables(dims):
    s, b, ctx = dims["s"], dims["b"], dims["ctx"]
    t = np.arange(s)
    n_freq = DIFF_DIM // 4
    inv = (ROPE_BASE ** (-np.arange(n_freq, dtype=np.float32) / n_freq)).astype(np.float32)
    lane = np.arange(LANES) % DIFF_DIM
    chunk = lane // n_freq
    pos = np.where((chunk // 2)[None, :] == 0, (t // GRID_W)[:, None], (t % GRID_W)[:, None]).astype(np.float32)
    ang = pos * inv[lane % n_freq][None, :]
    cos, sin = np.cos(ang), np.sin(ang)
    s_lo = np.where((chunk % 2)[None, :] == 1, sin, 0.0)
    s_hi = np.where((chunk % 2)[None, :] == 0, -sin, 0.0)
    pad = b * ctx

    def full(tab, fill):
        return jnp.asarray(np.concatenate([np.tile(tab, (b, 1)), np.full((pad, LANES), fill)], axis=0), dtype=F32)

    return full(cos, 1.0), full(s_lo, 0.0), full(s_hi, 0.0)


def kernel(x, c, ctx, c_ctx, w_mod, b_mod, norm_gain, w_ffn_in, w_ffn_out, ev_w_in, ev_w_out, ev_qk_gain,
           ev_lambda, ev_subln_gain, od_w_in, od_w_out, od_qk_gain, od_conv_w, od_rpb):
    b, s, d = x.shape
    n_ctx = ctx.shape[1]
    depth = w_mod.shape[0]
    tm = b * n_ctx
    r = b * s + b * n_ctx
    assert s % tm == 0 and s % GRID_W == 0
    ev_v = ev_w_out.shape[1] - FOURIER_GROUPS * FOURIER_DIM
    dims = dict(b=b, s=s, ctx=n_ctx, r=r, tm=tm, tm_in=min(1024, s), tn=512, tpb=s // tm,
                ev_q=ev_v // DIFF_V * 2 * DIFF_DIM, ev_v=ev_v, fw=FOURIER_GROUPS * FOURIER_DIM,
                cw=od_conv_w.shape[2], nw=od_rpb.shape[1] * NA_DIM,
                tq=min(512, s), tk=min(1024, s))
    n_all, n_lat = r // tm, b * s // tm

    hh = jnp.concatenate([x.reshape(b * s, d), ctx.reshape(b * n_ctx, d)], axis=0)
    crow = jnp.concatenate([c, c_ctx[None, :], jnp.zeros((8 - b - 1, d), F32)], axis=0)
    mod_first = _mod_vectors(crow, w_mod, b_mod, 1)
    mod_later = None
    rope_tabs = _rope_tables(dims)
    ev_w_in = ev_w_in.astype(BF16)
    later_ws = None
    tn = dims["tn"]

    for i in range(depth):
        last = i == depth - 1
        n_out = n_lat if last else n_all
        mods = (mod_first[0] if i == 0 else mod_later[i - 1]).reshape(8 * N_MOD, 1, d)
        gain1 = norm_gain[i, 0].reshape(1, d)
        gain2 = norm_gain[i, 1].reshape(1, d)
        j = i // 2
        if i % 2 == 0:
            lam_init = 0.8 - 0.6 * math.exp(-0.3 * i)
            gq = jnp.tile(ev_qk_gain[j, 0], DIFF_V // DIFF_DIM).reshape(1, DIFF_V)
            gk = jnp.tile(ev_qk_gain[j, 1], tn // DIFF_DIM).reshape(1, tn)
            q, kr, v, f = _inproj_even(hh, mods, gain1, ev_w_in, j, gk, rope_tabs, dims)
            subln = ev_subln_gain[j].reshape(1, DIFF_V)
            first = later_ws is None
            o, cast, side_mods = _diff_attn_latent(
                ev_lambda[j], subln, gq, q, rope_tabs, kr, v, dims, lam_init,
                cast_ws=(w_ffn_in, w_ffn_out, ev_w_out, od_w_in, od_w_out) if first else (),
                mod_args=(crow, w_mod, b_mod, 1) if first and depth > 1 else None)
            if first:
                later_ws, mod_later = cast, side_mods
                w_ffn_in, w_ffn_out, ev_w_out, od_w_in, od_w_out = cast
            fo = _fourier_latent(f, dims)
            if not last:
                o = _diff_attn_ctx(ev_lambda[j], subln, gq, q, kr, v, o, dims, lam_init)
                fo = _fourier_ctx(f, fo, dims)
            hh = _outproj(o, fo, ev_w_out, j, hh, mods, n_out, dims)
        else:
            gq = jnp.tile(od_qk_gain[j, 0], tn // NA_DIM).reshape(1, tn)
            gk = jnp.tile(od_qk_gain[j, 1], tn // NA_DIM).reshape(1, tn)
            cg, q, k, v = _inproj_odd(hh, mods, gain1, od_w_in, j, gq, gk, dims)
            o = _na_latent(q, k, v, _na_bias(od_rpb[j], s // GRID_W), dims)
            if not last:
                o = _na_ctx(q, k, v, o, dims)
            hh = _outproj(cg, o, od_w_out, j, hh, mods, n_out, dims, conv_w=od_conv_w[j])
        hh = _ffn(hh, mods, gain2, w_ffn_in, w_ffn_out, i, n_out, dims)

    return hh.reshape(b, s, d)
```

```python
import functools
import math

import jax
import jax.numpy as jnp
import numpy as np
from jax import lax
from jax.experimental import pallas as pl
from jax.experimental.pallas import tpu as pltpu

GRID_W = 64
EPS = 1e-6
N_MOD = 6

DIFF_HEADS = 12
DIFF_DIM = 64
DIFF_V = 2 * DIFF_DIM
FOURIER_GROUPS = 4
FOURIER_DIM = 128
ROPE_BASE = 10000.0

CONV_WIDTH = 1024
CONV_K = 3
NA_HEADS = 8
NA_DIM = 128
NA_WIN_ROWS = 8
NA_WIN_COLS = 16

F32 = jnp.float32
BF16 = jnp.bfloat16
LOG2E = 1.4426950408889634
NEG_BIG = -1e30
LANES = 128
MXU_DIM = 256
VMEM_LIMIT = 56 * 1024 * 1024
ATTN_SCORE_LEAD = 1


def _cparams(*sem):
    return pltpu.CompilerParams(dimension_semantics=sem, vmem_limit_bytes=VMEM_LIMIT)


def _mm(a, b):
    return jnp.dot(a, b, preferred_element_type=F32)


def _mm_nt(a, b):
    return lax.dot_general(a, b, (((1,), (1,)), ((), ())), preferred_element_type=F32)


def _tile_lanes(x, n):
    return x if n == 1 else jnp.concatenate([x] * n, axis=1)


def _mod_kernel(c_ref, w_ref, b_ref, o_ref):
    c = c_ref[...]
    s = (c * jax.nn.sigmoid(c)).astype(BF16)
    o_ref[0] = _mm(s, w_ref[0].astype(BF16)) + b_ref[0]


def _mod_side_specs(w_mod, first_layer, n_steps, step_of):
    depth, d, nm = w_mod.shape
    n_layers = depth - first_layer
    cw = LANES * pl.cdiv(n_layers * nm // LANES, n_steps)
    while nm % cw:
        cw += LANES
    per_layer = nm // cw
    last = n_layers * per_layer - 1

    def blk(*idx):
        t = jnp.minimum(step_of(*idx), last)
        return t // per_layer, t % per_layer

    return [pl.BlockSpec((8, d), lambda *idx: (0, 0)),
            pl.BlockSpec((1, d, cw), lambda *idx: (first_layer + blk(*idx)[0], 0, blk(*idx)[1])),
            pl.BlockSpec((1, 1, cw), lambda *idx: (first_layer + blk(*idx)[0], 0, blk(*idx)[1])),
            pl.BlockSpec((1, 8, cw), lambda *idx: (blk(*idx)[0], 0, blk(*idx)[1]))]


def _mod_vectors(crow, w_mod, b_mod, n_layers):
    depth, d, nm = w_mod.shape
    tn = 1024 if nm % 1024 == 0 else 512
    return pl.pallas_call(
        _mod_kernel,
        grid=(n_layers, nm // tn),
        in_specs=[
            pl.BlockSpec((8, d), lambda l, j: (0, 0)),
            pl.BlockSpec((1, d, tn), lambda l, j: (l, 0, j)),
            pl.BlockSpec((1, 1, tn), lambda l, j: (l, 0, j)),
        ],
        out_specs=pl.BlockSpec((1, 8, tn), lambda l, j: (l, 0, j)),
        out_shape=jax.ShapeDtypeStruct((n_layers, 8, nm), F32),
        compiler_params=_cparams("parallel", "parallel"),
        name="mod_vectors",
    )(crow, w_mod, b_mod.reshape(depth, 1, nm))


def _norm_mod(x, gain, shift, scale):
    r = lax.rsqrt(jnp.mean(x * x, axis=-1, keepdims=True) + EPS)
    return (x * r) * (gain * (1.0 + scale)) + shift


def _group_rms(x, ones_bd, gain, group):
    x2 = (x * x).astype(BF16)
    w = ones_bd.shape[0]
    parts = [_mm(x2[:, c:c + w], ones_bd) for c in range(0, x.shape[1], w)]
    ms = (parts[0] if len(parts) == 1 else jnp.concatenate(parts, axis=1)) * (1.0 / group)
    return x * lax.rsqrt(ms + EPS) * gain


def _rope(x, cos, s_lo, s_hi):
    q = DIFF_DIM // 4
    outs = []
    for c in range(0, x.shape[1], LANES):
        xs = x[:, c:c + LANES]
        outs.append(xs * cos + pltpu.roll(xs, q, 1) * s_lo + pltpu.roll(xs, LANES - q, 1) * s_hi)
    return jnp.concatenate(outs, axis=1)


def _inproj_even_kernel(h_ref, hc_ref, sh_ref, sc_ref, gain_ref, w_ref, gk_ref, ones_ref, cos_ref, slo_ref, shi_ref,
                        q_ref, kr_ref, v_ref, f_ref, a_scr, *, nq, nv, n_lat_tiles):
    i, j = pl.program_id(0), pl.program_id(1)

    @pl.when(jnp.logical_and(j == 0, i < n_lat_tiles))
    def _():
        a_scr[...] = _norm_mod(h_ref[...], gain_ref[...], sh_ref[0], sc_ref[0]).astype(BF16)

    @pl.when(jnp.logical_and(j == 0, i >= n_lat_tiles))
    def _():
        a_scr[0:hc_ref.shape[0], :] = _norm_mod(hc_ref[...], gain_ref[...], sh_ref[0], sc_ref[0]).astype(BF16)

    acc = _mm(a_scr[...], w_ref[...])

    @pl.when(j < nq)
    def _():
        q_ref[...] = acc.astype(BF16)

    @pl.when(jnp.logical_and(j >= nq, j < 2 * nq))
    def _():
        kn = _group_rms(acc, ones_ref[...], gk_ref[...], DIFF_DIM)
        kr_ref[...] = _rope(kn, cos_ref[...], slo_ref[...], shi_ref[...]).astype(BF16)

    @pl.when(jnp.logical_and(j >= 2 * nq, j < 2 * nq + nv))
    def _():
        v_ref[...] = acc.astype(BF16)

    @pl.when(j >= 2 * nq + nv)
    def _():
        f_ref[...] = acc.astype(BF16)


def _inproj_odd_kernel(h_ref, sh_ref, sc_ref, gain_ref, w_ref, gq_ref, gk_ref, ones_ref,
                       cg_ref, q_ref, k_ref, v_ref, a_scr, *, ncg, nh, qscale):
    j = pl.program_id(1)

    @pl.when(j == 0)
    def _():
        a_scr[...] = _norm_mod(h_ref[...], gain_ref[...], sh_ref[0], sc_ref[0]).astype(BF16)

    acc = _mm(a_scr[...], w_ref[...])

    @pl.when(j < ncg)
    def _():
        cg_ref[...] = acc.astype(BF16)

    @pl.when(jnp.logical_and(j >= ncg, j < ncg + nh))
    def _():
        q_ref[...] = (_group_rms(acc, ones_ref[...], gq_ref[...], NA_DIM) * qscale).astype(BF16)

    @pl.when(jnp.logical_and(j >= ncg + nh, j < ncg + 2 * nh))
    def _():
        k_ref[...] = _group_rms(acc, ones_ref[...], gk_ref[...], NA_DIM).astype(BF16)

    @pl.when(j >= ncg + 2 * nh)
    def _():
        v_ref[...] = acc.astype(BF16)


def _row_specs(tm, d, tiles_per_batch, n_batch):
    def mod_spec(k):
        return pl.BlockSpec((1, 1, d), lambda i, j: (jnp.minimum(i // tiles_per_batch, n_batch) * N_MOD + k, 0, 0))

    return pl.BlockSpec((tm, d), lambda i, j: (i, 0)), mod_spec


def _clamped(tm, tn, lo, n):
    return pl.BlockSpec((tm, tn), lambda i, j: (i, jnp.clip(j - lo, 0, n - 1)))


def _ones_blockdiag(group, size=MXU_DIM):
    idx = np.arange(size) // group
    return jnp.asarray((idx[:, None] == idx[None, :]).astype(np.float32), dtype=BF16)


def _inproj_even(hh, mods, gain, w, layer, gk, rope_tabs, dims, ctx_rows=None):
    d = hh.shape[1]
    r = dims["r"]
    tm, tn = dims["tm_in"], dims["tn"]
    n_lat_tiles = dims["b"] * dims["s"] // tm if ctx_rows is not None else pl.cdiv(r, tm)
    if ctx_rows is None:
        ctx_rows = jnp.zeros((8, d), F32)
    assert ctx_rows.shape[0] <= tm
    ev_q, ev_v, fw = dims["ev_q"], dims["ev_v"], dims["fw"]
    nq, nv, nf = ev_q // tn, ev_v // tn, fw // tn
    nj = 2 * nq + nv + nf
    hspec, mod_spec = _row_specs(tm, d, dims["s"] // tm, dims["b"])
    const = lambda shape: pl.BlockSpec(shape, lambda i, j: (0,) * len(shape))
    tab = pl.BlockSpec((tm, LANES), lambda i, j: (i, 0))
    kern = functools.partial(_inproj_even_kernel, nq=nq, nv=nv, n_lat_tiles=n_lat_tiles)
    return pl.pallas_call(
        kern,
        grid=(pl.cdiv(r, tm), nj),
        in_specs=[pl.BlockSpec((tm, d), lambda i, j: (jnp.minimum(i, n_lat_tiles - 1), 0)),
                  const(ctx_rows.shape),
                  mod_spec(0), mod_spec(1), const((1, d)),
                  pl.BlockSpec((None, d, tn), lambda i, j: (layer, 0, j)),
                  const((1, tn)), const((MXU_DIM, MXU_DIM)), tab, tab, tab],
        out_specs=[_clamped(tm, tn, 0, nq), _clamped(tm, tn, nq, nq),
                   _clamped(tm, tn, 2 * nq, nv), _clamped(tm, tn, 2 * nq + nv, nf)],
        out_shape=[jax.ShapeDtypeStruct((r, ev_q), BF16)] * 2
        + [jax.ShapeDtypeStruct((r, ev_v), BF16), jax.ShapeDtypeStruct((r, fw), BF16)],
        scratch_shapes=[pltpu.VMEM((tm, d), BF16)],
        compiler_params=_cparams("parallel", "arbitrary"),
        name="inproj_even",
    )(hh, ctx_rows, mods, mods, gain, w, gk, _ones_blockdiag(DIFF_DIM), *rope_tabs)


def _inproj_odd(hh, mods, gain, w, layer, gq, gk, dims):
    r, d = hh.shape
    tm, tn = dims["tm_in"], dims["tn"]
    cw, nw = dims["cw"], dims["nw"]
    ncg, nh = 3 * cw // tn, nw // tn
    nj = ncg + 3 * nh
    hspec, mod_spec = _row_specs(tm, d, dims["s"] // tm, dims["b"])
    const = lambda shape: pl.BlockSpec(shape, lambda i, j: (0,) * len(shape))
    kern = functools.partial(_inproj_odd_kernel, ncg=ncg, nh=nh, qscale=NA_DIM ** -0.5 * LOG2E)
    return pl.pallas_call(
        kern,
        grid=(pl.cdiv(r, tm), nj),
        in_specs=[hspec, mod_spec(0), mod_spec(1), const((1, d)),
                  pl.BlockSpec((None, d, tn), lambda i, j: (layer, 0, j)),
                  const((1, tn)), const((1, tn)), const((MXU_DIM, MXU_DIM))],
        out_specs=[_clamped(tm, tn, 0, ncg), _clamped(tm, tn, ncg, nh), _clamped(tm, tn, ncg + nh, nh),
                   _clamped(tm, tn, ncg + 2 * nh, nh)],
        out_shape=[jax.ShapeDtypeStruct((r, 3 * cw), BF16)] + [jax.ShapeDtypeStruct((r, nw), BF16)] * 3,
        scratch_shapes=[pltpu.VMEM((tm, d), BF16)],
        compiler_params=_cparams("parallel", "arbitrary"),
        name="inproj_odd",
    )(hh, mods, mods, gain, w, gq, gk, _ones_blockdiag(NA_DIM))


def _split_components(q):
    lane = lax.broadcasted_iota(jnp.int32, q.shape, 1)
    zero = jnp.zeros_like(q)
    return jnp.concatenate([jnp.where(lane < DIFF_DIM, q, zero), jnp.where(lane >= DIFF_DIM, q, zero)], axis=0)


def _online_softmax(chunks):
    lead = ATTN_SCORE_LEAD
    rows = chunks[0][0].shape[0]
    scores = [_mm_nt(q2, kk) for q2, kk, _ in chunks[:lead]]
    m = jnp.full((rows, LANES), NEG_BIG, F32)
    acc = jnp.zeros((rows, 2 * DIFF_V), F32)
    for c, (_, _, vv) in enumerate(chunks):
        if c + lead < len(chunks):
            scores.append(_mm_nt(chunks[c + lead][0], chunks[c + lead][1]))
        s = scores[c]
        m_new = jnp.maximum(m, jnp.max(s, axis=-1, keepdims=True))
        alpha = jnp.exp2(m - m_new)
        p = jnp.exp2((s - _tile_lanes(m_new, s.shape[1] // LANES)).astype(BF16))
        v_ext = jnp.concatenate([vv, jnp.ones_like(vv)], axis=1)
        acc = _tile_lanes(alpha, 2) * acc + _mm(p, v_ext)
        m = m_new
    return acc


def _diff_attn_kernel(*refs, n_chunks, tk, lam_init, n_cast=0, with_mods=False):
    if n_chunks:
        (lam_ref, g_ref, gq_ref, ones_ref, q_ref, cos_ref, slo_ref, shi_ref,
         kc_ref, vc_ref, k_ref, v_ref) = refs[:12]
        n_side = n_cast + 3 * with_mods
        o_ref = refs[12 + n_side]
        for w_ref, wo_ref in zip(refs[12:12 + n_cast], refs[13 + n_side:]):
            wo_ref[...] = w_ref[...].astype(BF16)
        if with_mods:
            _mod_kernel(*refs[12 + n_cast:12 + n_side], refs[-1])
    else:
        lam_ref, g_ref, gq_ref, ones_ref, q_ref, kc_ref, vc_ref, o_ref = refs
    tq = q_ref.shape[0]
    qn = _group_rms(q_ref[...].astype(F32), ones_ref[...], gq_ref[...], DIFF_DIM) * (DIFF_DIM ** -0.5 * LOG2E)
    chunks = []
    if n_chunks:
        q2 = _split_components(_rope(qn, cos_ref[...], slo_ref[...], shi_ref[...]).astype(BF16))
        chunks = [(q2, k_ref[c * tk:(c + 1) * tk, :], v_ref[c * tk:(c + 1) * tk, :]) for c in range(n_chunks)]
    chunks.append((_split_components(qn.astype(BF16)), kc_ref[...], vc_ref[...]))
    acc = _online_softmax(chunks)

    lv = lam_ref[...]
    lam = (jnp.exp(jnp.sum(lv[0:1] * lv[1:2], axis=-1, keepdims=True))
           - jnp.exp(jnp.sum(lv[2:3] * lv[3:4], axis=-1, keepdims=True)) + lam_init)
    o2 = acc[:, :DIFF_V] / acc[:, DIFF_V:]
    o = o2[:tq] - lam * o2[tq:]
    o = o * lax.rsqrt(jnp.mean(o * o, axis=-1, keepdims=True) + EPS) * g_ref[...] * (1.0 - lam_init)
    o_ref[...] = o.astype(BF16)


def _cast_block_rows(rows, n_steps):
    rb = 16 * pl.cdiv(pl.cdiv(rows, 16), n_steps)
    while rows % rb:
        rb += 16
    return rb


def _diff_attn_latent(lam_vec, subln, gq, q, rope_tabs, kr, v, dims, lam_init, cast_ws=(), mod_args=None):
    b, s, ctx, r = dims["b"], dims["s"], dims["ctx"], dims["r"]
    tq, tk = dims["tq"], dims["tk"]
    nh = q.shape[1] // DIFF_V
    ctx_blk0 = b * s // ctx
    nq = s // tq
    kern = functools.partial(_diff_attn_kernel, n_chunks=s // tk, tk=tk, lam_init=lam_init, n_cast=len(cast_ws),
                             with_mods=mod_args is not None)
    step_of = lambda bi, h, qi: (bi * nh + h) * nq + qi
    w2d = [w.reshape(-1, w.shape[-1]) for w in cast_ws]
    cast_specs = []
    for w in w2d:
        rb = _cast_block_rows(w.shape[0], b * nh * nq)
        nblk = w.shape[0] // rb
        cast_specs.append(pl.BlockSpec(
            (rb, w.shape[1]), lambda bi, h, qi, nblk=nblk: (jnp.minimum(step_of(bi, h, qi), nblk - 1), 0)))
    mod_in_specs, mod_out_specs, mod_out_shapes, mod_in = [], [], [], []
    if mod_args is not None:
        crow, w_mod, b_mod, first_layer = mod_args
        depth, _, nm = w_mod.shape
        *mod_in_specs, mod_out = _mod_side_specs(w_mod, first_layer, b * nh * nq, step_of)
        mod_out_specs = [mod_out]
        mod_out_shapes = [jax.ShapeDtypeStruct((depth - first_layer, 8, nm), F32)]
        mod_in = [crow, w_mod, b_mod.reshape(depth, 1, nm)]
    const = lambda shape: pl.BlockSpec(shape, lambda bi, h, qi: (0, 0))
    tab = pl.BlockSpec((tq, LANES), lambda bi, h, qi: (bi * (s // tq) + qi, 0))
    qspec = pl.BlockSpec((tq, DIFF_V), lambda bi, h, qi: (bi * (s // tq) + qi, h))
    cspec = pl.BlockSpec((ctx, DIFF_V), lambda bi, h, qi: (ctx_blk0 + bi, h))
    kspec = pl.BlockSpec((s, DIFF_V), lambda bi, h, qi: (bi, h))
    out = pl.pallas_call(
        kern,
        grid=(b, nh, nq),
        in_specs=[const((4, DIFF_DIM)), const((1, DIFF_V)), const((1, DIFF_V)), const((DIFF_V, DIFF_V)),
                  qspec, tab, tab, tab, cspec, cspec, kspec, kspec] + cast_specs + mod_in_specs,
        out_specs=[qspec] + cast_specs + mod_out_specs,
        out_shape=[jax.ShapeDtypeStruct((r, nh * DIFF_V), BF16)]
        + [jax.ShapeDtypeStruct(w.shape, BF16) for w in w2d] + mod_out_shapes,
        compiler_params=_cparams("arbitrary", "arbitrary", "arbitrary"),
        name="diff_attn_latent",
    )(lam_vec, subln, gq, _ones_blockdiag(DIFF_DIM, DIFF_V), q, *rope_tabs, kr, v, kr, v, *w2d, *mod_in)
    casts = [o.reshape(w.shape) for o, w in zip(out[1:1 + len(w2d)], cast_ws)]
    return out[0], casts, (out[-1] if mod_args is not None else None)


def _diff_attn_ctx(lam_vec, subln, gq, q, kr, v, o_lat, dims, lam_init):
    b, s, ctx = dims["b"], dims["s"], dims["ctx"]
    nh = q.shape[1] // DIFF_V
    ctx_blk0 = b * s // ctx
    kern = functools.partial(_diff_attn_kernel, n_chunks=0, tk=0, lam_init=lam_init)
    const = lambda shape: pl.BlockSpec(shape, lambda bi, h: (0, 0))
    cspec = pl.BlockSpec((ctx, DIFF_V), lambda bi, h: (ctx_blk0 + bi, h))

    def wrapped(lam_ref, g_ref, gq_ref, ones_ref, q_ref, kc_ref, vc_ref, alias_ref, o_ref):
        del alias_ref
        kern(lam_ref, g_ref, gq_ref, ones_ref, q_ref, kc_ref, vc_ref, o_ref)

    return pl.pallas_call(
        wrapped,
        grid=(b, nh),
        in_specs=[const((4, DIFF_DIM)), const((1, DIFF_V)), const((1, DIFF_V)), const((DIFF_V, DIFF_V)),
                  cspec, cspec, cspec, pl.BlockSpec(memory_space=pl.ANY)],
        out_specs=cspec,
        out_shape=jax.ShapeDtypeStruct(o_lat.shape, BF16),
        input_output_aliases={7: 0},
        compiler_params=_cparams("parallel", "parallel"),
        name="diff_attn_ctx",
    )(lam_vec, subln, gq, _ones_blockdiag(DIFF_DIM, DIFF_V), q, kr, v, o_lat)


def _dft_cs(n):
    k = np.arange(n)
    ang = 2.0 * np.pi * ((k[:, None] * k[None, :]) % n) / n
    return np.cos(ang), np.sin(ang)


def _channel_dft(groups, scale):
    c, s = _dft_cs(FOURIER_DIM)
    eye = np.eye(groups)
    return np.kron(eye, c) * scale, np.kron(eye, s) * scale


def _fourier_rows_kernel(x_ref, w_ref, tc_ref, ts_ref, y_ref, *, cg, fw, nr):
    for c in range(cg):
        y = _mm(w_ref[...], x_ref[:, c * fw:(c + 1) * fw])
        yr, yi = y[:nr], y[nr:]
        tc = _tile_lanes(tc_ref[c], fw // LANES)
        ts = _tile_lanes(ts_ref[c], fw // LANES)
        y_ref[0, 0, :, c * fw:(c + 1) * fw] = (yr * tc + yi * ts).astype(BF16)
        y_ref[0, 1, :, c * fw:(c + 1) * fw] = (yi * tc - yr * ts).astype(BF16)


def _fourier_cols_kernel(y_ref, w2_ref, wc_ref, ws_ref, o_ref, *, ag, fw, nc):
    for a in range(ag):
        yy = jnp.concatenate([y_ref[0, 0, a], y_ref[0, 1, a]], axis=0)
        z = _mm(w2_ref[...], yy)
        out = _mm(z[:nc].astype(BF16), wc_ref[...]) + _mm(z[nc:].astype(BF16), ws_ref[...])
        o_ref[:, a * fw:(a + 1) * fw] = out.astype(BF16)


def _fourier_dense_kernel(x_ref, wc_ref, ws_ref, cn_ref, sn_ref, alias_ref, o_ref):
    del alias_ref
    x = x_ref[...]
    gc = _mm(x, wc_ref[...]).astype(BF16)
    gs = _mm(x, ws_ref[...]).astype(BF16)
    o_ref[...] = (_mm(cn_ref[...], gc) - _mm(sn_ref[...], gs)).astype(BF16)


def _fourier_latent(f, dims):
    b, s, r = dims["b"], dims["s"], dims["r"]
    fw = f.shape[1]
    groups = fw // FOURIER_DIM
    nc = GRID_W
    nr = s // nc
    cg = 4
    ag = 8
    c1, s1 = _dft_cs(nr)
    w1 = jnp.asarray(np.concatenate([c1, -s1], axis=0), dtype=BF16)
    ang = 2.0 * np.pi * (np.arange(nc)[:, None] * np.arange(nr)[None, :]) / s
    tcos = jnp.asarray(np.repeat(np.cos(ang)[:, :, None], LANES, axis=2), dtype=F32)
    tsin = jnp.asarray(np.repeat(np.sin(ang)[:, :, None], LANES, axis=2), dtype=F32)
    x2d = f.reshape(r // nc, nc * fw)
    y = pl.pallas_call(
        functools.partial(_fourier_rows_kernel, cg=cg, fw=fw, nr=nr),
        grid=(b, nc // cg),
        in_specs=[pl.BlockSpec((nr, cg * fw), lambda bi, j: (bi, j)),
                  pl.BlockSpec((2 * nr, nr), lambda bi, j: (0, 0)),
                  pl.BlockSpec((cg, nr, LANES), lambda bi, j: (j, 0, 0)),
                  pl.BlockSpec((cg, nr, LANES), lambda bi, j: (j, 0, 0))],
        out_specs=pl.BlockSpec((1, 2, nr, cg * fw), lambda bi, j: (bi, 0, 0, j)),
        out_shape=jax.ShapeDtypeStruct((b, 2, nr, nc * fw), BF16),
        compiler_params=_cparams("parallel", "parallel"),
        name="fourier_rows",
    )(x2d, w1, tcos, tsin)
    c2, s2 = _dft_cs(nc)
    w2 = jnp.asarray(np.block([[c2, s2], [-s2, c2]]), dtype=BF16)
    wc, ws = _channel_dft(groups, 1.0 / math.sqrt(s * FOURIER_DIM))
    out = pl.pallas_call(
        functools.partial(_fourier_cols_kernel, ag=ag, fw=fw, nc=nc),
        grid=(b, nr // ag),
        in_specs=[pl.BlockSpec((1, 2, ag, nc, fw), lambda bi, j: (bi, 0, j, 0, 0)),
                  pl.BlockSpec((2 * nc, 2 * nc), lambda bi, j: (0, 0)),
                  pl.BlockSpec((fw, fw), lambda bi, j: (0, 0)),
                  pl.BlockSpec((fw, fw), lambda bi, j: (0, 0))],
        out_specs=pl.BlockSpec((nc, ag * fw), lambda bi, j: (bi, j)),
        out_shape=jax.ShapeDtypeStruct((r // nr, nr * fw), BF16),
        compiler_params=_cparams("parallel", "parallel"),
        name="fourier_cols",
    )(y.reshape(b, 2, nr, nc, fw), w2, jnp.asarray(wc, dtype=BF16), jnp.asarray(ws, dtype=BF16))
    return out.reshape(r, fw)


def _fourier_ctx(f, fo, dims):
    b, s, ctx = dims["b"], dims["s"], dims["ctx"]
    fw = f.shape[1]
    groups = fw // FOURIER_DIM
    wc, ws = _channel_dft(groups, 1.0 / math.sqrt(ctx * FOURIER_DIM))
    cn, sn = _dft_cs(ctx)
    blk0 = b * s // ctx
    rows = pl.BlockSpec((ctx, fw), lambda bi: (blk0 + bi, 0))
    const = lambda shape: pl.BlockSpec(shape, lambda bi: (0, 0))
    return pl.pallas_call(
        _fourier_dense_kernel,
        grid=(b,),
        in_specs=[rows, const((fw, fw)), const((fw, fw)), const((ctx, ctx)), const((ctx, ctx)),
                  pl.BlockSpec(memory_space=pl.ANY)],
        out_specs=rows,
        out_shape=jax.ShapeDtypeStruct(fo.shape, BF16),
        input_output_aliases={5: 0},
        compiler_params=_cparams("parallel"),
        name="fourier_ctx",
    )(f, jnp.asarray(wc, dtype=BF16), jnp.asarray(ws, dtype=BF16),
      jnp.asarray(cn, dtype=BF16), jnp.asarray(sn, dtype=BF16), fo)


def _gated_conv(gb_ref, gc_ref, hh_ref, gcp_ref, hhp_ref, gcn_ref, hhn_ref, w_ref, *, n_lat_tiles, s, ctx):
    i = pl.program_id(0)
    tm = gb_ref.shape[0]
    is_lat = i < n_lat_tiles
    seq = jnp.where(is_lat, s, ctx)
    off = lax.rem(i * tm, seq)
    u = gc_ref[...].astype(F32) * hh_ref[...].astype(F32)
    halo = gcp_ref.shape[0]
    u_prev = gcp_ref[halo - 1:halo, :].astype(F32) * hhp_ref[halo - 1:halo, :].astype(F32)
    u_next = gcn_ref[0:1, :].astype(F32) * hhn_ref[0:1, :].astype(F32)
    u_prev = jnp.where(off == 0, 0.0, u_prev)
    u_next = jnp.where(lax.rem(off + tm, seq) == 0, 0.0, u_next)
    row = lax.broadcasted_iota(jnp.int32, (tm, 1), 0)
    dn = jnp.where(row == 0, u_prev, pltpu.roll(u, 1, 0))
    up = jnp.where(row == tm - 1, u_next, pltpu.roll(u, tm - 1, 0))
    if ctx < tm:
        inner_start = functools.reduce(jnp.logical_or, [row == k * ctx for k in range(1, tm // ctx)])
        inner_end = functools.reduce(jnp.logical_or, [row == k * ctx - 1 for k in range(1, tm // ctx)])
        is_ctx = jnp.logical_not(is_lat)
        dn = jnp.where(jnp.logical_and(is_ctx, inner_start), 0.0, dn)
        up = jnp.where(jnp.logical_and(is_ctx, inner_end), 0.0, up)
    w = w_ref[...]
    y = dn * w[0:1] + u * w[1:2] + up * w[2:3]
    return gb_ref[...].astype(F32) * y


def _conv_specs(cw, tm, r):
    halo = 16
    nhb = tm // halo
    last = r // halo - 1
    cur = lambda off: pl.BlockSpec((tm, cw), lambda i: (i, off))
    prev = lambda off: pl.BlockSpec((halo, cw), lambda i: (jnp.maximum(i * nhb - 1, 0), off))
    nxt = lambda off: pl.BlockSpec((halo, cw), lambda i: (jnp.minimum((i + 1) * nhb, last), off))
    return [cur(0), cur(1), cur(2), prev(1), prev(2), nxt(1), nxt(2), pl.BlockSpec((CONV_K, cw), lambda i: (0, 0))]


NA_BLOCK_ROWS = 8
NA_KEY_ROWS = 16
NA_BLOCKS_PER_TRIP = 16


def _na_plan(rows):
    wr = min(NA_WIN_ROWS, rows)
    starts, variants, keys = [], [], {}
    for jb in range(rows // NA_BLOCK_ROWS):
        r0 = jb * NA_BLOCK_ROWS
        ks = int(np.clip(r0 - wr // 2, 0, rows - NA_KEY_ROWS))
        rs = [int(np.clip(r0 + t - wr // 2, 0, rows - wr)) for t in range(NA_BLOCK_ROWS)]
        key = (r0 - ks, tuple(x - ks for x in rs))
        variants.append(keys.setdefault(key, len(keys)))
        starts.append(ks)
    return starts, variants, list(keys), wr


def _na_bias(rpb, rows):
    _, _, keys, wr = _na_plan(rows)
    col = np.arange(GRID_W)
    cs = np.clip(col - NA_WIN_COLS // 2, 0, GRID_W - NA_WIN_COLS)
    kc = np.arange(GRID_W)
    dc = kc[None, :] - col[:, None] + (NA_WIN_COLS - 1)
    col_ok = (kc[None, :] >= cs[:, None]) & (kc[None, :] < cs[:, None] + NA_WIN_COLS)
    n_dr, n_dc = 2 * NA_WIN_ROWS - 1, 2 * NA_WIN_COLS - 1
    col_sel = ((dc[None] == np.arange(n_dc)[:, None, None]) & col_ok[None]).astype(np.float32)
    t = np.arange(NA_BLOCK_ROWS)
    i = np.arange(NA_KEY_ROWS)
    row_sel, row_okv = [], []
    for r0_rel, rs_rel in keys:
        dr = i[None, :] - (r0_rel + t[:, None]) + (NA_WIN_ROWS - 1)
        rs_arr = np.asarray(rs_rel)
        row_ok = (i[None, :] >= rs_arr[:, None]) & (i[None, :] < rs_arr[:, None] + wr)
        row_sel.append(((dr[None] == np.arange(n_dr)[:, None, None]) & row_ok[None]).astype(np.float32))
        row_okv.append(row_ok)
    row_sel = np.stack(row_sel)
    ok = (jnp.asarray(np.stack(row_okv))[:, None, :, None, :, None]
          & jnp.asarray(col_ok)[None, None, None, :, None, :])
    hi = lax.Precision.HIGHEST
    col_exp = jnp.einsum("hrd,dck->hrck", rpb.astype(F32) * LOG2E, jnp.asarray(col_sel), precision=hi)
    rs = jnp.asarray(row_sel)
    vals = sum(rs[:, r][:, None, :, None, :, None] * col_exp[None, :, r][:, :, None, :, None, :]
               for r in range(n_dr))
    vals = jnp.where(ok, vals, NEG_BIG)
    return vals.reshape(len(keys), rpb.shape[0], NA_BLOCK_ROWS * GRID_W, NA_KEY_ROWS * GRID_W)


def _na_kernel(q_ref, k_ref, v_ref, kc_ref, vc_ref, bias_ref, o_ref, *, starts, variants):
    qb = NA_BLOCK_ROWS * GRID_W
    kb = NA_KEY_ROWS * GRID_W

    def lookup(table, jb):
        out = jnp.int32(table[0])
        for idx in range(1, len(table)):
            out = jnp.where(jb == idx, jnp.int32(table[idx]), out)
        return out

    def block(jb):
        q0 = pl.multiple_of(jb * qb, qb)
        k0 = pl.multiple_of(lookup(starts, jb) * GRID_W, GRID_W)
        q = q_ref[pl.ds(q0, qb), :]
        s_nb = _mm_nt(q, k_ref[pl.ds(k0, kb), :]) + bias_ref[lookup(variants, jb), 0]
        s_cx = _mm_nt(q, kc_ref[...])
        m = jnp.maximum(jnp.max(s_nb, axis=-1, keepdims=True), jnp.max(s_cx, axis=-1, keepdims=True))
        p_nb = jnp.exp2((s_nb - m).astype(BF16))
        p_cx = jnp.exp2((s_cx - m).astype(BF16))
        v_nb, v_cx = v_ref[pl.ds(k0, kb), :], vc_ref[...]
        o = (_mm(p_nb, jnp.concatenate([v_nb, jnp.ones_like(v_nb)], axis=1))
             + _mm(p_cx, jnp.concatenate([v_cx, jnp.ones_like(v_cx)], axis=1)))
        o_ref[pl.ds(q0, qb), :] = (o[:, :NA_DIM] / o[:, NA_DIM:]).astype(BF16)

    per_trip = math.gcd(NA_BLOCKS_PER_TRIP, len(starts))

    def body(t, carry):
        for u in range(per_trip):
            block(t * per_trip + u)
        return carry

    lax.fori_loop(0, len(starts) // per_trip, body, 0)


def _na_ctx_kernel(q_ref, kc_ref, vc_ref, alias_ref, o_ref):
    del alias_ref
    s = _mm_nt(q_ref[...], kc_ref[...])
    p = jnp.exp2(s - jnp.max(s, axis=-1, keepdims=True))
    o = _mm(p.astype(BF16), vc_ref[...]) / jnp.sum(p, axis=-1, keepdims=True)
    o_ref[...] = o.astype(BF16)


def _na_latent(q, k, v, bias, dims):
    b, s, ctx, r = dims["b"], dims["s"], dims["ctx"], dims["r"]
    nh = q.shape[1] // NA_DIM
    starts, variants, keys, _ = _na_plan(s // GRID_W)
    blk0 = b * s // ctx
    lat = pl.BlockSpec((s, NA_DIM), lambda bi, h: (bi, h))
    cx = pl.BlockSpec((ctx, NA_DIM), lambda bi, h: (blk0 + bi, h))
    nvar = len(keys)
    return pl.pallas_call(
        functools.partial(_na_kernel, starts=tuple(starts), variants=tuple(variants)),
        grid=(b, nh),
        in_specs=[lat, lat, lat, cx, cx,
                  pl.BlockSpec((nvar, 1) + bias.shape[2:], lambda bi, h: (0, h, 0, 0))],
        out_specs=lat,
        out_shape=jax.ShapeDtypeStruct((r, nh * NA_DIM), BF16),
        compiler_params=_cparams("parallel", "arbitrary"),
        name="na_latent",
    )(q, k, v, k, v, bias)


def _na_ctx(q, k, v, o_lat, dims):
    b, s, ctx = dims["b"], dims["s"], dims["ctx"]
    nh = q.shape[1] // NA_DIM
    blk0 = b * s // ctx
    cx = pl.BlockSpec((ctx, NA_DIM), lambda bi, h: (blk0 + bi, h))
    return pl.pallas_call(
        _na_ctx_kernel,
        grid=(b, nh),
        in_specs=[cx, cx, cx, pl.BlockSpec(memory_space=pl.ANY)],
        out_specs=cx,
        out_shape=jax.ShapeDtypeStruct(o_lat.shape, BF16),
        input_output_aliases={3: 0},
        compiler_params=_cparams("parallel", "parallel"),
        name="na_ctx",
    )(q, k, v, o_lat)


def _outproj_kernel(xa_ref, xb_ref, wa_ref, wb_ref, h_ref, gate_ref, o_ref):
    y = _mm(xa_ref[...], wa_ref[...]) + _mm(xb_ref[...], wb_ref[...])
    o_ref[...] = h_ref[...] + gate_ref[0] * y


def _outproj_split_kernel(xa_ref, xb_ref, wa_ref, wb_ref, h_ref, hc_ref, gate_ref, o_ref, *, n_lat_tiles):
    i = pl.program_id(0)
    y = gate_ref[0] * (_mm(xa_ref[...], wa_ref[...]) + _mm(xb_ref[...], wb_ref[...]))

    @pl.when(i < n_lat_tiles)
    def _():
        o_ref[...] = h_ref[...] + y

    @pl.when(i >= n_lat_tiles)
    def _():
        o_ref[...] = hc_ref[...] + y


def _outproj_conv_kernel(*refs, conv_args):
    conv_refs, (xb_ref, wa_ref, wb_ref, h_ref, gate_ref, o_ref) = refs[:8], refs[8:]
    xa = _gated_conv(*conv_refs, **conv_args).astype(BF16)
    y = _mm(xa, wa_ref[...]) + _mm(xb_ref[...], wb_ref[...])
    o_ref[...] = h_ref[...] + gate_ref[0] * y


def _outproj(xa, xb, w, layer, hh, mods, n_tiles, dims, conv_w=None, ctx_rows=None):
    d = hh.shape[1]
    r = dims["r"]
    tm = dims["tm"]
    if ctx_rows is not None:
        assert conv_w is None
        n_lat = hh.shape[0] // tm
        kern = functools.partial(_outproj_split_kernel, n_lat_tiles=n_lat)
        hspec, mod_spec = _row_specs(tm, d, dims["tpb"], dims["b"])
        one = lambda spec: pl.BlockSpec(spec.block_shape, lambda i: spec.index_map(i, 0))
        ka, kb = xa.shape[1], xb.shape[1]
        return pl.pallas_call(
            kern,
            grid=(n_tiles,),
            in_specs=[pl.BlockSpec((tm, ka), lambda i: (i, 0)), pl.BlockSpec((tm, kb), lambda i: (i, 0)),
                      pl.BlockSpec((None, ka, d), lambda i: (layer, 0, 0)),
                      pl.BlockSpec((None, kb, d), lambda i: (layer, ka // kb, 0)),
                      pl.BlockSpec((tm, d), lambda i: (jnp.minimum(i, n_lat - 1), 0)),
                      pl.BlockSpec((tm, d), lambda i: (jnp.maximum(i - n_lat, 0), 0)),
                      one(mod_spec(2))],
            out_specs=one(hspec),
            out_shape=jax.ShapeDtypeStruct((r, d), F32),
            compiler_params=_cparams("parallel"),
            name="outproj",
        )(xa, xb, w, w, hh, ctx_rows, mods)
    kb = xb.shape[1]
    ka = w.shape[1] - kb
    assert ka % kb == 0
    hspec, mod_spec = _row_specs(tm, d, dims["tpb"], dims["b"])
    one = lambda spec: pl.BlockSpec(spec.block_shape, lambda i: spec.index_map(i, 0))
    if conv_w is None:
        kern, xa_specs, xa_args = _outproj_kernel, [pl.BlockSpec((tm, ka), lambda i: (i, 0))], [xa]
    else:
        conv_args = dict(n_lat_tiles=dims["b"] * dims["tpb"], s=dims["s"], ctx=dims["ctx"])
        kern = functools.partial(_outproj_conv_kernel, conv_args=conv_args)
        xa_specs, xa_args = _conv_specs(ka, tm, r), [xa] * 7 + [conv_w]
    return pl.pallas_call(
        kern,
        grid=(n_tiles,),
        in_specs=xa_specs + [pl.BlockSpec((tm, kb), lambda i: (i, 0)),
                             pl.BlockSpec((None, ka, d), lambda i: (layer, 0, 0)),
                             pl.BlockSpec((None, kb, d), lambda i: (layer, ka // kb, 0)),
                             one(hspec), one(mod_spec(2))],
        out_specs=one(hspec),
        out_shape=jax.ShapeDtypeStruct((r, d), F32),
        input_output_aliases={len(xa_specs) + 3: 0},
        compiler_params=_cparams("parallel"),
        name="outproj",
    )(*xa_args, xb, w, w, hh, mods)


def _ffn_kernel(h_ref, sh_ref, sc_ref, gate_ref, gain_ref, wg_ref, wv_ref, wo_ref, o_ref, u_scr, acc_scr):
    j = pl.program_id(1)

    @pl.when(j == 0)
    def _():
        u_scr[...] = _norm_mod(h_ref[...], gain_ref[...], sh_ref[0], sc_ref[0]).astype(BF16)
        acc_scr[...] = jnp.zeros(acc_scr.shape, F32)

    u = u_scr[...]
    g = _mm(u, wg_ref[...])
    val = _mm(u, wv_ref[...])
    act = (g * jax.nn.sigmoid(g) * val).astype(BF16)
    acc_scr[...] += _mm(act, wo_ref[...])

    @pl.when(j == pl.num_programs(1) - 1)
    def _():
        o_ref[...] = h_ref[...] + gate_ref[0] * acc_scr[...]


def _ffn(hh, mods, gain, w_in, w_out, layer, n_tiles, dims):
    r, d = hh.shape
    tm = dims["tm"]
    hid = w_out.shape[1]
    th = 512 if hid % 512 == 0 else 256
    nj = hid // th
    hspec, mod_spec = _row_specs(tm, d, dims["tpb"], dims["b"])
    in_place = n_tiles * tm == r
    return pl.pallas_call(
        _ffn_kernel,
        grid=(n_tiles, nj),
        in_specs=[hspec, mod_spec(3), mod_spec(4), mod_spec(5), pl.BlockSpec((1, d), lambda i, j: (0, 0)),
                  pl.BlockSpec((None, d, th), lambda i, j: (layer, 0, j)),
                  pl.BlockSpec((None, d, th), lambda i, j: (layer, 0, nj + j)),
                  pl.BlockSpec((None, th, d), lambda i, j: (layer, j, 0))],
        out_specs=hspec,
        out_shape=jax.ShapeDtypeStruct((n_tiles * tm, d), F32),
        input_output_aliases={0: 0} if in_place else {},
        scratch_shapes=[pltpu.VMEM((tm, d), BF16), pltpu.VMEM((tm, d), F32)],
        compiler_params=_cparams("parallel", "arbitrary"),
        name="ffn",
    )(hh, mods, mods, mods, gain, w_in, w_in, w_out)


def _rope_tables(dims):
    s, b, ctx = dims["s"], dims["b"], dims["ctx"]
    t = np.arange(s)
    n_freq = DIFF_DIM // 4
    inv = (ROPE_BASE ** (-np.arange(n_freq, dtype=np.float32) / n_freq)).astype(np.float32)
    lane = np.arange(LANES) % DIFF_DIM
    chunk = lane // n_freq
    pos = np.where((chunk // 2)[None, :] == 0, (t // GRID_W)[:, None], (t % GRID_W)[:, None]).astype(np.float32)
    ang = pos * inv[lane % n_freq][None, :]
    cos, sin = np.cos(ang), np.sin(ang)
    s_lo = np.where((chunk % 2)[None, :] == 1, sin, 0.0)
    s_hi = np.where((chunk % 2)[None, :] == 0, -sin, 0.0)
    pad = b * ctx

    def full(tab, fill):
        return jnp.asarray(np.concatenate([np.tile(tab, (b, 1)), np.full((pad, LANES), fill)], axis=0), dtype=F32)

    return full(cos, 1.0), full(s_lo, 0.0), full(s_hi, 0.0)


def kernel(x, c, ctx, c_ctx, w_mod, b_mod, norm_gain, w_ffn_in, w_ffn_out, ev_w_in, ev_w_out, ev_qk_gain,
           ev_lambda, ev_subln_gain, od_w_in, od_w_out, od_qk_gain, od_conv_w, od_rpb):
    b, s, d = x.shape
    n_ctx = ctx.shape[1]
    depth = w_mod.shape[0]
    tm = b * n_ctx
    r = b * s + b * n_ctx
    assert s % tm == 0 and s % GRID_W == 0
    ev_v = ev_w_out.shape[1] - FOURIER_GROUPS * FOURIER_DIM
    dims = dict(b=b, s=s, ctx=n_ctx, r=r, tm=tm, tm_in=min(1024, s), tn=512, tpb=s // tm,
                ev_q=ev_v // DIFF_V * 2 * DIFF_DIM, ev_v=ev_v, fw=FOURIER_GROUPS * FOURIER_DIM,
                cw=od_conv_w.shape[2], nw=od_rpb.shape[1] * NA_DIM,
                tq=min(512, s), tk=min(1024, s))
    n_all, n_lat = r // tm, b * s // tm

    x_rows, ctx_rows = x.reshape(b * s, d), ctx.reshape(b * n_ctx, d)
    hh = None
    crow = jnp.concatenate([c, c_ctx[None, :], jnp.zeros((8 - b - 1, d), F32)], axis=0)
    mod_first = _mod_vectors(crow, w_mod, b_mod, 1)
    mod_later = None
    rope_tabs = _rope_tables(dims)
    ev_w_in = ev_w_in.astype(BF16)
    later_ws = None
    tn = dims["tn"]

    for i in range(depth):
        last = i == depth - 1
        n_out = n_lat if last else n_all
        mods = (mod_first[0] if i == 0 else mod_later[i - 1]).reshape(8 * N_MOD, 1, d)
        gain1 = norm_gain[i, 0].reshape(1, d)
        gain2 = norm_gain[i, 1].reshape(1, d)
        j = i // 2
        if i % 2 == 0:
            lam_init = 0.8 - 0.6 * math.exp(-0.3 * i)
            gq = jnp.tile(ev_qk_gain[j, 0], DIFF_V // DIFF_DIM).reshape(1, DIFF_V)
            gk = jnp.tile(ev_qk_gain[j, 1], tn // DIFF_DIM).reshape(1, tn)
            q, kr, v, f = _inproj_even(x_rows if i == 0 else hh, mods, gain1, ev_w_in, j, gk, rope_tabs, dims,
                                       ctx_rows=ctx_rows if i == 0 else None)
            subln = ev_subln_gain[j].reshape(1, DIFF_V)
            first = later_ws is None
            o, cast, side_mods = _diff_attn_latent(
                ev_lambda[j], subln, gq, q, rope_tabs, kr, v, dims, lam_init,
                cast_ws=(w_ffn_in, w_ffn_out, ev_w_out, od_w_in, od_w_out) if first else (),
                mod_args=(crow, w_mod, b_mod, 1) if first and depth > 1 else None)
            if first:
                later_ws, mod_later = cast, side_mods
                w_ffn_in, w_ffn_out, ev_w_out, od_w_in, od_w_out = cast
            fo = _fourier_latent(f, dims)
            if not last:
                o = _diff_attn_ctx(ev_lambda[j], subln, gq, q, kr, v, o, dims, lam_init)
                fo = _fourier_ctx(f, fo, dims)
            if i == 0 and not last:
                hh = _outproj(o, fo, ev_w_out, j, x_rows, mods, n_out, dims, ctx_rows=ctx_rows)
            else:
                hh = _outproj(o, fo, ev_w_out, j, hh, mods, n_out, dims)
        else:
            gq = jnp.tile(od_qk_gain[j, 0], tn // NA_DIM).reshape(1, tn)
            gk = jnp.tile(od_qk_gain[j, 1], tn // NA_DIM).reshape(1, tn)
            cg, q, k, v = _inproj_odd(hh, mods, gain1, od_w_in, j, gq, gk, dims)
            o = _na_latent(q, k, v, _na_bias(od_rpb[j], s // GRID_W), dims)
            if not last:
                o = _na_ctx(q, k, v, o, dims)
            hh = _outproj(cg, o, od_w_out, j, hh, mods, n_out, dims, conv_w=od_conv_w[j])
        hh = _ffn(hh, mods, gain2, w_ffn_in, w_ffn_out, i, n_out, dims)

    return hh.reshape(b, s, d)
```

```python
import functools
import math

import jax
import jax.numpy as jnp
import numpy as np
from jax import lax
from jax.experimental import pallas as pl
from jax.experimental.pallas import tpu as pltpu

GRID_W = 64
EPS = 1e-6
N_MOD = 6

DIFF_HEADS = 12
DIFF_DIM = 64
DIFF_V = 2 * DIFF_DIM
FOURIER_GROUPS = 4
FOURIER_DIM = 128
ROPE_BASE = 10000.0

CONV_WIDTH = 1024
CONV_K = 3
NA_HEADS = 8
NA_DIM = 128
NA_WIN_ROWS = 8
NA_WIN_COLS = 16

F32 = jnp.float32
BF16 = jnp.bfloat16
LOG2E = 1.4426950408889634
NEG_BIG = -1e30
LANES = 128
MXU_DIM = 256
VMEM_LIMIT = 56 * 1024 * 1024
ATTN_SCORE_LEAD = 1


def _cparams(*sem):
    return pltpu.CompilerParams(dimension_semantics=sem, vmem_limit_bytes=VMEM_LIMIT)


def _mm(a, b):
    return jnp.dot(a, b, preferred_element_type=F32)


def _mm_nt(a, b):
    return lax.dot_general(a, b, (((1,), (1,)), ((), ())), preferred_element_type=F32)


def _tile_lanes(x, n):
    return x if n == 1 else jnp.concatenate([x] * n, axis=1)


def _mod_kernel(c_ref, w_ref, b_ref, o_ref):
    c = c_ref[...]
    s = (c * jax.nn.sigmoid(c)).astype(BF16)
    o_ref[0] = _mm(s, w_ref[0].astype(BF16)) + b_ref[0]


def _mod_side_specs(w_mod, first_layer, n_steps, step_of):
    depth, d, nm = w_mod.shape
    n_layers = depth - first_layer
    cw = LANES * pl.cdiv(n_layers * nm // LANES, n_steps)
    while nm % cw:
        cw += LANES
    per_layer = nm // cw
    last = n_layers * per_layer - 1

    def blk(*idx):
        t = jnp.minimum(step_of(*idx), last)
        return t // per_layer, t % per_layer

    return [pl.BlockSpec((8, d), lambda *idx: (0, 0)),
            pl.BlockSpec((1, d, cw), lambda *idx: (first_layer + blk(*idx)[0], 0, blk(*idx)[1])),
            pl.BlockSpec((1, 1, cw), lambda *idx: (first_layer + blk(*idx)[0], 0, blk(*idx)[1])),
            pl.BlockSpec((1, 8, cw), lambda *idx: (blk(*idx)[0], 0, blk(*idx)[1]))]


def _mod_vectors(crow, w_mod, b_mod, n_layers):
    depth, d, nm = w_mod.shape
    tn = 1024 if nm % 1024 == 0 else 512
    return pl.pallas_call(
        _mod_kernel,
        grid=(n_layers, nm // tn),
        in_specs=[
            pl.BlockSpec((8, d), lambda l, j: (0, 0)),
            pl.BlockSpec((1, d, tn), lambda l, j: (l, 0, j)),
            pl.BlockSpec((1, 1, tn), lambda l, j: (l, 0, j)),
        ],
        out_specs=pl.BlockSpec((1, 8, tn), lambda l, j: (l, 0, j)),
        out_shape=jax.ShapeDtypeStruct((n_layers, 8, nm), F32),
        compiler_params=_cparams("parallel", "parallel"),
        name="mod_vectors",
    )(crow, w_mod, b_mod.reshape(depth, 1, nm))


def _norm_mod(x, gain, shift, scale):
    r = lax.rsqrt(jnp.mean(x * x, axis=-1, keepdims=True) + EPS)
    return (x * r) * (gain * (1.0 + scale)) + shift


def _group_rms(x, ones_bd, gain, group):
    x2 = (x * x).astype(BF16)
    w = ones_bd.shape[0]
    parts = [_mm(x2[:, c:c + w], ones_bd) for c in range(0, x.shape[1], w)]
    ms = (parts[0] if len(parts) == 1 else jnp.concatenate(parts, axis=1)) * (1.0 / group)
    return x * lax.rsqrt(ms + EPS) * gain


def _rope(x, cos, s_lo, s_hi):
    q = DIFF_DIM // 4
    outs = []
    for c in range(0, x.shape[1], LANES):
        xs = x[:, c:c + LANES]
        outs.append(xs * cos + pltpu.roll(xs, q, 1) * s_lo + pltpu.roll(xs, LANES - q, 1) * s_hi)
    return jnp.concatenate(outs, axis=1)


def _inproj_even_kernel(h_ref, hc_ref, sh_ref, sc_ref, gain_ref, w_ref, gk_ref, ones_ref, cos_ref, slo_ref, shi_ref,
                        q_ref, kr_ref, v_ref, f_ref, a_scr, *, nq, nv, n_lat_tiles):
    i, j = pl.program_id(0), pl.program_id(1)

    @pl.when(jnp.logical_and(j == 0, i < n_lat_tiles))
    def _():
        a_scr[...] = _norm_mod(h_ref[...], gain_ref[...], sh_ref[0], sc_ref[0]).astype(BF16)

    @pl.when(jnp.logical_and(j == 0, i >= n_lat_tiles))
    def _():
        a_scr[0:hc_ref.shape[0], :] = _norm_mod(hc_ref[...], gain_ref[...], sh_ref[0], sc_ref[0]).astype(BF16)

    acc = _mm(a_scr[...], w_ref[...])

    @pl.when(j < nq)
    def _():
        q_ref[...] = acc.astype(BF16)

    @pl.when(jnp.logical_and(j >= nq, j < 2 * nq))
    def _():
        kn = _group_rms(acc, ones_ref[...], gk_ref[...], DIFF_DIM)
        kr_ref[...] = _rope(kn, cos_ref[...], slo_ref[...], shi_ref[...]).astype(BF16)

    @pl.when(jnp.logical_and(j >= 2 * nq, j < 2 * nq + nv))
    def _():
        v_ref[...] = acc.astype(BF16)

    @pl.when(j >= 2 * nq + nv)
    def _():
        f_ref[...] = acc.astype(BF16)


def _inproj_odd_kernel(h_ref, sh_ref, sc_ref, gain_ref, w_ref, gq_ref, gk_ref, ones_ref,
                       cg_ref, q_ref, k_ref, v_ref, a_scr, *, ncg, nh, qscale):
    j = pl.program_id(1)

    @pl.when(j == 0)
    def _():
        a_scr[...] = _norm_mod(h_ref[...], gain_ref[...], sh_ref[0], sc_ref[0]).astype(BF16)

    acc = _mm(a_scr[...], w_ref[...])

    @pl.when(j < ncg)
    def _():
        cg_ref[...] = acc.astype(BF16)

    @pl.when(jnp.logical_and(j >= ncg, j < ncg + nh))
    def _():
        q_ref[...] = (_group_rms(acc, ones_ref[...], gq_ref[...], NA_DIM) * qscale).astype(BF16)

    @pl.when(jnp.logical_and(j >= ncg + nh, j < ncg + 2 * nh))
    def _():
        k_ref[...] = _group_rms(acc, ones_ref[...], gk_ref[...], NA_DIM).astype(BF16)

    @pl.when(j >= ncg + 2 * nh)
    def _():
        v_ref[...] = acc.astype(BF16)


def _row_specs(tm, d, tiles_per_batch, n_batch):
    def mod_spec(k):
        return pl.BlockSpec((1, 1, d), lambda i, j: (jnp.minimum(i // tiles_per_batch, n_batch) * N_MOD + k, 0, 0))

    return pl.BlockSpec((tm, d), lambda i, j: (i, 0)), mod_spec


def _clamped(tm, tn, lo, n):
    return pl.BlockSpec((tm, tn), lambda i, j: (i, jnp.clip(j - lo, 0, n - 1)))


def _ones_blockdiag(group, size=MXU_DIM):
    idx = np.arange(size) // group
    return jnp.asarray((idx[:, None] == idx[None, :]).astype(np.float32), dtype=BF16)


def _inproj_even(hh, mods, gain, w, layer, gk, rope_tabs, dims, ctx_rows=None):
    d = hh.shape[1]
    r = dims["r"]
    tm, tn = dims["tm_in"], dims["tn"]
    n_lat_tiles = dims["b"] * dims["s"] // tm if ctx_rows is not None else pl.cdiv(r, tm)
    if ctx_rows is None:
        ctx_rows = jnp.zeros((8, d), F32)
    assert ctx_rows.shape[0] <= tm
    ev_q, ev_v, fw = dims["ev_q"], dims["ev_v"], dims["fw"]
    nq, nv, nf = ev_q // tn, ev_v // tn, fw // tn
    nj = 2 * nq + nv + nf
    hspec, mod_spec = _row_specs(tm, d, dims["s"] // tm, dims["b"])
    const = lambda shape: pl.BlockSpec(shape, lambda i, j: (0,) * len(shape))
    tab = pl.BlockSpec((tm, LANES), lambda i, j: (i, 0))
    kern = functools.partial(_inproj_even_kernel, nq=nq, nv=nv, n_lat_tiles=n_lat_tiles)
    return pl.pallas_call(
        kern,
        grid=(pl.cdiv(r, tm), nj),
        in_specs=[pl.BlockSpec((tm, d), lambda i, j: (jnp.minimum(i, n_lat_tiles - 1), 0)),
                  const(ctx_rows.shape),
                  mod_spec(0), mod_spec(1), const((1, d)),
                  pl.BlockSpec((None, d, tn), lambda i, j: (layer, 0, j)),
                  const((1, tn)), const((MXU_DIM, MXU_DIM)), tab, tab, tab],
        out_specs=[_clamped(tm, tn, 0, nq), _clamped(tm, tn, nq, nq),
                   _clamped(tm, tn, 2 * nq, nv), _clamped(tm, tn, 2 * nq + nv, nf)],
        out_shape=[jax.ShapeDtypeStruct((r, ev_q), BF16)] * 2
        + [jax.ShapeDtypeStruct((r, ev_v), BF16), jax.ShapeDtypeStruct((r, fw), BF16)],
        scratch_shapes=[pltpu.VMEM((tm, d), BF16)],
        compiler_params=_cparams("parallel", "arbitrary"),
        name="inproj_even",
    )(hh, ctx_rows, mods, mods, gain, w, gk, _ones_blockdiag(DIFF_DIM), *rope_tabs)


def _inproj_odd(hh, mods, gain, w, layer, gq, gk, dims):
    r, d = hh.shape
    tm, tn = dims["tm_in"], dims["tn"]
    cw, nw = dims["cw"], dims["nw"]
    ncg, nh = 3 * cw // tn, nw // tn
    nj = ncg + 3 * nh
    hspec, mod_spec = _row_specs(tm, d, dims["s"] // tm, dims["b"])
    const = lambda shape: pl.BlockSpec(shape, lambda i, j: (0,) * len(shape))
    kern = functools.partial(_inproj_odd_kernel, ncg=ncg, nh=nh, qscale=NA_DIM ** -0.5 * LOG2E)
    return pl.pallas_call(
        kern,
        grid=(pl.cdiv(r, tm), nj),
        in_specs=[hspec, mod_spec(0), mod_spec(1), const((1, d)),
                  pl.BlockSpec((None, d, tn), lambda i, j: (layer, 0, j)),
                  const((1, tn)), const((1, tn)), const((MXU_DIM, MXU_DIM))],
        out_specs=[_clamped(tm, tn, 0, ncg), _clamped(tm, tn, ncg, nh), _clamped(tm, tn, ncg + nh, nh),
                   _clamped(tm, tn, ncg + 2 * nh, nh)],
        out_shape=[jax.ShapeDtypeStruct((r, 3 * cw), BF16)] + [jax.ShapeDtypeStruct((r, nw), BF16)] * 3,
        scratch_shapes=[pltpu.VMEM((tm, d), BF16)],
        compiler_params=_cparams("parallel", "arbitrary"),
        name="inproj_odd",
    )(hh, mods, mods, gain, w, gq, gk, _ones_blockdiag(NA_DIM))


def _split_components(q):
    lane = lax.broadcasted_iota(jnp.int32, q.shape, 1)
    zero = jnp.zeros_like(q)
    return jnp.concatenate([jnp.where(lane < DIFF_DIM, q, zero), jnp.where(lane >= DIFF_DIM, q, zero)], axis=0)


def _online_softmax(chunks):
    lead = ATTN_SCORE_LEAD
    rows = chunks[0][0].shape[0]
    scores = [_mm_nt(q2, kk) for q2, kk, _ in chunks[:lead]]
    m = jnp.full((rows, LANES), NEG_BIG, F32)
    acc = jnp.zeros((rows, 2 * DIFF_V), F32)
    for c, (_, _, vv) in enumerate(chunks):
        if c + lead < len(chunks):
            scores.append(_mm_nt(chunks[c + lead][0], chunks[c + lead][1]))
        s = scores[c]
        m_new = jnp.maximum(m, jnp.max(s, axis=-1, keepdims=True))
        alpha = jnp.exp2(m - m_new)
        p = jnp.exp2((s - _tile_lanes(m_new, s.shape[1] // LANES)).astype(BF16))
        v_ext = jnp.concatenate([vv, jnp.ones_like(vv)], axis=1)
        acc = _tile_lanes(alpha, 2) * acc + _mm(p, v_ext)
        m = m_new
    return acc


def _diff_attn_kernel(*refs, n_chunks, tk, lam_init, n_cast=0, with_mods=False):
    if n_chunks:
        (lam_ref, g_ref, gq_ref, ones_ref, q_ref, cos_ref, slo_ref, shi_ref,
         kc_ref, vc_ref, k_ref, v_ref) = refs[:12]
        n_side = n_cast + 3 * with_mods
        o_ref = refs[12 + n_side]
        for w_ref, wo_ref in zip(refs[12:12 + n_cast], refs[13 + n_side:]):
            wo_ref[...] = w_ref[...].astype(BF16)
        if with_mods:
            _mod_kernel(*refs[12 + n_cast:12 + n_side], refs[-1])
    else:
        lam_ref, g_ref, gq_ref, ones_ref, q_ref, kc_ref, vc_ref, o_ref = refs
    tq = q_ref.shape[0]
    qn = _group_rms(q_ref[...].astype(F32), ones_ref[...], gq_ref[...], DIFF_DIM) * (DIFF_DIM ** -0.5 * LOG2E)
    chunks = []
    if n_chunks:
        q2 = _split_components(_rope(qn, cos_ref[...], slo_ref[...], shi_ref[...]).astype(BF16))
        chunks = [(q2, k_ref[c * tk:(c + 1) * tk, :], v_ref[c * tk:(c + 1) * tk, :]) for c in range(n_chunks)]
    chunks.append((_split_components(qn.astype(BF16)), kc_ref[...], vc_ref[...]))
    acc = _online_softmax(chunks)

    lv = lam_ref[...]
    lam = (jnp.exp(jnp.sum(lv[0:1] * lv[1:2], axis=-1, keepdims=True))
           - jnp.exp(jnp.sum(lv[2:3] * lv[3:4], axis=-1, keepdims=True)) + lam_init)
    o2 = acc[:, :DIFF_V] / acc[:, DIFF_V:]
    o = o2[:tq] - lam * o2[tq:]
    o = o * lax.rsqrt(jnp.mean(o * o, axis=-1, keepdims=True) + EPS) * g_ref[...] * (1.0 - lam_init)
    o_ref[...] = o.astype(BF16)


def _cast_block_rows(rows, n_steps):
    rb = 16 * pl.cdiv(pl.cdiv(rows, 16), n_steps)
    while rows % rb:
        rb += 16
    return rb


def _diff_attn_latent(lam_vec, subln, gq, q, rope_tabs, kr, v, dims, lam_init, cast_ws=(), mod_args=None):
    b, s, ctx, r = dims["b"], dims["s"], dims["ctx"], dims["r"]
    tq, tk = dims["tq"], dims["tk"]
    nh = q.shape[1] // DIFF_V
    ctx_blk0 = b * s // ctx
    nq = s // tq
    kern = functools.partial(_diff_attn_kernel, n_chunks=s // tk, tk=tk, lam_init=lam_init, n_cast=len(cast_ws),
                             with_mods=mod_args is not None)
    step_of = lambda bi, h, qi: (bi * nh + h) * nq + qi
    w2d = [w.reshape(-1, w.shape[-1]) for w in cast_ws]
    cast_specs = []
    for w in w2d:
        rb = _cast_block_rows(w.shape[0], b * nh * nq)
        nblk = w.shape[0] // rb
        cast_specs.append(pl.BlockSpec(
            (rb, w.shape[1]), lambda bi, h, qi, nblk=nblk: (jnp.minimum(step_of(bi, h, qi), nblk - 1), 0)))
    mod_in_specs, mod_out_specs, mod_out_shapes, mod_in = [], [], [], []
    if mod_args is not None:
        crow, w_mod, b_mod, first_layer = mod_args
        depth, _, nm = w_mod.shape
        *mod_in_specs, mod_out = _mod_side_specs(w_mod, first_layer, b * nh * nq, step_of)
        mod_out_specs = [mod_out]
        mod_out_shapes = [jax.ShapeDtypeStruct((depth - first_layer, 8, nm), F32)]
        mod_in = [crow, w_mod, b_mod.reshape(depth, 1, nm)]
    const = lambda shape: pl.BlockSpec(shape, lambda bi, h, qi: (0, 0))
    tab = pl.BlockSpec((tq, LANES), lambda bi, h, qi: (bi * (s // tq) + qi, 0))
    qspec = pl.BlockSpec((tq, DIFF_V), lambda bi, h, qi: (bi * (s // tq) + qi, h))
    cspec = pl.BlockSpec((ctx, DIFF_V), lambda bi, h, qi: (ctx_blk0 + bi, h))
    kspec = pl.BlockSpec((s, DIFF_V), lambda bi, h, qi: (bi, h))
    out = pl.pallas_call(
        kern,
        grid=(b, nh, nq),
        in_specs=[const((4, DIFF_DIM)), const((1, DIFF_V)), const((1, DIFF_V)), const((DIFF_V, DIFF_V)),
                  qspec, tab, tab, tab, cspec, cspec, kspec, kspec] + cast_specs + mod_in_specs,
        out_specs=[qspec] + cast_specs + mod_out_specs,
        out_shape=[jax.ShapeDtypeStruct((r, nh * DIFF_V), BF16)]
        + [jax.ShapeDtypeStruct(w.shape, BF16) for w in w2d] + mod_out_shapes,
        compiler_params=_cparams("arbitrary", "arbitrary", "arbitrary"),
        name="diff_attn_latent",
    )(lam_vec, subln, gq, _ones_blockdiag(DIFF_DIM, DIFF_V), q, *rope_tabs, kr, v, kr, v, *w2d, *mod_in)
    casts = [o.reshape(w.shape) for o, w in zip(out[1:1 + len(w2d)], cast_ws)]
    return out[0], casts, (out[-1] if mod_args is not None else None)


def _diff_attn_ctx(lam_vec, subln, gq, q, kr, v, o_lat, dims, lam_init):
    b, s, ctx = dims["b"], dims["s"], dims["ctx"]
    nh = q.shape[1] // DIFF_V
    ctx_blk0 = b * s // ctx
    kern = functools.partial(_diff_attn_kernel, n_chunks=0, tk=0, lam_init=lam_init)
    const = lambda shape: pl.BlockSpec(shape, lambda bi, h: (0, 0))
    cspec = pl.BlockSpec((ctx, DIFF_V), lambda bi, h: (ctx_blk0 + bi, h))

    def wrapped(lam_ref, g_ref, gq_ref, ones_ref, q_ref, kc_ref, vc_ref, alias_ref, o_ref):
        del alias_ref
        kern(lam_ref, g_ref, gq_ref, ones_ref, q_ref, kc_ref, vc_ref, o_ref)

    return pl.pallas_call(
        wrapped,
        grid=(b, nh),
        in_specs=[const((4, DIFF_DIM)), const((1, DIFF_V)), const((1, DIFF_V)), const((DIFF_V, DIFF_V)),
                  cspec, cspec, cspec, pl.BlockSpec(memory_space=pl.ANY)],
        out_specs=cspec,
        out_shape=jax.ShapeDtypeStruct(o_lat.shape, BF16),
        input_output_aliases={7: 0},
        compiler_params=_cparams("parallel", "parallel"),
        name="diff_attn_ctx",
    )(lam_vec, subln, gq, _ones_blockdiag(DIFF_DIM, DIFF_V), q, kr, v, o_lat)


def _dft_cs(n):
    k = np.arange(n)
    ang = 2.0 * np.pi * ((k[:, None] * k[None, :]) % n) / n
    return np.cos(ang), np.sin(ang)


def _channel_dft(groups, scale):
    c, s = _dft_cs(FOURIER_DIM)
    eye = np.eye(groups)
    return np.kron(eye, c) * scale, np.kron(eye, s) * scale


def _fourier_rows_kernel(x_ref, w_ref, tc_ref, ts_ref, y_ref, *, cg, fw, nr):
    for c in range(cg):
        y = _mm(w_ref[...], x_ref[:, c * fw:(c + 1) * fw])
        yr, yi = y[:nr], y[nr:]
        tc = _tile_lanes(tc_ref[c], fw // LANES)
        ts = _tile_lanes(ts_ref[c], fw // LANES)
        y_ref[0, 0, :, c * fw:(c + 1) * fw] = (yr * tc + yi * ts).astype(BF16)
        y_ref[0, 1, :, c * fw:(c + 1) * fw] = (yi * tc - yr * ts).astype(BF16)


def _fourier_cols_kernel(y_ref, w2_ref, wc_ref, ws_ref, o_ref, *, ag, fw, nc):
    for a in range(ag):
        yy = jnp.concatenate([y_ref[0, 0, a], y_ref[0, 1, a]], axis=0)
        z = _mm(w2_ref[...], yy)
        out = _mm(z[:nc].astype(BF16), wc_ref[...]) + _mm(z[nc:].astype(BF16), ws_ref[...])
        o_ref[:, a * fw:(a + 1) * fw] = out.astype(BF16)


def _fourier_dense_kernel(x_ref, wc_ref, ws_ref, cn_ref, sn_ref, alias_ref, o_ref):
    del alias_ref
    x = x_ref[...]
    gc = _mm(x, wc_ref[...]).astype(BF16)
    gs = _mm(x, ws_ref[...]).astype(BF16)
    o_ref[...] = (_mm(cn_ref[...], gc) - _mm(sn_ref[...], gs)).astype(BF16)


def _fourier_latent(f, dims):
    b, s, r = dims["b"], dims["s"], dims["r"]
    fw = f.shape[1]
    groups = fw // FOURIER_DIM
    nc = GRID_W
    nr = s // nc
    cg = 4
    ag = 8
    c1, s1 = _dft_cs(nr)
    w1 = jnp.asarray(np.concatenate([c1, -s1], axis=0), dtype=BF16)
    ang = 2.0 * np.pi * (np.arange(nc)[:, None] * np.arange(nr)[None, :]) / s
    tcos = jnp.asarray(np.repeat(np.cos(ang)[:, :, None], LANES, axis=2), dtype=F32)
    tsin = jnp.asarray(np.repeat(np.sin(ang)[:, :, None], LANES, axis=2), dtype=F32)
    x2d = f.reshape(r // nc, nc * fw)
    y = pl.pallas_call(
        functools.partial(_fourier_rows_kernel, cg=cg, fw=fw, nr=nr),
        grid=(b, nc // cg),
        in_specs=[pl.BlockSpec((nr, cg * fw), lambda bi, j: (bi, j)),
                  pl.BlockSpec((2 * nr, nr), lambda bi, j: (0, 0)),
                  pl.BlockSpec((cg, nr, LANES), lambda bi, j: (j, 0, 0)),
                  pl.BlockSpec((cg, nr, LANES), lambda bi, j: (j, 0, 0))],
        out_specs=pl.BlockSpec((1, 2, nr, cg * fw), lambda bi, j: (bi, 0, 0, j)),
        out_shape=jax.ShapeDtypeStruct((b, 2, nr, nc * fw), BF16),
        compiler_params=_cparams("parallel", "parallel"),
        name="fourier_rows",
    )(x2d, w1, tcos, tsin)
    c2, s2 = _dft_cs(nc)
    w2 = jnp.asarray(np.block([[c2, s2], [-s2, c2]]), dtype=BF16)
    wc, ws = _channel_dft(groups, 1.0 / math.sqrt(s * FOURIER_DIM))
    out = pl.pallas_call(
        functools.partial(_fourier_cols_kernel, ag=ag, fw=fw, nc=nc),
        grid=(b, nr // ag),
        in_specs=[pl.BlockSpec((1, 2, ag, nc, fw), lambda bi, j: (bi, 0, j, 0, 0)),
                  pl.BlockSpec((2 * nc, 2 * nc), lambda bi, j: (0, 0)),
                  pl.BlockSpec((fw, fw), lambda bi, j: (0, 0)),
                  pl.BlockSpec((fw, fw), lambda bi, j: (0, 0))],
        out_specs=pl.BlockSpec((nc, ag * fw), lambda bi, j: (bi, j)),
        out_shape=jax.ShapeDtypeStruct((r // nr, nr * fw), BF16),
        compiler_params=_cparams("parallel", "parallel"),
        name="fourier_cols",
    )(y.reshape(b, 2, nr, nc, fw), w2, jnp.asarray(wc, dtype=BF16), jnp.asarray(ws, dtype=BF16))
    return out.reshape(r, fw)


def _fourier_ctx(f, fo, dims):
    b, s, ctx = dims["b"], dims["s"], dims["ctx"]
    fw = f.shape[1]
    groups = fw // FOURIER_DIM
    wc, ws = _channel_dft(groups, 1.0 / math.sqrt(ctx * FOURIER_DIM))
    cn, sn = _dft_cs(ctx)
    blk0 = b * s // ctx
    rows = pl.BlockSpec((ctx, fw), lambda bi: (blk0 + bi, 0))
    const = lambda shape: pl.BlockSpec(shape, lambda bi: (0, 0))
    return pl.pallas_call(
        _fourier_dense_kernel,
        grid=(b,),
        in_specs=[rows, const((fw, fw)), const((fw, fw)), const((ctx, ctx)), const((ctx, ctx)),
                  pl.BlockSpec(memory_space=pl.ANY)],
        out_specs=rows,
        out_shape=jax.ShapeDtypeStruct(fo.shape, BF16),
        input_output_aliases={5: 0},
        compiler_params=_cparams("parallel"),
        name="fourier_ctx",
    )(f, jnp.asarray(wc, dtype=BF16), jnp.asarray(ws, dtype=BF16),
      jnp.asarray(cn, dtype=BF16), jnp.asarray(sn, dtype=BF16), fo)


def _gated_conv(gb_ref, gc_ref, hh_ref, gcp_ref, hhp_ref, gcn_ref, hhn_ref, w_ref, *, n_lat_tiles, s, ctx):
    i = pl.program_id(0)
    tm = gb_ref.shape[0]
    is_lat = i < n_lat_tiles
    seq = jnp.where(is_lat, s, ctx)
    off = lax.rem(i * tm, seq)
    u = gc_ref[...].astype(F32) * hh_ref[...].astype(F32)
    halo = gcp_ref.shape[0]
    u_prev = gcp_ref[halo - 1:halo, :].astype(F32) * hhp_ref[halo - 1:halo, :].astype(F32)
    u_next = gcn_ref[0:1, :].astype(F32) * hhn_ref[0:1, :].astype(F32)
    u_prev = jnp.where(off == 0, 0.0, u_prev)
    u_next = jnp.where(lax.rem(off + tm, seq) == 0, 0.0, u_next)
    row = lax.broadcasted_iota(jnp.int32, (tm, 1), 0)
    dn = jnp.where(row == 0, u_prev, pltpu.roll(u, 1, 0))
    up = jnp.where(row == tm - 1, u_next, pltpu.roll(u, tm - 1, 0))
    if ctx < tm:
        inner_start = functools.reduce(jnp.logical_or, [row == k * ctx for k in range(1, tm // ctx)])
        inner_end = functools.reduce(jnp.logical_or, [row == k * ctx - 1 for k in range(1, tm // ctx)])
        is_ctx = jnp.logical_not(is_lat)
        dn = jnp.where(jnp.logical_and(is_ctx, inner_start), 0.0, dn)
        up = jnp.where(jnp.logical_and(is_ctx, inner_end), 0.0, up)
    w = w_ref[...]
    y = dn * w[0:1] + u * w[1:2] + up * w[2:3]
    return gb_ref[...].astype(F32) * y


def _conv_specs(cw, tm, r):
    halo = 16
    nhb = tm // halo
    last = r // halo - 1
    cur = lambda off: pl.BlockSpec((tm, cw), lambda i: (i, off))
    prev = lambda off: pl.BlockSpec((halo, cw), lambda i: (jnp.maximum(i * nhb - 1, 0), off))
    nxt = lambda off: pl.BlockSpec((halo, cw), lambda i: (jnp.minimum((i + 1) * nhb, last), off))
    return [cur(0), cur(1), cur(2), prev(1), prev(2), nxt(1), nxt(2), pl.BlockSpec((CONV_K, cw), lambda i: (0, 0))]


NA_BLOCK_ROWS = 8
NA_KEY_ROWS = 16
NA_BLOCKS_PER_TRIP = 16


def _na_plan(rows):
    wr = min(NA_WIN_ROWS, rows)
    starts, variants, keys = [], [], {}
    for jb in range(rows // NA_BLOCK_ROWS):
        r0 = jb * NA_BLOCK_ROWS
        ks = int(np.clip(r0 - wr // 2, 0, rows - NA_KEY_ROWS))
        rs = [int(np.clip(r0 + t - wr // 2, 0, rows - wr)) for t in range(NA_BLOCK_ROWS)]
        key = (r0 - ks, tuple(x - ks for x in rs))
        variants.append(keys.setdefault(key, len(keys)))
        starts.append(ks)
    return starts, variants, list(keys), wr


def _na_bias(rpb, rows):
    _, _, keys, wr = _na_plan(rows)
    col = np.arange(GRID_W)
    cs = np.clip(col - NA_WIN_COLS // 2, 0, GRID_W - NA_WIN_COLS)
    kc = np.arange(GRID_W)
    dc = kc[None, :] - col[:, None] + (NA_WIN_COLS - 1)
    col_ok = (kc[None, :] >= cs[:, None]) & (kc[None, :] < cs[:, None] + NA_WIN_COLS)
    n_dr, n_dc = 2 * NA_WIN_ROWS - 1, 2 * NA_WIN_COLS - 1
    col_sel = ((dc[None] == np.arange(n_dc)[:, None, None]) & col_ok[None]).astype(np.float32)
    t = np.arange(NA_BLOCK_ROWS)
    i = np.arange(NA_KEY_ROWS)
    row_sel, row_okv = [], []
    for r0_rel, rs_rel in keys:
        dr = i[None, :] - (r0_rel + t[:, None]) + (NA_WIN_ROWS - 1)
        rs_arr = np.asarray(rs_rel)
        row_ok = (i[None, :] >= rs_arr[:, None]) & (i[None, :] < rs_arr[:, None] + wr)
        row_sel.append(((dr[None] == np.arange(n_dr)[:, None, None]) & row_ok[None]).astype(np.float32))
        row_okv.append(row_ok)
    row_sel = np.stack(row_sel)
    ok = (jnp.asarray(np.stack(row_okv))[:, None, :, None, :, None]
          & jnp.asarray(col_ok)[None, None, None, :, None, :])
    hi = lax.Precision.HIGHEST
    col_exp = jnp.einsum("hrd,dck->hrck", rpb.astype(F32) * LOG2E, jnp.asarray(col_sel), precision=hi)
    vals = jnp.einsum("vrti,hrck->vhtcik", jnp.asarray(row_sel), col_exp, precision=hi)
    vals = jnp.where(ok, vals, NEG_BIG)
    return vals.reshape(len(keys), rpb.shape[0], NA_BLOCK_ROWS * GRID_W, NA_KEY_ROWS * GRID_W)


def _na_kernel(q_ref, k_ref, v_ref, kc_ref, vc_ref, bias_ref, o_ref, *, starts, variants):
    qb = NA_BLOCK_ROWS * GRID_W
    kb = NA_KEY_ROWS * GRID_W

    def lookup(table, jb):
        out = jnp.int32(table[0])
        for idx in range(1, len(table)):
            out = jnp.where(jb == idx, jnp.int32(table[idx]), out)
        return out

    def block(jb):
        q0 = pl.multiple_of(jb * qb, qb)
        k0 = pl.multiple_of(lookup(starts, jb) * GRID_W, GRID_W)
        q = q_ref[pl.ds(q0, qb), :]
        s_nb = _mm_nt(q, k_ref[pl.ds(k0, kb), :]) + bias_ref[lookup(variants, jb), 0]
        s_cx = _mm_nt(q, kc_ref[...])
        m = jnp.maximum(jnp.max(s_nb, axis=-1, keepdims=True), jnp.max(s_cx, axis=-1, keepdims=True))
        p_nb = jnp.exp2((s_nb - m).astype(BF16))
        p_cx = jnp.exp2((s_cx - m).astype(BF16))
        v_nb, v_cx = v_ref[pl.ds(k0, kb), :], vc_ref[...]
        o = (_mm(p_nb, jnp.concatenate([v_nb, jnp.ones_like(v_nb)], axis=1))
             + _mm(p_cx, jnp.concatenate([v_cx, jnp.ones_like(v_cx)], axis=1)))
        o_ref[pl.ds(q0, qb), :] = (o[:, :NA_DIM] / o[:, NA_DIM:]).astype(BF16)

    per_trip = math.gcd(NA_BLOCKS_PER_TRIP, len(starts))

    def body(t, carry):
        for u in range(per_trip):
            block(t * per_trip + u)
        return carry

    lax.fori_loop(0, len(starts) // per_trip, body, 0)


def _na_ctx_kernel(q_ref, kc_ref, vc_ref, alias_ref, o_ref):
    del alias_ref
    s = _mm_nt(q_ref[...], kc_ref[...])
    p = jnp.exp2(s - jnp.max(s, axis=-1, keepdims=True))
    o = _mm(p.astype(BF16), vc_ref[...]) / jnp.sum(p, axis=-1, keepdims=True)
    o_ref[...] = o.astype(BF16)


def _na_latent(q, k, v, bias, dims):
    b, s, ctx, r = dims["b"], dims["s"], dims["ctx"], dims["r"]
    nh = q.shape[1] // NA_DIM
    starts, variants, keys, _ = _na_plan(s // GRID_W)
    blk0 = b * s // ctx
    lat = pl.BlockSpec((s, NA_DIM), lambda bi, h: (bi, h))
    cx = pl.BlockSpec((ctx, NA_DIM), lambda bi, h: (blk0 + bi, h))
    nvar = len(keys)
    return pl.pallas_call(
        functools.partial(_na_kernel, starts=tuple(starts), variants=tuple(variants)),
        grid=(b, nh),
        in_specs=[lat, lat, lat, cx, cx,
                  pl.BlockSpec((nvar, 1) + bias.shape[2:], lambda bi, h: (0, h, 0, 0))],
        out_specs=lat,
        out_shape=jax.ShapeDtypeStruct((r, nh * NA_DIM), BF16),
        compiler_params=_cparams("parallel", "arbitrary"),
        name="na_latent",
    )(q, k, v, k, v, bias)


def _na_ctx(q, k, v, o_lat, dims):
    b, s, ctx = dims["b"], dims["s"], dims["ctx"]
    nh = q.shape[1] // NA_DIM
    blk0 = b * s // ctx
    cx = pl.BlockSpec((ctx, NA_DIM), lambda bi, h: (blk0 + bi, h))
    return pl.pallas_call(
        _na_ctx_kernel,
        grid=(b, nh),
        in_specs=[cx, cx, cx, pl.BlockSpec(memory_space=pl.ANY)],
        out_specs=cx,
        out_shape=jax.ShapeDtypeStruct(o_lat.shape, BF16),
        input_output_aliases={3: 0},
        compiler_params=_cparams("parallel", "parallel"),
        name="na_ctx",
    )(q, k, v, o_lat)


def _outproj_kernel(xa_ref, xb_ref, wa_ref, wb_ref, h_ref, gate_ref, o_ref):
    y = _mm(xa_ref[...], wa_ref[...]) + _mm(xb_ref[...], wb_ref[...])
    o_ref[...] = h_ref[...] + gate_ref[0] * y


def _outproj_split_kernel(xa_ref, xb_ref, wa_ref, wb_ref, h_ref, hc_ref, gate_ref, o_ref, *, n_lat_tiles):
    i = pl.program_id(0)
    y = gate_ref[0] * (_mm(xa_ref[...], wa_ref[...]) + _mm(xb_ref[...], wb_ref[...]))

    @pl.when(i < n_lat_tiles)
    def _():
        o_ref[...] = h_ref[...] + y

    @pl.when(i >= n_lat_tiles)
    def _():
        o_ref[...] = hc_ref[...] + y


def _outproj_conv_kernel(*refs, conv_args):
    conv_refs, (xb_ref, wa_ref, wb_ref, h_ref, gate_ref, o_ref) = refs[:8], refs[8:]
    xa = _gated_conv(*conv_refs, **conv_args).astype(BF16)
    y = _mm(xa, wa_ref[...]) + _mm(xb_ref[...], wb_ref[...])
    o_ref[...] = h_ref[...] + gate_ref[0] * y


def _outproj(xa, xb, w, layer, hh, mods, n_tiles, dims, conv_w=None, ctx_rows=None):
    d = hh.shape[1]
    r = dims["r"]
    tm = dims["tm"]
    if ctx_rows is not None:
        assert conv_w is None
        n_lat = hh.shape[0] // tm
        kern = functools.partial(_outproj_split_kernel, n_lat_tiles=n_lat)
        hspec, mod_spec = _row_specs(tm, d, dims["tpb"], dims["b"])
        one = lambda spec: pl.BlockSpec(spec.block_shape, lambda i: spec.index_map(i, 0))
        ka, kb = xa.shape[1], xb.shape[1]
        return pl.pallas_call(
            kern,
            grid=(n_tiles,),
            in_specs=[pl.BlockSpec((tm, ka), lambda i: (i, 0)), pl.BlockSpec((tm, kb), lambda i: (i, 0)),
                      pl.BlockSpec((None, ka, d), lambda i: (layer, 0, 0)),
                      pl.BlockSpec((None, kb, d), lambda i: (layer, ka // kb, 0)),
                      pl.BlockSpec((tm, d), lambda i: (jnp.minimum(i, n_lat - 1), 0)),
                      pl.BlockSpec((tm, d), lambda i: (jnp.maximum(i - n_lat, 0), 0)),
                      one(mod_spec(2))],
            out_specs=one(hspec),
            out_shape=jax.ShapeDtypeStruct((r, d), F32),
            compiler_params=_cparams("parallel"),
            name="outproj",
        )(xa, xb, w, w, hh, ctx_rows, mods)
    kb = xb.shape[1]
    ka = w.shape[1] - kb
    assert ka % kb == 0
    hspec, mod_spec = _row_specs(tm, d, dims["tpb"], dims["b"])
    one = lambda spec: pl.BlockSpec(spec.block_shape, lambda i: spec.index_map(i, 0))
    if conv_w is None:
        kern, xa_specs, xa_args = _outproj_kernel, [pl.BlockSpec((tm, ka), lambda i: (i, 0))], [xa]
    else:
        conv_args = dict(n_lat_tiles=dims["b"] * dims["tpb"], s=dims["s"], ctx=dims["ctx"])
        kern = functools.partial(_outproj_conv_kernel, conv_args=conv_args)
        xa_specs, xa_args = _conv_specs(ka, tm, r), [xa] * 7 + [conv_w]
    return pl.pallas_call(
        kern,
        grid=(n_tiles,),
        in_specs=xa_specs + [pl.BlockSpec((tm, kb), lambda i: (i, 0)),
                             pl.BlockSpec((None, ka, d), lambda i: (layer, 0, 0)),
                             pl.BlockSpec((None, kb, d), lambda i: (layer, ka // kb, 0)),
                             one(hspec), one(mod_spec(2))],
        out_specs=one(hspec),
        out_shape=jax.ShapeDtypeStruct((r, d), F32),
        input_output_aliases={len(xa_specs) + 3: 0},
        compiler_params=_cparams("parallel"),
        name="outproj",
    )(*xa_args, xb, w, w, hh, mods)


def _ffn_kernel(h_ref, sh_ref, sc_ref, gate_ref, gain_ref, wg_ref, wv_ref, wo_ref, o_ref, u_scr, acc_scr):
    j = pl.program_id(1)

    @pl.when(j == 0)
    def _():
        u_scr[...] = _norm_mod(h_ref[...], gain_ref[...], sh_ref[0], sc_ref[0]).astype(BF16)
        acc_scr[...] = jnp.zeros(acc_scr.shape, F32)

    u = u_scr[...]
    g = _mm(u, wg_ref[...])
    val = _mm(u, wv_ref[...])
    act = (g * jax.nn.sigmoid(g) * val).astype(BF16)
    acc_scr[...] += _mm(act, wo_ref[...])

    @pl.when(j == pl.num_programs(1) - 1)
    def _():
        o_ref[...] = h_ref[...] + gate_ref[0] * acc_scr[...]


def _ffn(hh, mods, gain, w_in, w_out, layer, n_tiles, dims):
    r, d = hh.shape
    tm = dims["tm"]
    hid = w_out.shape[1]
    th = 512 if hid % 512 == 0 else 256
    nj = hid // th
    hspec, mod_spec = _row_specs(tm, d, dims["tpb"], dims["b"])
    in_place = n_tiles * tm == r
    return pl.pallas_call(
        _ffn_kernel,
        grid=(n_tiles, nj),
        in_specs=[hspec, mod_spec(3), mod_spec(4), mod_spec(5), pl.BlockSpec((1, d), lambda i, j: (0, 0)),
                  pl.BlockSpec((None, d, th), lambda i, j: (layer, 0, j)),
                  pl.BlockSpec((None, d, th), lambda i, j: (layer, 0, nj + j)),
                  pl.BlockSpec((None, th, d), lambda i, j: (layer, j, 0))],
        out_specs=hspec,
        out_shape=jax.ShapeDtypeStruct((n_tiles * tm, d), F32),
        input_output_aliases={0: 0} if in_place else {},
        scratch_shapes=[pltpu.VMEM((tm, d), BF16), pltpu.VMEM((tm, d), F32)],
        compiler_params=_cparams("parallel", "arbitrary"),
        name="ffn",
    )(hh, mods, mods, mods, gain, w_in, w_in, w_out)


def _rope_tables(dims):
    s, b, ctx = dims["s"], dims["b"], dims["ctx"]
    t = np.arange(s)
    n_freq = DIFF_DIM // 4
    inv = (ROPE_BASE ** (-np.arange(n_freq, dtype=np.float32) / n_freq)).astype(np.float32)
    lane = np.arange(LANES) % DIFF_DIM
    chunk = lane // n_freq
    pos = np.where((chunk // 2)[None, :] == 0, (t // GRID_W)[:, None], (t % GRID_W)[:, None]).astype(np.float32)
    ang = pos * inv[lane % n_freq][None, :]
    cos, sin = np.cos(ang), np.sin(ang)
    s_lo = np.where((chunk % 2)[None, :] == 1, sin, 0.0)
    s_hi = np.where((chunk % 2)[None, :] == 0, -sin, 0.0)
    pad = b * ctx

    def full(tab, fill):
        return jnp.asarray(np.concatenate([np.tile(tab, (b, 1)), np.full((pad, LANES), fill)], axis=0), dtype=F32)

    return full(cos, 1.0), full(s_lo, 0.0), full(s_hi, 0.0)


def kernel(x, c, ctx, c_ctx, w_mod, b_mod, norm_gain, w_ffn_in, w_ffn_out, ev_w_in, ev_w_out, ev_qk_gain,
           ev_lambda, ev_subln_gain, od_w_in, od_w_out, od_qk_gain, od_conv_w, od_rpb):
    b, s, d = x.shape
    n_ctx = ctx.shape[1]
    depth = w_mod.shape[0]
    tm = b * n_ctx
    r = b * s + b * n_ctx
    assert s % tm == 0 and s % GRID_W == 0
    ev_v = ev_w_out.shape[1] - FOURIER_GROUPS * FOURIER_DIM
    dims = dict(b=b, s=s, ctx=n_ctx, r=r, tm=tm, tm_in=min(1024, s), tn=512, tpb=s // tm,
                ev_q=ev_v // DIFF_V * 2 * DIFF_DIM, ev_v=ev_v, fw=FOURIER_GROUPS * FOURIER_DIM,
                cw=od_conv_w.shape[2], nw=od_rpb.shape[1] * NA_DIM,
                tq=min(512, s), tk=min(1024, s))
    n_all, n_lat = r // tm, b * s // tm

    x_rows, ctx_rows = x.reshape(b * s, d), ctx.reshape(b * n_ctx, d)
    hh = None
    crow = jnp.concatenate([c, c_ctx[None, :], jnp.zeros((8 - b - 1, d), F32)], axis=0)
    mod_first = _mod_vectors(crow, w_mod, b_mod, 1)
    mod_later = None
    rope_tabs = _rope_tables(dims)
    ev_w_in = ev_w_in.astype(BF16)
    later_ws = None
    tn = dims["tn"]

    for i in range(depth):
        last = i == depth - 1
        n_out = n_lat if last else n_all
        mods = (mod_first[0] if i == 0 else mod_later[i - 1]).reshape(8 * N_MOD, 1, d)
        gain1 = norm_gain[i, 0].reshape(1, d)
        gain2 = norm_gain[i, 1].reshape(1, d)
        j = i // 2
        if i % 2 == 0:
            lam_init = 0.8 - 0.6 * math.exp(-0.3 * i)
            gq = jnp.tile(ev_qk_gain[j, 0], DIFF_V // DIFF_DIM).reshape(1, DIFF_V)
            gk = jnp.tile(ev_qk_gain[j, 1], tn // DIFF_DIM).reshape(1, tn)
            q, kr, v, f = _inproj_even(x_rows if i == 0 else hh, mods, gain1, ev_w_in, j, gk, rope_tabs, dims,
                                       ctx_rows=ctx_rows if i == 0 else None)
            subln = ev_subln_gain[j].reshape(1, DIFF_V)
            first = later_ws is None
            o, cast, side_mods = _diff_attn_latent(
                ev_lambda[j], subln, gq, q, rope_tabs, kr, v, dims, lam_init,
                cast_ws=(w_ffn_in, w_ffn_out, ev_w_out, od_w_in, od_w_out) if first else (),
                mod_args=(crow, w_mod, b_mod, 1) if first and depth > 1 else None)
            if first:
                later_ws, mod_later = cast, side_mods
                w_ffn_in, w_ffn_out, ev_w_out, od_w_in, od_w_out = cast
            fo = _fourier_latent(f, dims)
            if not last:
                o = _diff_attn_ctx(ev_lambda[j], subln, gq, q, kr, v, o, dims, lam_init)
                fo = _fourier_ctx(f, fo, dims)
            if i == 0 and not last:
                hh = _outproj(o, fo, ev_w_out, j, x_rows, mods, n_out, dims, ctx_rows=ctx_rows)
            else:
                hh = _outproj(o, fo, ev_w_out, j, hh, mods, n_out, dims)
        else:
            gq = jnp.tile(od_qk_gain[j, 0], tn // NA_DIM).reshape(1, tn)
            gk = jnp.tile(od_qk_gain[j, 1], tn // NA_DIM).reshape(1, tn)
            cg, q, k, v = _inproj_odd(hh, mods, gain1, od_w_in, j, gq, gk, dims)
            o = _na_latent(q, k, v, _na_bias(od_rpb[j], s // GRID_W), dims)
            if not last:
                o = _na_ctx(q, k, v, o, dims)
            hh = _outproj(cg, o, od_w_out, j, hh, mods, n_out, dims, conv_w=od_conv_w[j])
        hh = _ffn(hh, mods, gain2, w_ffn_in, w_ffn_out, i, n_out, dims)

    return hh.reshape(b, s, d)
```

```python
import functools
import math

import jax
import jax.numpy as jnp
import numpy as np
from jax import lax
from jax.experimental import pallas as pl
from jax.experimental.pallas import tpu as pltpu

GRID_W = 64
EPS = 1e-6
N_MOD = 6

DIFF_HEADS = 12
DIFF_DIM = 64
DIFF_V = 2 * DIFF_DIM
FOURIER_GROUPS = 4
FOURIER_DIM = 128
ROPE_BASE = 10000.0

CONV_WIDTH = 1024
CONV_K = 3
NA_HEADS = 8
NA_DIM = 128
NA_WIN_ROWS = 8
NA_WIN_COLS = 16

F32 = jnp.float32
BF16 = jnp.bfloat16
LOG2E = 1.4426950408889634
NEG_BIG = -1e30
LANES = 128
MXU_DIM = 256
VMEM_LIMIT = 56 * 1024 * 1024
ATTN_SCORE_LEAD = 1


def _cparams(*sem):
    return pltpu.CompilerParams(dimension_semantics=sem, vmem_limit_bytes=VMEM_LIMIT)


def _mm(a, b):
    return jnp.dot(a, b, preferred_element_type=F32)


def _mm_nt(a, b):
    return lax.dot_general(a, b, (((1,), (1,)), ((), ())), preferred_element_type=F32)


def _tile_lanes(x, n):
    return x if n == 1 else jnp.concatenate([x] * n, axis=1)


def _mod_kernel(c_ref, w_ref, b_ref, o_ref):
    c = c_ref[...]
    s = (c * jax.nn.sigmoid(c)).astype(BF16)
    o_ref[0] = _mm(s, w_ref[0].astype(BF16)) + b_ref[0]


def _mod_side_specs(w_mod, first_layer, n_steps, step_of):
    depth, d, nm = w_mod.shape
    n_layers = depth - first_layer
    cw = LANES * pl.cdiv(n_layers * nm // LANES, n_steps)
    while nm % cw:
        cw += LANES
    per_layer = nm // cw
    last = n_layers * per_layer - 1

    def blk(*idx):
        t = jnp.minimum(step_of(*idx), last)
        return t // per_layer, t % per_layer

    return [pl.BlockSpec((8, d), lambda *idx: (0, 0)),
            pl.BlockSpec((1, d, cw), lambda *idx: (first_layer + blk(*idx)[0], 0, blk(*idx)[1])),
            pl.BlockSpec((1, 1, cw), lambda *idx: (first_layer + blk(*idx)[0], 0, blk(*idx)[1])),
            pl.BlockSpec((1, 8, cw), lambda *idx: (blk(*idx)[0], 0, blk(*idx)[1]))]


def _mod_vectors(crow, w_mod, b_mod, n_layers):
    depth, d, nm = w_mod.shape
    tn = 1024 if nm % 1024 == 0 else 512
    return pl.pallas_call(
        _mod_kernel,
        grid=(n_layers, nm // tn),
        in_specs=[
            pl.BlockSpec((8, d), lambda l, j: (0, 0)),
            pl.BlockSpec((1, d, tn), lambda l, j: (l, 0, j)),
            pl.BlockSpec((1, 1, tn), lambda l, j: (l, 0, j)),
        ],
        out_specs=pl.BlockSpec((1, 8, tn), lambda l, j: (l, 0, j)),
        out_shape=jax.ShapeDtypeStruct((n_layers, 8, nm), F32),
        compiler_params=_cparams("parallel", "parallel"),
        name="mod_vectors",
    )(crow, w_mod, b_mod.reshape(depth, 1, nm))


def _norm_mod(x, gain, shift, scale):
    r = lax.rsqrt(jnp.mean(x * x, axis=-1, keepdims=True) + EPS)
    return (x * r) * (gain * (1.0 + scale)) + shift


def _group_rms(x, ones_bd, gain, group):
    x2 = (x * x).astype(BF16)
    w = ones_bd.shape[0]
    parts = [_mm(x2[:, c:c + w], ones_bd) for c in range(0, x.shape[1], w)]
    ms = (parts[0] if len(parts) == 1 else jnp.concatenate(parts, axis=1)) * (1.0 / group)
    return x * lax.rsqrt(ms + EPS) * gain


def _rope(x, cos, s_lo, s_hi):
    q = DIFF_DIM // 4
    outs = []
    for c in range(0, x.shape[1], LANES):
        xs = x[:, c:c + LANES]
        outs.append(xs * cos + pltpu.roll(xs, q, 1) * s_lo + pltpu.roll(xs, LANES - q, 1) * s_hi)
    return jnp.concatenate(outs, axis=1)


def _inproj_even_kernel(h_ref, hc_ref, sh_ref, sc_ref, gain_ref, w_ref, gk_ref, ones_ref, cos_ref, slo_ref, shi_ref,
                        q_ref, kr_ref, v_ref, f_ref, a_scr, *, nq, nv, n_lat_tiles):
    i, j = pl.program_id(0), pl.program_id(1)

    @pl.when(jnp.logical_and(j == 0, i < n_lat_tiles))
    def _():
        a_scr[...] = _norm_mod(h_ref[...], gain_ref[...], sh_ref[0], sc_ref[0]).astype(BF16)

    @pl.when(jnp.logical_and(j == 0, i >= n_lat_tiles))
    def _():
        a_scr[0:hc_ref.shape[0], :] = _norm_mod(hc_ref[...], gain_ref[...], sh_ref[0], sc_ref[0]).astype(BF16)

    acc = _mm(a_scr[...], w_ref[...])

    @pl.when(j < nq)
    def _():
        q_ref[...] = acc.astype(BF16)

    @pl.when(jnp.logical_and(j >= nq, j < 2 * nq))
    def _():
        kn = _group_rms(acc, ones_ref[...], gk_ref[...], DIFF_DIM)
        kr_ref[...] = _rope(kn, cos_ref[...], slo_ref[...], shi_ref[...]).astype(BF16)

    @pl.when(jnp.logical_and(j >= 2 * nq, j < 2 * nq + nv))
    def _():
        v_ref[...] = acc.astype(BF16)

    @pl.when(j >= 2 * nq + nv)
    def _():
        f_ref[...] = acc.astype(BF16)


def _inproj_odd_kernel(h_ref, sh_ref, sc_ref, gain_ref, w_ref, gq_ref, gk_ref, ones_ref,
                       cg_ref, q_ref, k_ref, v_ref, a_scr, *, ncg, nh, qscale):
    j = pl.program_id(1)

    @pl.when(j == 0)
    def _():
        a_scr[...] = _norm_mod(h_ref[...], gain_ref[...], sh_ref[0], sc_ref[0]).astype(BF16)

    acc = _mm(a_scr[...], w_ref[...])

    @pl.when(j < ncg)
    def _():
        cg_ref[...] = acc.astype(BF16)

    @pl.when(jnp.logical_and(j >= ncg, j < ncg + nh))
    def _():
        q_ref[...] = (_group_rms(acc, ones_ref[...], gq_ref[...], NA_DIM) * qscale).astype(BF16)

    @pl.when(jnp.logical_and(j >= ncg + nh, j < ncg + 2 * nh))
    def _():
        k_ref[...] = _group_rms(acc, ones_ref[...], gk_ref[...], NA_DIM).astype(BF16)

    @pl.when(j >= ncg + 2 * nh)
    def _():
        v_ref[...] = acc.astype(BF16)


def _row_specs(tm, d, tiles_per_batch, n_batch):
    def mod_spec(k):
        return pl.BlockSpec((1, 1, d), lambda i, j: (jnp.minimum(i // tiles_per_batch, n_batch) * N_MOD + k, 0, 0))

    return pl.BlockSpec((tm, d), lambda i, j: (i, 0)), mod_spec


def _clamped(tm, tn, lo, n):
    return pl.BlockSpec((tm, tn), lambda i, j: (i, jnp.clip(j - lo, 0, n - 1)))


def _ones_blockdiag(group, size=MXU_DIM):
    idx = np.arange(size) // group
    return jnp.asarray((idx[:, None] == idx[None, :]).astype(np.float32), dtype=BF16)


def _inproj_even(hh, mods, gain, w, layer, gk, rope_tabs, dims, ctx_rows=None):
    d = hh.shape[1]
    r = dims["r"]
    tm, tn = dims["tm_in"], dims["tn"]
    n_lat_tiles = dims["b"] * dims["s"] // tm if ctx_rows is not None else pl.cdiv(r, tm)
    if ctx_rows is None:
        ctx_rows = jnp.zeros((8, d), F32)
    assert ctx_rows.shape[0] <= tm
    ev_q, ev_v, fw = dims["ev_q"], dims["ev_v"], dims["fw"]
    nq, nv, nf = ev_q // tn, ev_v // tn, fw // tn
    nj = 2 * nq + nv + nf
    hspec, mod_spec = _row_specs(tm, d, dims["s"] // tm, dims["b"])
    const = lambda shape: pl.BlockSpec(shape, lambda i, j: (0,) * len(shape))
    tab = pl.BlockSpec((tm, LANES), lambda i, j: (i, 0))
    kern = functools.partial(_inproj_even_kernel, nq=nq, nv=nv, n_lat_tiles=n_lat_tiles)
    return pl.pallas_call(
        kern,
        grid=(pl.cdiv(r, tm), nj),
        in_specs=[pl.BlockSpec((tm, d), lambda i, j: (jnp.minimum(i, n_lat_tiles - 1), 0)),
                  const(ctx_rows.shape),
                  mod_spec(0), mod_spec(1), const((1, d)),
                  pl.BlockSpec((None, d, tn), lambda i, j: (layer, 0, j)),
                  const((1, tn)), const((MXU_DIM, MXU_DIM)), tab, tab, tab],
        out_specs=[_clamped(tm, tn, 0, nq), _clamped(tm, tn, nq, nq),
                   _clamped(tm, tn, 2 * nq, nv), _clamped(tm, tn, 2 * nq + nv, nf)],
        out_shape=[jax.ShapeDtypeStruct((r, ev_q), BF16)] * 2
        + [jax.ShapeDtypeStruct((r, ev_v), BF16), jax.ShapeDtypeStruct((r, fw), BF16)],
        scratch_shapes=[pltpu.VMEM((tm, d), BF16)],
        compiler_params=_cparams("parallel", "arbitrary"),
        name="inproj_even",
    )(hh, ctx_rows, mods, mods, gain, w, gk, _ones_blockdiag(DIFF_DIM), *rope_tabs)


def _inproj_odd(hh, mods, gain, w, layer, gq, gk, dims):
    r, d = hh.shape
    tm, tn = dims["tm_in"], dims["tn"]
    cw, nw = dims["cw"], dims["nw"]
    ncg, nh = 3 * cw // tn, nw // tn
    nj = ncg + 3 * nh
    hspec, mod_spec = _row_specs(tm, d, dims["s"] // tm, dims["b"])
    const = lambda shape: pl.BlockSpec(shape, lambda i, j: (0,) * len(shape))
    kern = functools.partial(_inproj_odd_kernel, ncg=ncg, nh=nh, qscale=NA_DIM ** -0.5 * LOG2E)
    return pl.pallas_call(
        kern,
        grid=(pl.cdiv(r, tm), nj),
        in_specs=[hspec, mod_spec(0), mod_spec(1), const((1, d)),
                  pl.BlockSpec((None, d, tn), lambda i, j: (layer, 0, j)),
                  const((1, tn)), const((1, tn)), const((MXU_DIM, MXU_DIM))],
        out_specs=[_clamped(tm, tn, 0, ncg), _clamped(tm, tn, ncg, nh), _clamped(tm, tn, ncg + nh, nh),
                   _clamped(tm, tn, ncg + 2 * nh, nh)],
        out_shape=[jax.ShapeDtypeStruct((r, 3 * cw), BF16)] + [jax.ShapeDtypeStruct((r, nw), BF16)] * 3,
        scratch_shapes=[pltpu.VMEM((tm, d), BF16)],
        compiler_params=_cparams("parallel", "arbitrary"),
        name="inproj_odd",
    )(hh, mods, mods, gain, w, gq, gk, _ones_blockdiag(NA_DIM))


def _split_components(q):
    lane = lax.broadcasted_iota(jnp.int32, q.shape, 1)
    zero = jnp.zeros_like(q)
    return jnp.concatenate([jnp.where(lane < DIFF_DIM, q, zero), jnp.where(lane >= DIFF_DIM, q, zero)], axis=0)


def _online_softmax(chunks):
    lead = ATTN_SCORE_LEAD
    rows = chunks[0][0].shape[0]
    scores = [_mm_nt(q2, kk) for q2, kk, _ in chunks[:lead]]
    m = jnp.full((rows, LANES), NEG_BIG, F32)
    acc = jnp.zeros((rows, 2 * DIFF_V), F32)
    for c, (_, _, vv) in enumerate(chunks):
        if c + lead < len(chunks):
            scores.append(_mm_nt(chunks[c + lead][0], chunks[c + lead][1]))
        s = scores[c]
        m_new = jnp.maximum(m, jnp.max(s, axis=-1, keepdims=True))
        alpha = jnp.exp2(m - m_new)
        p = jnp.exp2((s - _tile_lanes(m_new, s.shape[1] // LANES)).astype(BF16))
        v_ext = jnp.concatenate([vv, jnp.ones_like(vv)], axis=1)
        acc = _tile_lanes(alpha, 2) * acc + _mm(p, v_ext)
        m = m_new
    return acc


def _diff_attn_kernel(*refs, n_chunks, tk, lam_init, n_cast=0, with_mods=False):
    if n_chunks:
        (lam_ref, g_ref, gq_ref, ones_ref, q_ref, cos_ref, slo_ref, shi_ref,
         kc_ref, vc_ref, k_ref, v_ref) = refs[:12]
        n_side = n_cast + 3 * with_mods
        o_ref = refs[12 + n_side]
        for w_ref, wo_ref in zip(refs[12:12 + n_cast], refs[13 + n_side:]):
            wo_ref[...] = w_ref[...].astype(BF16)
        if with_mods:
            _mod_kernel(*refs[12 + n_cast:12 + n_side], refs[-1])
    else:
        lam_ref, g_ref, gq_ref, ones_ref, q_ref, kc_ref, vc_ref, o_ref = refs
    tq = q_ref.shape[0]
    qn = _group_rms(q_ref[...].astype(F32), ones_ref[...], gq_ref[...], DIFF_DIM) * (DIFF_DIM ** -0.5 * LOG2E)
    chunks = []
    if n_chunks:
        q2 = _split_components(_rope(qn, cos_ref[...], slo_ref[...], shi_ref[...]).astype(BF16))
        chunks = [(q2, k_ref[c * tk:(c + 1) * tk, :], v_ref[c * tk:(c + 1) * tk, :]) for c in range(n_chunks)]
    chunks.append((_split_components(qn.astype(BF16)), kc_ref[...], vc_ref[...]))
    acc = _online_softmax(chunks)

    lv = lam_ref[...]
    lam = (jnp.exp(jnp.sum(lv[0:1] * lv[1:2], axis=-1, keepdims=True))
           - jnp.exp(jnp.sum(lv[2:3] * lv[3:4], axis=-1, keepdims=True)) + lam_init)
    o2 = acc[:, :DIFF_V] / acc[:, DIFF_V:]
    o = o2[:tq] - lam * o2[tq:]
    o = o * lax.rsqrt(jnp.mean(o * o, axis=-1, keepdims=True) + EPS) * g_ref[...] * (1.0 - lam_init)
    o_ref[...] = o.astype(BF16)


def _cast_block_rows(rows, n_steps):
    rb = 16 * pl.cdiv(pl.cdiv(rows, 16), n_steps)
    while rows % rb:
        rb += 16
    return rb


def _diff_attn_latent(lam_vec, subln, gq, q, rope_tabs, kr, v, dims, lam_init, cast_ws=(), mod_args=None):
    b, s, ctx, r = dims["b"], dims["s"], dims["ctx"], dims["r"]
    tq, tk = dims["tq"], dims["tk"]
    nh = q.shape[1] // DIFF_V
    ctx_blk0 = b * s // ctx
    nq = s // tq
    kern = functools.partial(_diff_attn_kernel, n_chunks=s // tk, tk=tk, lam_init=lam_init, n_cast=len(cast_ws),
                             with_mods=mod_args is not None)
    step_of = lambda bi, h, qi: (bi * nh + h) * nq + qi
    w2d = [w.reshape(-1, w.shape[-1]) for w in cast_ws]
    cast_specs = []
    for w in w2d:
        rb = _cast_block_rows(w.shape[0], b * nh * nq)
        nblk = w.shape[0] // rb
        cast_specs.append(pl.BlockSpec(
            (rb, w.shape[1]), lambda bi, h, qi, nblk=nblk: (jnp.minimum(step_of(bi, h, qi), nblk - 1), 0)))
    mod_in_specs, mod_out_specs, mod_out_shapes, mod_in = [], [], [], []
    if mod_args is not None:
        crow, w_mod, b_mod, first_layer = mod_args
        depth, _, nm = w_mod.shape
        *mod_in_specs, mod_out = _mod_side_specs(w_mod, first_layer, b * nh * nq, step_of)
        mod_out_specs = [mod_out]
        mod_out_shapes = [jax.ShapeDtypeStruct((depth - first_layer, 8, nm), F32)]
        mod_in = [crow, w_mod, b_mod.reshape(depth, 1, nm)]
    const = lambda shape: pl.BlockSpec(shape, lambda bi, h, qi: (0, 0))
    tab = pl.BlockSpec((tq, LANES), lambda bi, h, qi: (bi * (s // tq) + qi, 0))
    qspec = pl.BlockSpec((tq, DIFF_V), lambda bi, h, qi: (bi * (s // tq) + qi, h))
    cspec = pl.BlockSpec((ctx, DIFF_V), lambda bi, h, qi: (ctx_blk0 + bi, h))
    kspec = pl.BlockSpec((s, DIFF_V), lambda bi, h, qi: (bi, h))
    out = pl.pallas_call(
        kern,
        grid=(b, nh, nq),
        in_specs=[const((4, DIFF_DIM)), const((1, DIFF_V)), const((1, DIFF_V)), const((DIFF_V, DIFF_V)),
                  qspec, tab, tab, tab, cspec, cspec, kspec, kspec] + cast_specs + mod_in_specs,
        out_specs=[qspec] + cast_specs + mod_out_specs,
        out_shape=[jax.ShapeDtypeStruct((r, nh * DIFF_V), BF16)]
        + [jax.ShapeDtypeStruct(w.shape, BF16) for w in w2d] + mod_out_shapes,
        compiler_params=_cparams("arbitrary", "arbitrary", "arbitrary"),
        name="diff_attn_latent",
    )(lam_vec, subln, gq, _ones_blockdiag(DIFF_DIM, DIFF_V), q, *rope_tabs, kr, v, kr, v, *w2d, *mod_in)
    casts = [o.reshape(w.shape) for o, w in zip(out[1:1 + len(w2d)], cast_ws)]
    return out[0], casts, (out[-1] if mod_args is not None else None)


def _diff_attn_ctx(lam_vec, subln, gq, q, kr, v, o_lat, dims, lam_init):
    b, s, ctx = dims["b"], dims["s"], dims["ctx"]
    nh = q.shape[1] // DIFF_V
    ctx_blk0 = b * s // ctx
    kern = functools.partial(_diff_attn_kernel, n_chunks=0, tk=0, lam_init=lam_init)
    const = lambda shape: pl.BlockSpec(shape, lambda bi, h: (0, 0))
    cspec = pl.BlockSpec((ctx, DIFF_V), lambda bi, h: (ctx_blk0 + bi, h))

    def wrapped(lam_ref, g_ref, gq_ref, ones_ref, q_ref, kc_ref, vc_ref, alias_ref, o_ref):
        del alias_ref
        kern(lam_ref, g_ref, gq_ref, ones_ref, q_ref, kc_ref, vc_ref, o_ref)

    return pl.pallas_call(
        wrapped,
        grid=(b, nh),
        in_specs=[const((4, DIFF_DIM)), const((1, DIFF_V)), const((1, DIFF_V)), const((DIFF_V, DIFF_V)),
                  cspec, cspec, cspec, pl.BlockSpec(memory_space=pl.ANY)],
        out_specs=cspec,
        out_shape=jax.ShapeDtypeStruct(o_lat.shape, BF16),
        input_output_aliases={7: 0},
        compiler_params=_cparams("parallel", "parallel"),
        name="diff_attn_ctx",
    )(lam_vec, subln, gq, _ones_blockdiag(DIFF_DIM, DIFF_V), q, kr, v, o_lat)


def _dft_cs(n):
    k = np.arange(n)
    ang = 2.0 * np.pi * ((k[:, None] * k[None, :]) % n) / n
    return np.cos(ang), np.sin(ang)


def _channel_dft(groups, scale):
    c, s = _dft_cs(FOURIER_DIM)
    eye = np.eye(groups)
    return np.kron(eye, c) * scale, np.kron(eye, s) * scale


def _fourier_rows_kernel(x_ref, w_ref, tc_ref, ts_ref, y_ref, *, cg, fw, nr):
    for c in range(cg):
        y = _mm(w_ref[...], x_ref[:, c * fw:(c + 1) * fw])
        yr, yi = y[:nr], y[nr:]
        tc = _tile_lanes(tc_ref[c], fw // LANES)
        ts = _tile_lanes(ts_ref[c], fw // LANES)
        y_ref[0, 0, :, c * fw:(c + 1) * fw] = (yr * tc + yi * ts).astype(BF16)
        y_ref[0, 1, :, c * fw:(c + 1) * fw] = (yi * tc - yr * ts).astype(BF16)


def _fourier_cols_kernel(y_ref, w2_ref, wc_ref, ws_ref, o_ref, *, ag, fw, nc):
    zr, zi = [], []
    for a in range(ag):
        yy = jnp.concatenate([y_ref[0, 0, a], y_ref[0, 1, a]], axis=0)
        z = _mm(w2_ref[...], yy)
        zr.append(z[:nc].astype(BF16))
        zi.append(z[nc:].astype(BF16))
    out = _mm(jnp.concatenate(zr, axis=0), wc_ref[...]) + _mm(jnp.concatenate(zi, axis=0), ws_ref[...])
    for a in range(ag):
        o_ref[:, a * fw:(a + 1) * fw] = out[a * nc:(a + 1) * nc].astype(BF16)


def _fourier_dense_kernel(x_ref, wc_ref, ws_ref, cn_ref, sn_ref, alias_ref, o_ref):
    del alias_ref
    x = x_ref[...]
    gc = _mm(x, wc_ref[...]).astype(BF16)
    gs = _mm(x, ws_ref[...]).astype(BF16)
    o_ref[...] = (_mm(cn_ref[...], gc) - _mm(sn_ref[...], gs)).astype(BF16)


def _fourier_latent(f, dims):
    b, s, r = dims["b"], dims["s"], dims["r"]
    fw = f.shape[1]
    groups = fw // FOURIER_DIM
    nc = GRID_W
    nr = s // nc
    cg = 4
    ag = 8
    c1, s1 = _dft_cs(nr)
    w1 = jnp.asarray(np.concatenate([c1, -s1], axis=0), dtype=BF16)
    ang = 2.0 * np.pi * (np.arange(nc)[:, None] * np.arange(nr)[None, :]) / s
    tcos = jnp.asarray(np.repeat(np.cos(ang)[:, :, None], LANES, axis=2), dtype=F32)
    tsin = jnp.asarray(np.repeat(np.sin(ang)[:, :, None], LANES, axis=2), dtype=F32)
    x2d = f.reshape(r // nc, nc * fw)
    y = pl.pallas_call(
        functools.partial(_fourier_rows_kernel, cg=cg, fw=fw, nr=nr),
        grid=(b, nc // cg),
        in_specs=[pl.BlockSpec((nr, cg * fw), lambda bi, j: (bi, j)),
                  pl.BlockSpec((2 * nr, nr), lambda bi, j: (0, 0)),
                  pl.BlockSpec((cg, nr, LANES), lambda bi, j: (j, 0, 0)),
                  pl.BlockSpec((cg, nr, LANES), lambda bi, j: (j, 0, 0))],
        out_specs=pl.BlockSpec((1, 2, nr, cg * fw), lambda bi, j: (bi, 0, 0, j)),
        out_shape=jax.ShapeDtypeStruct((b, 2, nr, nc * fw), BF16),
        compiler_params=_cparams("parallel", "parallel"),
        name="fourier_rows",
    )(x2d, w1, tcos, tsin)
    c2, s2 = _dft_cs(nc)
    w2 = jnp.asarray(np.block([[c2, s2], [-s2, c2]]), dtype=BF16)
    wc, ws = _channel_dft(groups, 1.0 / math.sqrt(s * FOURIER_DIM))
    out = pl.pallas_call(
        functools.partial(_fourier_cols_kernel, ag=ag, fw=fw, nc=nc),
        grid=(b, nr // ag),
        in_specs=[pl.BlockSpec((1, 2, ag, nc, fw), lambda bi, j: (bi, 0, j, 0, 0)),
                  pl.BlockSpec((2 * nc, 2 * nc), lambda bi, j: (0, 0)),
                  pl.BlockSpec((fw, fw), lambda bi, j: (0, 0)),
                  pl.BlockSpec((fw, fw), lambda bi, j: (0, 0))],
        out_specs=pl.BlockSpec((nc, ag * fw), lambda bi, j: (bi, j)),
        out_shape=jax.ShapeDtypeStruct((r // nr, nr * fw), BF16),
        compiler_params=_cparams("parallel", "parallel"),
        name="fourier_cols",
    )(y.reshape(b, 2, nr, nc, fw), w2, jnp.asarray(wc, dtype=BF16), jnp.asarray(ws, dtype=BF16))
    return out.reshape(r, fw)


def _fourier_ctx(f, fo, dims):
    b, s, ctx = dims["b"], dims["s"], dims["ctx"]
    fw = f.shape[1]
    groups = fw // FOURIER_DIM
    wc, ws = _channel_dft(groups, 1.0 / math.sqrt(ctx * FOURIER_DIM))
    cn, sn = _dft_cs(ctx)
    blk0 = b * s // ctx
    rows = pl.BlockSpec((ctx, fw), lambda bi: (blk0 + bi, 0))
    const = lambda shape: pl.BlockSpec(shape, lambda bi: (0, 0))
    return pl.pallas_call(
        _fourier_dense_kernel,
        grid=(b,),
        in_specs=[rows, const((fw, fw)), const((fw, fw)), const((ctx, ctx)), const((ctx, ctx)),
                  pl.BlockSpec(memory_space=pl.ANY)],
        out_specs=rows,
        out_shape=jax.ShapeDtypeStruct(fo.shape, BF16),
        input_output_aliases={5: 0},
        compiler_params=_cparams("parallel"),
        name="fourier_ctx",
    )(f, jnp.asarray(wc, dtype=BF16), jnp.asarray(ws, dtype=BF16),
      jnp.asarray(cn, dtype=BF16), jnp.asarray(sn, dtype=BF16), fo)


def _gated_conv(gb_ref, gc_ref, hh_ref, gcp_ref, hhp_ref, gcn_ref, hhn_ref, w_ref, *, n_lat_tiles, s, ctx):
    i = pl.program_id(0)
    tm = gb_ref.shape[0]
    is_lat = i < n_lat_tiles
    seq = jnp.where(is_lat, s, ctx)
    off = lax.rem(i * tm, seq)
    u = gc_ref[...].astype(F32) * hh_ref[...].astype(F32)
    halo = gcp_ref.shape[0]
    u_prev = gcp_ref[halo - 1:halo, :].astype(F32) * hhp_ref[halo - 1:halo, :].astype(F32)
    u_next = gcn_ref[0:1, :].astype(F32) * hhn_ref[0:1, :].astype(F32)
    u_prev = jnp.where(off == 0, 0.0, u_prev)
    u_next = jnp.where(lax.rem(off + tm, seq) == 0, 0.0, u_next)
    row = lax.broadcasted_iota(jnp.int32, (tm, 1), 0)
    dn = jnp.where(row == 0, u_prev, pltpu.roll(u, 1, 0))
    up = jnp.where(row == tm - 1, u_next, pltpu.roll(u, tm - 1, 0))
    if ctx < tm:
        inner_start = functools.reduce(jnp.logical_or, [row == k * ctx for k in range(1, tm // ctx)])
        inner_end = functools.reduce(jnp.logical_or, [row == k * ctx - 1 for k in range(1, tm // ctx)])
        is_ctx = jnp.logical_not(is_lat)
        dn = jnp.where(jnp.logical_and(is_ctx, inner_start), 0.0, dn)
        up = jnp.where(jnp.logical_and(is_ctx, inner_end), 0.0, up)
    w = w_ref[...]
    y = dn * w[0:1] + u * w[1:2] + up * w[2:3]
    return gb_ref[...].astype(F32) * y


def _conv_specs(cw, tm, r):
    halo = 16
    nhb = tm // halo
    last = r // halo - 1
    cur = lambda off: pl.BlockSpec((tm, cw), lambda i: (i, off))
    prev = lambda off: pl.BlockSpec((halo, cw), lambda i: (jnp.maximum(i * nhb - 1, 0), off))
    nxt = lambda off: pl.BlockSpec((halo, cw), lambda i: (jnp.minimum((i + 1) * nhb, last), off))
    return [cur(0), cur(1), cur(2), prev(1), prev(2), nxt(1), nxt(2), pl.BlockSpec((CONV_K, cw), lambda i: (0, 0))]


NA_BLOCK_ROWS = 8
NA_KEY_ROWS = 16
NA_BLOCKS_PER_TRIP = 16


def _na_plan(rows):
    wr = min(NA_WIN_ROWS, rows)
    starts, variants, keys = [], [], {}
    for jb in range(rows // NA_BLOCK_ROWS):
        r0 = jb * NA_BLOCK_ROWS
        ks = int(np.clip(r0 - wr // 2, 0, rows - NA_KEY_ROWS))
        rs = [int(np.clip(r0 + t - wr // 2, 0, rows - wr)) for t in range(NA_BLOCK_ROWS)]
        key = (r0 - ks, tuple(x - ks for x in rs))
        variants.append(keys.setdefault(key, len(keys)))
        starts.append(ks)
    return starts, variants, list(keys), wr


def _na_bias(rpb, rows):
    _, _, keys, wr = _na_plan(rows)
    col = np.arange(GRID_W)
    cs = np.clip(col - NA_WIN_COLS // 2, 0, GRID_W - NA_WIN_COLS)
    kc = np.arange(GRID_W)
    dc = kc[None, :] - col[:, None] + (NA_WIN_COLS - 1)
    col_ok = (kc[None, :] >= cs[:, None]) & (kc[None, :] < cs[:, None] + NA_WIN_COLS)
    n_dr, n_dc = 2 * NA_WIN_ROWS - 1, 2 * NA_WIN_COLS - 1
    col_sel = ((dc[None] == np.arange(n_dc)[:, None, None]) & col_ok[None]).astype(np.float32)
    t = np.arange(NA_BLOCK_ROWS)
    i = np.arange(NA_KEY_ROWS)
    row_sel, row_okv = [], []
    for r0_rel, rs_rel in keys:
        dr = i[None, :] - (r0_rel + t[:, None]) + (NA_WIN_ROWS - 1)
        rs_arr = np.asarray(rs_rel)
        row_ok = (i[None, :] >= rs_arr[:, None]) & (i[None, :] < rs_arr[:, None] + wr)
        row_sel.append(((dr[None] == np.arange(n_dr)[:, None, None]) & row_ok[None]).astype(np.float32))
        row_okv.append(row_ok)
    row_sel = np.stack(row_sel)
    ok = (jnp.asarray(np.stack(row_okv))[:, None, :, None, :, None]
          & jnp.asarray(col_ok)[None, None, None, :, None, :])
    hi = lax.Precision.HIGHEST
    col_exp = jnp.einsum("hrd,dck->hrck", rpb.astype(F32) * LOG2E, jnp.asarray(col_sel), precision=hi)
    vals = jnp.einsum("vrti,hrck->vhtcik", jnp.asarray(row_sel), col_exp, precision=hi)
    vals = jnp.where(ok, vals, NEG_BIG)
    return vals.reshape(len(keys), rpb.shape[0], NA_BLOCK_ROWS * GRID_W, NA_KEY_ROWS * GRID_W)


def _na_kernel(q_ref, k_ref, v_ref, kc_ref, vc_ref, bias_ref, o_ref, *, starts, variants):
    qb = NA_BLOCK_ROWS * GRID_W
    kb = NA_KEY_ROWS * GRID_W

    def lookup(table, jb):
        out = jnp.int32(table[0])
        for idx in range(1, len(table)):
            out = jnp.where(jb == idx, jnp.int32(table[idx]), out)
        return out

    def block(jb):
        q0 = pl.multiple_of(jb * qb, qb)
        k0 = pl.multiple_of(lookup(starts, jb) * GRID_W, GRID_W)
        q = q_ref[pl.ds(q0, qb), :]
        s_nb = _mm_nt(q, k_ref[pl.ds(k0, kb), :]) + bias_ref[lookup(variants, jb), 0]
        s_cx = _mm_nt(q, kc_ref[...])
        m = jnp.maximum(jnp.max(s_nb, axis=-1, keepdims=True), jnp.max(s_cx, axis=-1, keepdims=True))
        p_nb = jnp.exp2((s_nb - m).astype(BF16))
        p_cx = jnp.exp2((s_cx - m).astype(BF16))
        v_nb, v_cx = v_ref[pl.ds(k0, kb), :], vc_ref[...]
        o = (_mm(p_nb, jnp.concatenate([v_nb, jnp.ones_like(v_nb)], axis=1))
             + _mm(p_cx, jnp.concatenate([v_cx, jnp.ones_like(v_cx)], axis=1)))
        o_ref[pl.ds(q0, qb), :] = (o[:, :NA_DIM] / o[:, NA_DIM:]).astype(BF16)

    per_trip = math.gcd(NA_BLOCKS_PER_TRIP, len(starts))

    def body(t, carry):
        for u in range(per_trip):
            block(t * per_trip + u)
        return carry

    lax.fori_loop(0, len(starts) // per_trip, body, 0)


def _na_ctx_kernel(q_ref, kc_ref, vc_ref, alias_ref, o_ref):
    del alias_ref
    s = _mm_nt(q_ref[...], kc_ref[...])
    p = jnp.exp2(s - jnp.max(s, axis=-1, keepdims=True))
    o = _mm(p.astype(BF16), vc_ref[...]) / jnp.sum(p, axis=-1, keepdims=True)
    o_ref[...] = o.astype(BF16)


def _na_latent(q, k, v, bias, dims):
    b, s, ctx, r = dims["b"], dims["s"], dims["ctx"], dims["r"]
    nh = q.shape[1] // NA_DIM
    starts, variants, keys, _ = _na_plan(s // GRID_W)
    blk0 = b * s // ctx
    lat = pl.BlockSpec((s, NA_DIM), lambda bi, h: (bi, h))
    cx = pl.BlockSpec((ctx, NA_DIM), lambda bi, h: (blk0 + bi, h))
    nvar = len(keys)
    return pl.pallas_call(
        functools.partial(_na_kernel, starts=tuple(starts), variants=tuple(variants)),
        grid=(b, nh),
        in_specs=[lat, lat, lat, cx, cx,
                  pl.BlockSpec((nvar, 1) + bias.shape[2:], lambda bi, h: (0, h, 0, 0))],
        out_specs=lat,
        out_shape=jax.ShapeDtypeStruct((r, nh * NA_DIM), BF16),
        compiler_params=_cparams("parallel", "arbitrary"),
        name="na_latent",
    )(q, k, v, k, v, bias)


def _na_ctx(q, k, v, o_lat, dims):
    b, s, ctx = dims["b"], dims["s"], dims["ctx"]
    nh = q.shape[1] // NA_DIM
    blk0 = b * s // ctx
    cx = pl.BlockSpec((ctx, NA_DIM), lambda bi, h: (blk0 + bi, h))
    return pl.pallas_call(
        _na_ctx_kernel,
        grid=(b, nh),
        in_specs=[cx, cx, cx, pl.BlockSpec(memory_space=pl.ANY)],
        out_specs=cx,
        out_shape=jax.ShapeDtypeStruct(o_lat.shape, BF16),
        input_output_aliases={3: 0},
        compiler_params=_cparams("parallel", "parallel"),
        name="na_ctx",
    )(q, k, v, o_lat)


def _outproj_kernel(xa_ref, xb_ref, wa_ref, wb_ref, h_ref, gate_ref, o_ref):
    y = _mm(xa_ref[...], wa_ref[...]) + _mm(xb_ref[...], wb_ref[...])
    o_ref[...] = h_ref[...] + gate_ref[0] * y


def _outproj_split_kernel(xa_ref, xb_ref, wa_ref, wb_ref, h_ref, hc_ref, gate_ref, o_ref, *, n_lat_tiles):
    i = pl.program_id(0)
    y = gate_ref[0] * (_mm(xa_ref[...], wa_ref[...]) + _mm(xb_ref[...], wb_ref[...]))

    @pl.when(i < n_lat_tiles)
    def _():
        o_ref[...] = h_ref[...] + y

    @pl.when(i >= n_lat_tiles)
    def _():
        o_ref[...] = hc_ref[...] + y


def _outproj_conv_kernel(*refs, conv_args):
    conv_refs, (xb_ref, wa_ref, wb_ref, h_ref, gate_ref, o_ref) = refs[:8], refs[8:]
    xa = _gated_conv(*conv_refs, **conv_args).astype(BF16)
    y = _mm(xa, wa_ref[...]) + _mm(xb_ref[...], wb_ref[...])
    o_ref[...] = h_ref[...] + gate_ref[0] * y


def _outproj(xa, xb, w, layer, hh, mods, n_tiles, dims, conv_w=None, ctx_rows=None):
    d = hh.shape[1]
    r = dims["r"]
    tm = dims["tm"]
    if ctx_rows is not None:
        assert conv_w is None
        n_lat = hh.shape[0] // tm
        kern = functools.partial(_outproj_split_kernel, n_lat_tiles=n_lat)
        hspec, mod_spec = _row_specs(tm, d, dims["tpb"], dims["b"])
        one = lambda spec: pl.BlockSpec(spec.block_shape, lambda i: spec.index_map(i, 0))
        ka, kb = xa.shape[1], xb.shape[1]
        return pl.pallas_call(
            kern,
            grid=(n_tiles,),
            in_specs=[pl.BlockSpec((tm, ka), lambda i: (i, 0)), pl.BlockSpec((tm, kb), lambda i: (i, 0)),
                      pl.BlockSpec((None, ka, d), lambda i: (layer, 0, 0)),
                      pl.BlockSpec((None, kb, d), lambda i: (layer, ka // kb, 0)),
                      pl.BlockSpec((tm, d), lambda i: (jnp.minimum(i, n_lat - 1), 0)),
                      pl.BlockSpec((tm, d), lambda i: (jnp.maximum(i - n_lat, 0), 0)),
                      one(mod_spec(2))],
            out_specs=one(hspec),
            out_shape=jax.ShapeDtypeStruct((r, d), F32),
            compiler_params=_cparams("parallel"),
            name="outproj",
        )(xa, xb, w, w, hh, ctx_rows, mods)
    kb = xb.shape[1]
    ka = w.shape[1] - kb
    assert ka % kb == 0
    hspec, mod_spec = _row_specs(tm, d, dims["tpb"], dims["b"])
    one = lambda spec: pl.BlockSpec(spec.block_shape, lambda i: spec.index_map(i, 0))
    if conv_w is None:
        kern, xa_specs, xa_args = _outproj_kernel, [pl.BlockSpec((tm, ka), lambda i: (i, 0))], [xa]
    else:
        conv_args = dict(n_lat_tiles=dims["b"] * dims["tpb"], s=dims["s"], ctx=dims["ctx"])
        kern = functools.partial(_outproj_conv_kernel, conv_args=conv_args)
        xa_specs, xa_args = _conv_specs(ka, tm, r), [xa] * 7 + [conv_w]
    return pl.pallas_call(
        kern,
        grid=(n_tiles,),
        in_specs=xa_specs + [pl.BlockSpec((tm, kb), lambda i: (i, 0)),
                             pl.BlockSpec((None, ka, d), lambda i: (layer, 0, 0)),
                             pl.BlockSpec((None, kb, d), lambda i: (layer, ka // kb, 0)),
                             one(hspec), one(mod_spec(2))],
        out_specs=one(hspec),
        out_shape=jax.ShapeDtypeStruct((r, d), F32),
        input_output_aliases={len(xa_specs) + 3: 0},
        compiler_params=_cparams("parallel"),
        name="outproj",
    )(*xa_args, xb, w, w, hh, mods)


def _ffn_kernel(h_ref, sh_ref, sc_ref, gate_ref, gain_ref, wg_ref, wv_ref, wo_ref, o_ref, u_scr, acc_scr):
    j = pl.program_id(1)

    @pl.when(j == 0)
    def _():
        u_scr[...] = _norm_mod(h_ref[...], gain_ref[...], sh_ref[0], sc_ref[0]).astype(BF16)
        acc_scr[...] = jnp.zeros(acc_scr.shape, F32)

    u = u_scr[...]
    g = _mm(u, wg_ref[...])
    val = _mm(u, wv_ref[...])
    act = (g * jax.nn.sigmoid(g) * val).astype(BF16)
    acc_scr[...] += _mm(act, wo_ref[...])

    @pl.when(j == pl.num_programs(1) - 1)
    def _():
        o_ref[...] = h_ref[...] + gate_ref[0] * acc_scr[...]


def _ffn(hh, mods, gain, w_in, w_out, layer, n_tiles, dims):
    r, d = hh.shape
    tm = dims["tm"]
    hid = w_out.shape[1]
    th = 512 if hid % 512 == 0 else 256
    nj = hid // th
    hspec, mod_spec = _row_specs(tm, d, dims["tpb"], dims["b"])
    in_place = n_tiles * tm == r
    return pl.pallas_call(
        _ffn_kernel,
        grid=(n_tiles, nj),
        in_specs=[hspec, mod_spec(3), mod_spec(4), mod_spec(5), pl.BlockSpec((1, d), lambda i, j: (0, 0)),
                  pl.BlockSpec((None, d, th), lambda i, j: (layer, 0, j)),
                  pl.BlockSpec((None, d, th), lambda i, j: (layer, 0, nj + j)),
                  pl.BlockSpec((None, th, d), lambda i, j: (layer, j, 0))],
        out_specs=hspec,
        out_shape=jax.ShapeDtypeStruct((n_tiles * tm, d), F32),
        input_output_aliases={0: 0} if in_place else {},
        scratch_shapes=[pltpu.VMEM((tm, d), BF16), pltpu.VMEM((tm, d), F32)],
        compiler_params=_cparams("parallel", "arbitrary"),
        name="ffn",
    )(hh, mods, mods, mods, gain, w_in, w_in, w_out)


def _rope_tables(dims):
    s, b, ctx = dims["s"], dims["b"], dims["ctx"]
    t = np.arange(s)
    n_freq = DIFF_DIM // 4
    inv = (ROPE_BASE ** (-np.arange(n_freq, dtype=np.float32) / n_freq)).astype(np.float32)
    lane = np.arange(LANES) % DIFF_DIM
    chunk = lane // n_freq
    pos = np.where((chunk // 2)[None, :] == 0, (t // GRID_W)[:, None], (t % GRID_W)[:, None]).astype(np.float32)
    ang = pos * inv[lane % n_freq][None, :]
    cos, sin = np.cos(ang), np.sin(ang)
    s_lo = np.where((chunk % 2)[None, :] == 1, sin, 0.0)
    s_hi = np.where((chunk % 2)[None, :] == 0, -sin, 0.0)
    pad = b * ctx

    def full(tab, fill):
        return jnp.asarray(np.concatenate([np.tile(tab, (b, 1)), np.full((pad, LANES), fill)], axis=0), dtype=F32)

    return full(cos, 1.0), full(s_lo, 0.0), full(s_hi, 0.0)


def kernel(x, c, ctx, c_ctx, w_mod, b_mod, norm_gain, w_ffn_in, w_ffn_out, ev_w_in, ev_w_out, ev_qk_gain,
           ev_lambda, ev_subln_gain, od_w_in, od_w_out, od_qk_gain, od_conv_w, od_rpb):
    b, s, d = x.shape
    n_ctx = ctx.shape[1]
    depth = w_mod.shape[0]
    tm = b * n_ctx
    r = b * s + b * n_ctx
    assert s % tm == 0 and s % GRID_W == 0
    ev_v = ev_w_out.shape[1] - FOURIER_GROUPS * FOURIER_DIM
    dims = dict(b=b, s=s, ctx=n_ctx, r=r, tm=tm, tm_in=min(1024, s), tn=512, tpb=s // tm,
                ev_q=ev_v // DIFF_V * 2 * DIFF_DIM, ev_v=ev_v, fw=FOURIER_GROUPS * FOURIER_DIM,
                cw=od_conv_w.shape[2], nw=od_rpb.shape[1] * NA_DIM,
                tq=min(512, s), tk=min(1024, s))
    n_all, n_lat = r // tm, b * s // tm

    x_rows, ctx_rows = x.reshape(b * s, d), ctx.reshape(b * n_ctx, d)
    hh = None
    crow = jnp.concatenate([c, c_ctx[None, :], jnp.zeros((8 - b - 1, d), F32)], axis=0)
    mod_first = _mod_vectors(crow, w_mod, b_mod, 1)
    mod_later = None
    rope_tabs = _rope_tables(dims)
    ev_w_in = ev_w_in.astype(BF16)
    later_ws = None
    tn = dims["tn"]

    for i in range(depth):
        last = i == depth - 1
        n_out = n_lat if last else n_all
        mods = (mod_first[0] if i == 0 else mod_later[i - 1]).reshape(8 * N_MOD, 1, d)
        gain1 = norm_gain[i, 0].reshape(1, d)
        gain2 = norm_gain[i, 1].reshape(1, d)
        j = i // 2
        if i % 2 == 0:
            lam_init = 0.8 - 0.6 * math.exp(-0.3 * i)
            gq = jnp.tile(ev_qk_gain[j, 0], DIFF_V // DIFF_DIM).reshape(1, DIFF_V)
            gk = jnp.tile(ev_qk_gain[j, 1], tn // DIFF_DIM).reshape(1, tn)
            q, kr, v, f = _inproj_even(x_rows if i == 0 else hh, mods, gain1, ev_w_in, j, gk, rope_tabs, dims,
                                       ctx_rows=ctx_rows if i == 0 else None)
            subln = ev_subln_gain[j].reshape(1, DIFF_V)
            first = later_ws is None
            o, cast, side_mods = _diff_attn_latent(
                ev_lambda[j], subln, gq, q, rope_tabs, kr, v, dims, lam_init,
                cast_ws=(w_ffn_in, w_ffn_out, ev_w_out, od_w_in, od_w_out) if first else (),
                mod_args=(crow, w_mod, b_mod, 1) if first and depth > 1 else None)
            if first:
                later_ws, mod_later = cast, side_mods
                w_ffn_in, w_ffn_out, ev_w_out, od_w_in, od_w_out = cast
            fo = _fourier_latent(f, dims)
            if not last:
                o = _diff_attn_ctx(ev_lambda[j], subln, gq, q, kr, v, o, dims, lam_init)
                fo = _fourier_ctx(f, fo, dims)
            if i == 0 and not last:
                hh = _outproj(o, fo, ev_w_out, j, x_rows, mods, n_out, dims, ctx_rows=ctx_rows)
            else:
                hh = _outproj(o, fo, ev_w_out, j, hh, mods, n_out, dims)
        else:
            gq = jnp.tile(od_qk_gain[j, 0], tn // NA_DIM).reshape(1, tn)
            gk = jnp.tile(od_qk_gain[j, 1], tn // NA_DIM).reshape(1, tn)
            cg, q, k, v = _inproj_odd(hh, mods, gain1, od_w_in, j, gq, gk, dims)
            o = _na_latent(q, k, v, _na_bias(od_rpb[j], s // GRID_W), dims)
            if not last:
                o = _na_ctx(q, k, v, o, dims)
            hh = _outproj(cg, o, od_w_out, j, hh, mods, n_out, dims, conv_w=od_conv_w[j])
        hh = _ffn(hh, mods, gain2, w_ffn_in, w_ffn_out, i, n_out, dims)

    return hh.reshape(b, s, d)
```
